```python
import jax, jax.numpy as jnp
from jax import lax
import numpy as np

D_MODEL = 1024
BATCH = 8
SEQ = 4096
DEPTH = 2

CHUNK = 64
NORM_EPS = 1e-5

RWKV_HEAD_DIM = 64
RWKV_WIDTH = D_MODEL
RWKV_HEADS = RWKV_WIDTH // RWKV_HEAD_DIM
RWKV_DECAY_RANK = 64
RWKV_ICL_RANK = 64
RWKV_GATE_RANK = 160
RWKV_GN_EPS = 64e-5

HGRN_HEAD_DIM = 128
HGRN_WIDTH = D_MODEL
HGRN_HEADS = HGRN_WIDTH // HGRN_HEAD_DIM

SSM_WIDTH = 2 * D_MODEL
SSM_HEAD_DIM = 64
SSM_HEADS = SSM_WIDTH // SSM_HEAD_DIM
SSM_GROUPS = 4
SSM_HEADS_PER_GROUP = SSM_HEADS // SSM_GROUPS
SSM_STATE = 128
SSM_CONV_WIDTH = 4
SSM_CONV_DIM = SSM_WIDTH + 2 * SSM_GROUPS * SSM_STATE

N_BRANCHES = 3
FFN_HIDDEN = ((8 * D_MODEL + 3 * 256 - 1) // (3 * 256)) * 256

RWKV_COLS = 3 * RWKV_WIDTH + RWKV_DECAY_RANK + RWKV_ICL_RANK + RWKV_GATE_RANK
HGRN_COLS = 4 * HGRN_WIDTH
SSM_COLS = SSM_WIDTH + SSM_CONV_DIM + SSM_HEADS
GATE_COLS = N_BRANCHES * D_MODEL
OFF_HGRN = RWKV_COLS
OFF_SSM = OFF_HGRN + HGRN_COLS
OFF_GATE = OFF_SSM + SSM_COLS
IN_COLS = OFF_GATE + GATE_COLS
BRANCH_ROWS = RWKV_WIDTH + HGRN_WIDTH + SSM_WIDTH

kernel_name = 'hybrid_rwkv7_hgrn2_mamba2_gated_merge'


def rmsnorm(x, gain):
    xf = x.astype(jnp.float32)
    y = xf * lax.rsqrt(jnp.mean(xf * xf, axis=-1, keepdims=True) + NORM_EPS)
    return (y * gain.astype(jnp.float32)).astype(x.dtype)


def to_chunks(t):
    b, s = t.shape[:2]
    return jnp.swapaxes(t.reshape(b, s // CHUNK, CHUNK, *t.shape[2:]), 0, 1)


def from_chunks(t):
    n, b = t.shape[:2]
    return jnp.swapaxes(t, 0, 1).reshape(b, n * CHUNK, *t.shape[3:])


def causal_mask():
    return jnp.tril(jnp.ones((CHUNK, CHUNK), dtype=bool))


def rwkv7_mixer(u, mu, w0, w_up, a0, a_up, g_up, k_k, k_a, r_k, gn_w, gn_b):
    b, s, _ = u.shape
    f32 = jnp.float32
    hd = (RWKV_HEADS, RWKV_HEAD_DIM)
    u_prev = jnp.pad(u, ((0, 0), (1, 0), (0, 0)))[:, :-1]
    u = u + (u_prev - u) * mu
    bounds = np.cumsum([RWKV_WIDTH, RWKV_WIDTH, RWKV_WIDTH, RWKV_DECAY_RANK, RWKV_ICL_RANK]).tolist()
    r, k, v, xw, xa, xg = jnp.split(u, bounds, axis=-1)
    log_w = -jnp.exp(-jax.nn.softplus(-(w0 + jnp.tanh(xw) @ w_up)) - 0.5)
    a = jax.nn.sigmoid(a0 + xa @ a_up)
    g = jax.nn.sigmoid(xg) @ g_up
    heads = lambda t: t.astype(f32).reshape(b, s, *hd)
    r, k, v, a, log_w = heads(r), heads(k), heads(v), heads(a), heads(log_w)
    kk = k * k_k.astype(f32).reshape(hd)
    kk = kk / jnp.maximum(jnp.sqrt(jnp.sum(kk * kk, axis=-1, keepdims=True)), 1e-12)
    k = k * (1.0 + (a - 1.0) * k_a.astype(f32).reshape(hd))

    def step(state, inp):
        r_t, w_t, k_t, v_t, ka_t, kb_t = inp
        sa = jnp.einsum('bhvk,bhk->bhv', state, ka_t)
        state = (state * w_t[:, :, None, :] + sa[..., None] * kb_t[:, :, None, :]
                 + v_t[..., None] * k_t[:, :, None, :])
        return state, jnp.einsum('bhvk,bhk->bhv', state, r_t)

    xs = tuple(jnp.moveaxis(t, 1, 0) for t in (r, jnp.exp(log_w), k, v, -kk, kk * a))
    state0 = jnp.zeros((b, RWKV_HEADS, RWKV_HEAD_DIM, RWKV_HEAD_DIM), f32)
    _, o = lax.scan(step, state0, xs)
    o = jnp.moveaxis(o, 0, 1)
    mean = jnp.mean(o, axis=-1, keepdims=True)
    var = jnp.mean(jnp.square(o - mean), axis=-1, keepdims=True)
    o = (o - mean) * lax.rsqrt(var + RWKV_GN_EPS) * gn_w.astype(f32).reshape(hd) + gn_b.astype(f32).reshape(hd)
    o = o + jnp.sum(r * k * r_k.astype(f32), axis=-1, keepdims=True) * v
    return o.reshape(b, s, RWKV_WIDTH).astype(u.dtype) * g


def hgrn2_mixer(u, lb, gn_w):
    b, s, _ = u.shape
    f32 = jnp.float32
    hd = (HGRN_HEADS, HGRN_HEAD_DIM)
    q, f_pre, i, g = jnp.split(u, 4, axis=-1)
    heads = lambda t: t.astype(f32).reshape(b, s, *hd)
    lb = lb.astype(f32).reshape(hd)
    log_f = jnp.logaddexp(jnp.log(lb), jnp.log1p(-lb) + jax.nn.log_sigmoid(heads(f_pre)))
    k = -jnp.expm1(log_f)
    q = jax.nn.silu(heads(q))
    v = heads(i)
    mask = causal_mask()[None, :, :, None, None]

    def step(state, inp):
        q_c, k_c, v_c, lf_c = inp
        cum = jnp.cumsum(lf_c, axis=1)
        decay = jnp.exp(jnp.where(mask, cum[:, :, None] - cum[:, None, :], -jnp.inf))
        scores = jnp.einsum('bthd,btshd,bshd->bths', q_c, decay, k_c)
        o = (jnp.einsum('bths,bshv->bthv', scores, v_c)
             + jnp.einsum('bthd,bhdv->bthv', q_c * jnp.exp(cum), state))
        last = cum[:, -1]
        state = (jnp.exp(last)[..., None] * state
                 + jnp.einsum('bshd,bshv->bhdv', k_c * jnp.exp(last[:, None] - cum), v_c))
        return state, o

    state0 = jnp.zeros((b, HGRN_HEADS, HGRN_HEAD_DIM, HGRN_HEAD_DIM), f32)
    _, o = lax.scan(step, state0, tuple(to_chunks(t) for t in (q, k, v, log_f)))
    o = from_chunks(o)
    o = o * lax.rsqrt(jnp.mean(o * o, axis=-1, keepdims=True) + NORM_EPS) * gn_w.astype(f32).reshape(hd)
    return o.reshape(b, s, HGRN_WIDTH).astype(u.dtype) * jax.nn.sigmoid(g)


def mamba2_mixer(u, conv_w, conv_b, dt_bias, a_log, d_skip, gn_w):
    b, s, _ = u.shape
    f32 = jnp.float32
    gh = (SSM_GROUPS, SSM_HEADS_PER_GROUP)
    z, xbc, dt = jnp.split(u, [SSM_WIDTH, SSM_WIDTH + SSM_CONV_DIM], axis=-1)
    xpad = jnp.pad(xbc, ((0, 0), (SSM_CONV_WIDTH - 1, 0), (0, 0)))
    conv = conv_b + sum(xpad[:, j:j + s] * conv_w[:, j] for j in range(SSM_CONV_WIDTH))
    xbc = jax.nn.silu(conv)
    x, bm, cm = jnp.split(xbc, [SSM_WIDTH, SSM_WIDTH + SSM_GROUPS * SSM_STATE], axis=-1)
    x = x.astype(f32).reshape(b, s, *gh, SSM_HEAD_DIM)
    bm = bm.astype(f32).reshape(b, s, SSM_GROUPS, SSM_STATE)
    cm = cm.astype(f32).reshape(b, s, SSM_GROUPS, SSM_STATE)
    dt = jax.nn.softplus(dt.astype(f32) + dt_bias.astype(f32)).reshape(b, s, *gh)
    log_a = dt * (-jnp.exp(a_log.astype(f32))).reshape(gh)
    mask = causal_mask()[None, :, :, None, None]

    def step(state, inp):
        x_c, b_c, c_c, dt_c, la_c = inp
        cum = jnp.cumsum(la_c, axis=1)
        decay = jnp.exp(jnp.where(mask, cum[:, :, None] - cum[:, None, :], -jnp.inf))
        cb = jnp.einsum('btgn,bsgn->btsg', c_c, b_c)
        y = jnp.einsum('btsg,btsgh,bsgh,bsghp->btghp', cb, decay, dt_c, x_c)
        y = y + jnp.einsum('btgn,bghpn->btghp', c_c, state) * jnp.exp(cum)[..., None]
        last = cum[:, -1]
        w_s = jnp.exp(last[:, None] - cum) * dt_c
        state = (jnp.exp(last)[..., None, None] * state
                 + jnp.einsum('bsgn,bsgh,bsghp->bghpn', b_c, w_s, x_c))
        return state, y

    state0 = jnp.zeros((b, *gh, SSM_HEAD_DIM, SSM_STATE), f32)
    _, y = lax.scan(step, state0, tuple(to_chunks(t) for t in (x, bm, cm, dt, log_a)))
    y = from_chunks(y) + d_skip.astype(f32).reshape(*gh, 1) * x
    y = y.reshape(b, s, SSM_WIDTH) * jax.nn.silu(z.astype(f32))
    y = y.reshape(b, s, SSM_GROUPS, SSM_WIDTH // SSM_GROUPS)
    y = y * lax.rsqrt(jnp.mean(y * y, axis=-1, keepdims=True) + NORM_EPS)
    return (y.reshape(b, s, SSM_WIDTH) * gn_w.astype(f32)).astype(u.dtype)


def setup_inputs(seed: int = 0) -> dict:
    key = jax.random.key(seed)
    ks = iter(jax.random.split(key, 40))
    L, D = DEPTH, D_MODEL
    nrm = lambda shape, scale: scale * jax.random.normal(next(ks), shape, jnp.float32)
    unif = lambda shape, lo, hi: jax.random.uniform(next(ks), shape, jnp.float32, lo, hi)
    x = nrm((BATCH, SEQ, D), 1.0)
    norm_mix = 1.0 + nrm((L, D), 0.02)
    w_in = nrm((L, D, IN_COLS), D ** -0.5)
    rwkv_mu = unif((L, RWKV_COLS), 0.0, 1.0)
    rwkv_w0 = unif((L, RWKV_WIDTH), -4.0, 0.0)
    rwkv_w_up = nrm((L, RWKV_DECAY_RANK, RWKV_WIDTH), 0.5 * RWKV_DECAY_RANK ** -0.5)
    rwkv_a0 = nrm((L, RWKV_WIDTH), 0.5)
    rwkv_a_up = nrm((L, RWKV_ICL_RANK, RWKV_WIDTH), 0.5 * RWKV_ICL_RANK ** -0.5)
    rwkv_g_up = nrm((L, RWKV_GATE_RANK, RWKV_WIDTH), RWKV_GATE_RANK ** -0.5)
    rwkv_k_k = 0.85 + nrm((L, RWKV_WIDTH), 0.02)
    rwkv_k_a = 1.0 + nrm((L, RWKV_WIDTH), 0.02)
    rwkv_r_k = nrm((L, RWKV_HEADS, RWKV_HEAD_DIM), 0.1)
    rwkv_gn_w = 1.0 + nrm((L, RWKV_WIDTH), 0.02)
    rwkv_gn_b = nrm((L, RWKV_WIDTH), 0.02)
    hgrn_lb_logits = nrm((L, HGRN_WIDTH), 0.5)
    hgrn_gn_w = 1.0 + nrm((L, HGRN_WIDTH), 0.02)
    ssm_conv_w = nrm((L, SSM_CONV_DIM, SSM_CONV_WIDTH), SSM_CONV_WIDTH ** -0.5)
    ssm_conv_b = nrm((L, SSM_CONV_DIM), 0.02)
    dt0 = jnp.exp(unif((L, SSM_HEADS), float(np.log(1e-3)), float(np.log(1e-1))))
    ssm_dt_bias = dt0 + jnp.log(-jnp.expm1(-dt0))
    ssm_a_log = jnp.log(unif((L, SSM_HEADS), 1.0, 16.0))
    ssm_d = 1.0 + nrm((L, SSM_HEADS), 0.1)
    ssm_gn_w = 1.0 + nrm((L, SSM_WIDTH), 0.02)
    w_branch = jnp.concatenate([nrm((L, RWKV_WIDTH, D), RWKV_WIDTH ** -0.5),
                                nrm((L, HGRN_WIDTH, D), HGRN_WIDTH ** -0.5),
                                nrm((L, SSM_WIDTH, D), SSM_WIDTH ** -0.5)], axis=1)
    w_out = nrm((L, D, D), D ** -0.5)
    norm_ffn = 1.0 + nrm((L, D), 0.02)
    w_ffn_in = nrm((L, D, 2 * FFN_HIDDEN), D ** -0.5)
    w_ffn_out = nrm((L, FFN_HIDDEN, D), FFN_HIDDEN ** -0.5)
    norm_final = 1.0 + nrm((D,), 0.02)
    return {'x': x, 'norm_mix': norm_mix, 'w_in': w_in,
            'rwkv_mu': rwkv_mu, 'rwkv_w0': rwkv_w0, 'rwkv_w_up': rwkv_w_up, 'rwkv_a0': rwkv_a0,
            'rwkv_a_up': rwkv_a_up, 'rwkv_g_up': rwkv_g_up, 'rwkv_k_k': rwkv_k_k, 'rwkv_k_a': rwkv_k_a,
            'rwkv_r_k': rwkv_r_k, 'rwkv_gn_w': rwkv_gn_w, 'rwkv_gn_b': rwkv_gn_b,
            'hgrn_lb_logits': hgrn_lb_logits, 'hgrn_gn_w': hgrn_gn_w,
            'ssm_conv_w': ssm_conv_w, 'ssm_conv_b': ssm_conv_b, 'ssm_dt_bias': ssm_dt_bias,
            'ssm_a_log': ssm_a_log, 'ssm_d': ssm_d, 'ssm_gn_w': ssm_gn_w,
            'w_branch': w_branch, 'w_out': w_out, 'norm_ffn': norm_ffn,
            'w_ffn_in': w_ffn_in, 'w_ffn_out': w_ffn_out, 'norm_final': norm_final}


def reference(x, norm_mix, w_in, rwkv_mu, rwkv_w0, rwkv_w_up, rwkv_a0, rwkv_a_up, rwkv_g_up,
              rwkv_k_k, rwkv_k_a, rwkv_r_k, rwkv_gn_w, rwkv_gn_b, hgrn_lb_logits, hgrn_gn_w,
              ssm_conv_w, ssm_conv_b, ssm_dt_bias, ssm_a_log, ssm_d, ssm_gn_w,
              w_branch, w_out, norm_ffn, w_ffn_in, w_ffn_out, norm_final):
    b, s, _ = x.shape
    cs = jnp.cumsum(jax.nn.softmax(hgrn_lb_logits.astype(jnp.float32), axis=0), axis=0)
    lbs = cs - cs[:1]
    for l in range(DEPTH):
        h = rmsnorm(x, norm_mix[l])
        wi = w_in[l]
        y_a = rwkv7_mixer(h @ wi[:, :OFF_HGRN], rwkv_mu[l], rwkv_w0[l], rwkv_w_up[l], rwkv_a0[l],
                          rwkv_a_up[l], rwkv_g_up[l], rwkv_k_k[l], rwkv_k_a[l], rwkv_r_k[l],
                          rwkv_gn_w[l], rwkv_gn_b[l])
        y_b = hgrn2_mixer(h @ wi[:, OFF_HGRN:OFF_SSM], lbs[l], hgrn_gn_w[l])
        y_c = mamba2_mixer(h @ wi[:, OFF_SSM:OFF_GATE], ssm_conv_w[l], ssm_conv_b[l], ssm_dt_bias[l],
                           ssm_a_log[l], ssm_d[l], ssm_gn_w[l])
        gates = jax.nn.sigmoid(h @ wi[:, OFF_GATE:]).reshape(b, s, N_BRANCHES, D_MODEL)
        wb = w_branch[l]
        merged = (gates[:, :, 0] * (y_a @ wb[:RWKV_WIDTH])
                  + gates[:, :, 1] * (y_b @ wb[RWKV_WIDTH:RWKV_WIDTH + HGRN_WIDTH])
                  + gates[:, :, 2] * (y_c @ wb[RWKV_WIDTH + HGRN_WIDTH:]))
        x = x + merged @ w_out[l]
        h = rmsnorm(x, norm_ffn[l])
        gate, up = jnp.split(h @ w_ffn_in[l], 2, axis=-1)
        x = x + (jax.nn.silu(gate) * up) @ w_ffn_out[l]
    return rmsnorm(x, norm_final)
```

```python
import functools

import numpy as np
import jax
import jax.numpy as jnp
from jax import lax
from jax.experimental import pallas as pl
from jax.experimental.pallas import tpu as pltpu

F32 = jnp.float32
BF16 = jnp.bfloat16

D_MODEL = 1024
CHUNK = 64
SUB = 16
NORM_EPS = 1e-5
LANES = 128
SUBLANES = 8
VMEM_LIMIT_BYTES = 56 * 1024 * 1024

RWKV_HEAD_DIM = 64
RWKV_WIDTH = D_MODEL
RWKV_DECAY_RANK = 64
RWKV_ICL_RANK = 64
RWKV_GATE_RANK = 160
RWKV_GATE_PAD = 256
RWKV_GN_EPS = 64e-5
RWKV_U_COLS = 3 * RWKV_WIDTH + LANES + RWKV_GATE_PAD

HGRN_HEAD_DIM = 128
HGRN_WIDTH = D_MODEL
HGRN_HEADS = HGRN_WIDTH // HGRN_HEAD_DIM

SSM_WIDTH = 2 * D_MODEL
SSM_HEAD_DIM = 64
SSM_HEADS = SSM_WIDTH // SSM_HEAD_DIM
SSM_GROUPS = 4
SSM_STATE = 128
SSM_CONV_WIDTH = 4
SSM_BC = SSM_GROUPS * SSM_STATE
SSM_CONV_DIM = SSM_WIDTH + 2 * SSM_BC
SSM_GROUP_WIDTH = SSM_WIDTH // SSM_GROUPS
SSM_U_COLS = SSM_WIDTH + SSM_CONV_DIM + LANES

FFN_HIDDEN = ((8 * D_MODEL + 3 * 256 - 1) // (3 * 256)) * 256

SEQ_BLOCK = 256
MERGE_BLOCK = 256


def _bf(x):
    return x if x.dtype == BF16 else x.astype(BF16)


def _dot(a, b):
    return jnp.dot(_bf(a), _bf(b), preferred_element_type=F32)


def _dot_nt(a, b):
    return lax.dot_general(_bf(a), _bf(b), (((1,), (1,)), ((), ())), preferred_element_type=F32)


def _dot_tn(a, b):
    return lax.dot_general(_bf(a), _bf(b), (((0,), (0,)), ((), ())), preferred_element_type=F32)


def _split(x, n):
    parts = []
    rest = x
    for i in range(n):
        p = rest.astype(BF16)
        parts.append(p)
        if i + 1 < n:
            rest = rest - p.astype(F32)
    return parts


def _dot_sel_left(sel, x, passes=3):
    out = None
    for p in _split(x, passes):
        t = jnp.dot(sel, p, preferred_element_type=F32)
        out = t if out is None else out + t
    return out


def _dot_sel_right(x, sel, passes=3):
    out = None
    for p in _split(x, passes):
        t = jnp.dot(p, sel, preferred_element_type=F32)
        out = t if out is None else out + t
    return out


def _dot_x3(a, b):
    a_hi, a_lo = _split(a, 2)
    b_hi, b_lo = _split(b, 2)
    out = jnp.dot(a_hi, b_hi, preferred_element_type=F32)
    out = out + jnp.dot(a_hi, b_lo, preferred_element_type=F32)
    return out + jnp.dot(a_lo, b_hi, preferred_element_type=F32)


def _rmsnorm(x, gain):
    ms = jnp.mean(x * x, axis=-1, keepdims=True)
    return x * lax.rsqrt(ms + NORM_EPS) * gain


def _sigmoid(x):
    return 1.0 / (1.0 + jnp.exp(-x))


def _silu(x):
    return x * _sigmoid(x)


def _softplus(x):
    return jnp.maximum(x, 0.0) + jnp.log1p(jnp.exp(-jnp.abs(x)))


def _iota(shape, axis):
    return lax.broadcasted_iota(jnp.int32, shape, axis)


def _rwkv_kernel(x_ref, gain_ref, w_ref, mu_ref, wlr_ref, blr_ref, gup_ref, pv_ref, hsum_ref, tri_ref,
                 o_ref, ush_ref, st_ref, r_s, lw_s, k_s, v_s, ka_s, kb_s, o_s, *, ts):
    pairs = RWKV_WIDTH // LANES
    W = RWKV_WIDTH

    @pl.when(pl.program_id(1) == 0)
    def _():
        ush_ref[0:SUBLANES, :] = jnp.zeros((SUBLANES, RWKV_U_COLS), F32)
        st_ref[...] = jnp.zeros(st_ref.shape, F32)

    h = _rmsnorm(x_ref[...], gain_ref[...])
    u = _dot(h, w_ref[...])
    ush_ref[SUBLANES:SUBLANES + ts, :] = u
    prev = ush_ref[SUBLANES - 1:SUBLANES - 1 + ts, :]
    ush_ref[0:SUBLANES, :] = u[ts - SUBLANES:ts, :]
    us = u + (prev - u) * mu_ref[...]

    r = us[:, 0:W]
    k = us[:, W:2 * W]
    v = us[:, 2 * W:3 * W]
    xwa = us[:, 3 * W:3 * W + LANES]
    xg = us[:, 3 * W + LANES:]
    lane = _iota(xwa.shape, 1)
    lr_in = jnp.where(lane < RWKV_DECAY_RANK, jnp.tanh(xwa), xwa)
    lr = _dot(lr_in, wlr_ref[...]) + blr_ref[...]
    log_w = -jnp.exp(-_softplus(-lr[:, 0:W]) - 0.5)
    a = _sigmoid(lr[:, W:])
    g = _dot(_sigmoid(xg), gup_ref[...])

    k_k = pv_ref[0:1, :]
    k_a = pv_ref[1:2, :]
    r_k = pv_ref[2:3, :]
    gn_w = pv_ref[3:4, :]
    gn_b = pv_ref[4:5, :]
    hsum = hsum_ref[...]

    kk = k * k_k
    ss = _dot_sel_right(kk * kk, hsum, 2)
    kk = kk / jnp.maximum(jnp.sqrt(ss), 1e-12)
    kmod = k * (1.0 + (a - 1.0) * k_a)

    r_s[...] = r
    lw_s[...] = log_w
    k_s[...] = kmod
    v_s[...] = v
    ka_s[...] = -kk
    kb_s[...] = kk * a

    tri = tri_ref[...]
    row2 = _iota((2 * CHUNK, LANES), 0)
    col2 = _iota((2 * CHUNK, LANES), 1)
    same_head = (row2 >= CHUNK) == (col2 >= CHUNK)
    strict = jnp.logical_and(same_head, row2 > col2)
    incl = jnp.logical_and(same_head, row2 >= col2)
    eye = jnp.where(row2 == col2, 1.0, 0.0)
    first_head = _iota((CHUNK, LANES), 1) < RWKV_HEAD_DIM

    def stack2(t):
        return jnp.concatenate([jnp.where(first_head, t, 0.0), jnp.where(first_head, 0.0, t)], axis=0)

    def tri_inverse(n):
        p = eye + n
        nk = n
        for _ in range(int(np.log2(CHUNK)) - 1):
            nk = _dot_x3(nk, nk)
            p = p + _dot_x3(p, nk)
        return p

    def chunk_body(c, carry):
        c0 = pl.multiple_of(c * CHUNK, CHUNK)
        sl = pl.ds(c0, CHUNK)
        lw = lw_s[sl, :]
        cum = _dot_sel_left(tri, lw, 3)
        e_last = jnp.exp(cum[CHUNK - 1:CHUNK, :])
        e_neg = jnp.exp(-cum)
        rt = r_s[sl, :] * jnp.exp(cum)
        at = ka_s[sl, :] * jnp.exp(cum - lw)
        bt = kb_s[sl, :] * e_neg
        kt = k_s[sl, :] * e_neg
        vv = v_s[sl, :]
        bh = bt * e_last
        kh = kt * e_last
        for p in range(pairs):
            ls = slice(LANES * p, LANES * (p + 1))
            ar2 = jnp.concatenate([stack2(at[:, ls]), stack2(rt[:, ls])], axis=0)
            bk2 = jnp.concatenate([stack2(bt[:, ls]), stack2(kt[:, ls])], axis=0)
            v2 = stack2(vv[:, ls])
            gram = _dot_nt(ar2, bk2)
            h2 = 2 * CHUNK
            n_ab = jnp.where(strict, gram[0:h2, 0:h2], 0.0)
            n_ak = jnp.where(strict, gram[0:h2, h2:], 0.0)
            n_rb = jnp.where(incl, gram[h2:, 0:h2], 0.0)
            n_rk = jnp.where(incl, gram[h2:, h2:], 0.0)
            state = st_ref[p]
            from_state = _dot_nt(ar2, state)
            rhs = from_state[0:h2] + _dot(n_ak, v2)
            u2 = _dot_x3(tri_inverse(n_ab), rhs)
            uv2 = jnp.concatenate([u2, v2], axis=0)
            o2 = from_state[h2:] + _dot(jnp.concatenate([n_rb, n_rk], axis=1), uv2)
            o_s[sl, ls] = o2[0:CHUNK] + o2[CHUNK:]
            bkh2 = jnp.concatenate([stack2(bh[:, ls]), stack2(kh[:, ls])], axis=0)
            st_ref[p] = state * e_last[:, ls] + _dot_tn(uv2, bkh2)
        return carry

    lax.fori_loop(0, ts // CHUNK, chunk_body, 0)

    o = o_s[...]
    inv_n = 1.0 / RWKV_HEAD_DIM
    mean = _dot_sel_right(o, hsum, 2) * inv_n
    cen = o - mean
    var = _dot_sel_right(cen * cen, hsum, 2) * inv_n
    o = cen * lax.rsqrt(var + RWKV_GN_EPS) * gn_w + gn_b
    bonus = _dot_sel_right(r_s[...] * k_s[...] * r_k, hsum, 2)
    o = o + bonus * v_s[...]
    o_ref[...] = (o * g).astype(o_ref.dtype)


def _hgrn_kernel(x_ref, gain_ref, w_ref, pv_ref, tri_ref, ones_ref, sel_ref,
                 o_ref, st_ref, q_s, k_s, v_s, lf_s, o_s, *, ts):
    W = HGRN_WIDTH
    nsub = CHUNK // SUB

    @pl.when(pl.program_id(1) == 0)
    def _():
        st_ref[...] = jnp.zeros(st_ref.shape, F32)

    h = _rmsnorm(x_ref[...], gain_ref[...])
    u = _dot(h, w_ref[...])
    log_lb = pv_ref[0:1, :]
    log1m_lb = pv_ref[1:2, :]
    gn_w = pv_ref[2:3, :]
    f_pre = u[:, W:2 * W]
    b = log1m_lb - _softplus(-f_pre)
    mx = jnp.maximum(log_lb, b)
    log_f = mx + jnp.log1p(jnp.exp(-jnp.abs(log_lb - b)))
    q_s[...] = _silu(u[:, 0:W])
    k_s[...] = pv_ref[3:4, :] * _sigmoid(-f_pre)
    v_s[...] = u[:, 2 * W:3 * W]
    lf_s[...] = log_f
    gate = u[:, 3 * W:]

    tri = tri_ref[...]
    ones_sq = ones_ref[...]
    sel = sel_ref[...]
    sub_shift = int(np.log2(SUB))
    rblk = _iota((CHUNK, CHUNK), 0) >> sub_shift
    cblk = _iota((CHUNK, CHUNK), 1) >> sub_shift
    off_mask = cblk < rblk
    rowblk = _iota((CHUNK, LANES), 0) >> sub_shift
    s_idx = _iota((SUB, LANES), 0)

    def chunk_body(c, carry):
        c0 = pl.multiple_of(c * CHUNK, CHUNK)
        sl = pl.ds(c0, CHUNK)
        cum_all = _dot_sel_left(tri, lf_s[sl, :], 3)
        q_all = q_s[sl, :]
        k_all = k_s[sl, :]
        v_all = v_s[sl, :]
        for hd in range(HGRN_HEADS):
            ls = slice(LANES * hd, LANES * (hd + 1))
            cum = cum_all[:, ls]
            q = q_all[:, ls]
            k = k_all[:, ls]
            v = v_all[:, ls]
            vb = _bf(v)
            last = cum[CHUNK - 1:CHUNK, :]
            starts = [None] + [cum[SUB * i - 1:SUB * i, :] for i in range(1, nsub)]
            cs = jnp.zeros_like(cum)
            for i in range(1, nsub):
                cs = jnp.where(rowblk == i, starts[i], cs)
            q_sub = q * jnp.exp(cum - cs)
            rows = [jnp.zeros((SUB, CHUNK), F32)]
            for i in range(1, nsub):
                k_i = k * jnp.exp(jnp.minimum(starts[i] - cum, 0.0))
                rows.append(_dot_nt(q_sub[SUB * i:SUB * (i + 1), :], k_i))
            a_off = jnp.where(off_mask, jnp.concatenate(rows, axis=0), 0.0)
            state = st_ref[hd]
            o = _dot(a_off, vb) + _dot_nt(q * jnp.exp(cum), state)
            diag = []
            for j in range(nsub):
                rs = slice(SUB * j, SUB * (j + 1))
                cum_b = cum[rs, :]
                q_b = q[rs, :]
                k_b = k[rs, :]
                pieces = []
                for t in range(SUB):
                    expo = jnp.where(s_idx <= t, cum_b[t:t + 1, :] - cum_b, -1e30)
                    pieces.append(jnp.exp(expo) * k_b * q_b[t:t + 1, :])
                pmat = jnp.concatenate(pieces, axis=0)
                score = _dot(pmat, ones_sq)
                wv = score * jnp.concatenate([v[rs, :]] * SUB, axis=0)
                diag.append(_dot(sel, wv))
            o = o + jnp.concatenate(diag, axis=0)
            o_s[sl, ls] = o
            k_end = k * jnp.exp(last - cum)
            st_ref[hd] = state * jnp.exp(last) + _dot_tn(vb, k_end)
        return carry

    lax.fori_loop(0, ts // CHUNK, chunk_body, 0)

    outs = []
    for hd in range(HGRN_HEADS):
        ls = slice(LANES * hd, LANES * (hd + 1))
        o = o_s[:, ls]
        outs.append(o * lax.rsqrt(jnp.mean(o * o, axis=-1, keepdims=True) + NORM_EPS))
    o = jnp.concatenate(outs, axis=1) * gn_w
    o_ref[...] = (o * _sigmoid(gate)).astype(o_ref.dtype)


def _ssm_kernel(x_ref, gain_ref, w_ref, cw_ref, cb_ref, hv_ref, wv_ref, tri_ref, exp_ref,
                o_ref, xb_ref, st_ref, ec_s, xd_s, b_s, c_s, y_s, *, ts):
    W = SSM_WIDTH
    gw = SSM_GROUP_WIDTH
    pairs = W // LANES

    @pl.when(pl.program_id(1) == 0)
    def _():
        xb_ref[0:SUBLANES, :] = jnp.zeros((SUBLANES, SSM_CONV_DIM), F32)
        st_ref[...] = jnp.zeros(st_ref.shape, F32)

    h = _rmsnorm(x_ref[...], gain_ref[...])
    u = _dot(h, w_ref[...])
    z = u[:, 0:W]
    xb_ref[SUBLANES:SUBLANES + ts, :] = u[:, W:W + SSM_CONV_DIM]
    conv = cb_ref[...]
    for j in range(SSM_CONV_WIDTH):
        off = SUBLANES - (SSM_CONV_WIDTH - 1) + j
        conv = conv + xb_ref[off:off + ts, :] * cw_ref[j:j + 1, :]
    xb_ref[0:SUBLANES, :] = xb_ref[ts:ts + SUBLANES, :]
    xbc = _silu(conv)
    xs = xbc[:, 0:W]
    b_s[...] = xbc[:, W:W + SSM_BC]
    c_s[...] = xbc[:, W + SSM_BC:]

    dt_bias = hv_ref[0:1, :]
    neg_a = hv_ref[1:2, :]
    dt = _softplus(u[:, W + SSM_CONV_DIM:] + dt_bias)
    log_a = dt * neg_a
    expand = exp_ref[...]
    tri = tri_ref[...]
    cums = [_dot_sel_left(tri, log_a[CHUNK * c:CHUNK * (c + 1), :], 3) for c in range(ts // CHUNK)]
    ec_s[...] = _dot_sel_right(jnp.concatenate(cums, axis=0), expand, 3)
    xd_s[...] = xs * _dot_sel_right(dt, expand, 3)

    t_idx = _iota((CHUNK, W), 0)
    s_idx = _iota((CHUNK, W), 1) & (CHUNK - 1)
    causal = s_idx <= t_idx
    on_diag = s_idx == t_idx
    first_head = (_iota((CHUNK, LANES), 1) < SSM_HEAD_DIM)

    def chunk_body(c, carry):
        c0 = pl.multiple_of(c * CHUNK, CHUNK)
        sl = pl.ds(c0, CHUNK)
        ec = ec_s[sl, :]
        xd = xd_s[sl, :]
        bm = b_s[sl, :]
        cm = c_s[sl, :]
        last = ec[CHUNK - 1:CHUNK, :]
        by_src = jnp.sum(jnp.where(on_diag, ec, 0.0), axis=0, keepdims=True)
        decay = jnp.exp(jnp.where(causal, ec - by_src, -1e30))
        xw = xd * jnp.exp(last - ec)
        e_in = jnp.exp(ec)
        e_last = jnp.exp(last)
        for g in range(SSM_GROUPS):
            gl = slice(gw * g, gw * (g + 1))
            sl_n = slice(SSM_STATE * g, SSM_STATE * (g + 1))
            c_g = _bf(cm[:, sl_n])
            b_g = _bf(bm[:, sl_n])
            cb2 = _dot_nt(c_g, jnp.concatenate([b_g, b_g], axis=0))
            state = st_ref[g]
            y_in = _dot(c_g, state) * e_in[:, gl]
            ys = []
            for p in range(gw // LANES):
                ls = slice(gw * g + LANES * p, gw * g + LANES * (p + 1))
                m = decay[:, ls] * cb2
                xp = xd[:, ls]
                x2 = jnp.concatenate([jnp.where(first_head, xp, 0.0), jnp.where(first_head, 0.0, xp)], axis=0)
                ys.append(_dot(m, x2))
            y_s[sl, gl] = y_in + jnp.concatenate(ys, axis=1)
            st_ref[g] = state * e_last[:, gl] + _dot_tn(b_g, xw[:, gl])
        return carry

    lax.fori_loop(0, ts // CHUNK, chunk_body, 0)

    d_skip = wv_ref[0:1, :]
    gn_w = wv_ref[1:2, :]
    y = (y_s[...] + d_skip * xs) * _silu(z)
    outs = []
    for g in range(SSM_GROUPS):
        yg = y[:, gw * g:gw * (g + 1)]
        outs.append(yg * lax.rsqrt(jnp.mean(yg * yg, axis=-1, keepdims=True) + NORM_EPS))
    o_ref[...] = (jnp.concatenate(outs, axis=1) * gn_w).astype(o_ref.dtype)


def _merge_kernel(x_ref, ya_ref, yb_ref, yc_ref, gmix_ref, wg_ref, wba_ref, wbb_ref, wbc_ref, wout_ref,
                  gffn_ref, wfi_ref, wfo_ref, gfin_ref, o_ref, *, final_norm):
    x = x_ref[...]
    h = _rmsnorm(x, gmix_ref[...])
    gates = _sigmoid(_dot(h, wg_ref[...]))
    merged = (gates[:, 0:D_MODEL] * jnp.dot(ya_ref[...], wba_ref[...], preferred_element_type=F32)
              + gates[:, D_MODEL:2 * D_MODEL] * jnp.dot(yb_ref[...], wbb_ref[...], preferred_element_type=F32)
              + gates[:, 2 * D_MODEL:] * jnp.dot(yc_ref[...], wbc_ref[...], preferred_element_type=F32))
    x = x + _dot(merged, wout_ref[...])
    h = _rmsnorm(x, gffn_ref[...])
    gu = _dot(h, wfi_ref[...])
    act = _silu(gu[:, 0:FFN_HIDDEN]) * gu[:, FFN_HIDDEN:]
    x = x + _dot(act, wfo_ref[...])
    if final_norm:
        x = _rmsnorm(x, gfin_ref[...])
    o_ref[...] = x


def _const_spec(arr):
    nd = arr.ndim
    return pl.BlockSpec(arr.shape, lambda b, s, _nd=nd: (0,) * _nd, pipeline_mode=pl.Buffered(1))


def _seq_spec(ts, width):
    return pl.BlockSpec((None, ts, width), lambda b, s: (b, s, 0))


def _params():
    return pltpu.CompilerParams(dimension_semantics=("arbitrary", "arbitrary"),
                                vmem_limit_bytes=VMEM_LIMIT_BYTES)


def _run_mixer(kernel_fn, x, consts, out_width, scratch, ts):
    bsz, seq, _ = x.shape
    return pl.pallas_call(
        functools.partial(kernel_fn, ts=ts),
        grid=(bsz, seq // ts),
        in_specs=[_seq_spec(ts, D_MODEL)] + [_const_spec(c) for c in consts],
        out_specs=_seq_spec(ts, out_width),
        out_shape=jax.ShapeDtypeStruct((bsz, seq, out_width), BF16),
        scratch_shapes=scratch,
        compiler_params=_params(),
    )(x, *consts)


def _row(v):
    return v.astype(F32).reshape(1, -1)


def _pad_cols(m, width):
    return jnp.pad(m, ((0, 0), (0, width - m.shape[1])))


def _rows8(rows, width):
    m = jnp.concatenate([_row(r) for r in rows], axis=0)
    return jnp.pad(m, ((0, SUBLANES - m.shape[0]), (0, width - m.shape[1])))


def _head_sum_matrix(width, head_dim):
    idx = np.arange(width) // head_dim
    return jnp.asarray(idx[:, None] == idx[None, :], dtype=BF16)


def _tri_ones(n):
    return jnp.asarray(np.tril(np.ones((n, n), np.float32)), dtype=BF16)


def _rwkv_branch(x, gain, w_in, mu, w0, w_up, a0, a_up, g_up, k_k, k_a, r_k, gn_w, gn_b, ts):
    W = RWKV_WIDTH
    dr, ir, gr = RWKV_DECAY_RANK, RWKV_ICL_RANK, RWKV_GATE_RANK
    pad = RWKV_GATE_PAD - gr
    w = _pad_cols(w_in, RWKV_U_COLS).astype(BF16)
    mu_p = _pad_cols(_row(mu), RWKV_U_COLS)
    wlr = jnp.zeros((LANES, 2 * W), F32).at[0:dr, 0:W].set(w_up).at[dr:dr + ir, W:].set(a_up).astype(BF16)
    blr = jnp.concatenate([_row(w0), _row(a0)], axis=1)
    gup = jnp.pad(g_up, ((0, pad), (0, 0))).astype(BF16)
    pv = _rows8([k_k, k_a, r_k.reshape(-1), gn_w, gn_b], W)
    consts = [_row(gain), w, mu_p, wlr, blr, gup, pv, _head_sum_matrix(W, RWKV_HEAD_DIM), _tri_ones(CHUNK)]
    scratch = ([pltpu.VMEM((ts + SUBLANES, RWKV_U_COLS), F32),
                pltpu.VMEM((W // LANES, LANES, LANES), F32)]
               + [pltpu.VMEM((ts, W), F32) for _ in range(7)])
    return _run_mixer(_rwkv_kernel, x, consts, W, scratch, ts)


def _hgrn_branch(x, gain, w_in, lb, gn_w, ts):
    W = HGRN_WIDTH
    lb = lb.astype(F32)
    pv = _rows8([jnp.log(lb), jnp.log1p(-lb), gn_w, 1.0 - lb], W)
    t_i = np.arange(SUB)[:, None]
    pair = np.arange(SUB * SUB)[None, :]
    sel = jnp.asarray((pair // SUB == t_i) & (pair % SUB <= t_i), dtype=BF16)
    consts = [_row(gain), w_in.astype(BF16), pv, _tri_ones(CHUNK), jnp.ones((LANES, LANES), BF16), sel]
    scratch = ([pltpu.VMEM((HGRN_HEADS, HGRN_HEAD_DIM, HGRN_HEAD_DIM), F32)]
               + [pltpu.VMEM((ts, W), F32) for _ in range(5)])
    return _run_mixer(_hgrn_kernel, x, consts, W, scratch, ts)


def _ssm_branch(x, gain, w_in, conv_w, conv_b, dt_bias, a_log, d_skip, gn_w, ts):
    W = SSM_WIDTH
    w = _pad_cols(w_in, SSM_U_COLS).astype(BF16)
    cw = jnp.pad(conv_w.astype(F32).T, ((0, SUBLANES - SSM_CONV_WIDTH), (0, 0)))
    hv = _rows8([dt_bias, -jnp.exp(a_log.astype(F32))], LANES)
    wv = _rows8([jnp.repeat(d_skip.astype(F32), SSM_HEAD_DIM), gn_w], W)
    head_of_lane = np.arange(W) // SSM_HEAD_DIM
    expand = jnp.asarray(np.arange(LANES)[:, None] == head_of_lane[None, :], dtype=BF16)
    consts = [_row(gain), w, cw, _row(conv_b), hv, wv, _tri_ones(CHUNK), expand]
    scratch = [pltpu.VMEM((ts + SUBLANES, SSM_CONV_DIM), F32),
               pltpu.VMEM((SSM_GROUPS, SSM_STATE, SSM_GROUP_WIDTH), F32),
               pltpu.VMEM((ts, W), F32), pltpu.VMEM((ts, W), F32),
               pltpu.VMEM((ts, SSM_BC), F32), pltpu.VMEM((ts, SSM_BC), F32),
               pltpu.VMEM((ts, W), F32)]
    return _run_mixer(_ssm_kernel, x, consts, W, scratch, ts)


def _merge_ffn(x, ya, yb, yc, gain_mix, w_gate, w_branch, w_out, gain_ffn, w_ffn_in, w_ffn_out, gain_final,
               final_norm, ts):
    bsz, seq, _ = x.shape
    wb = w_branch.astype(BF16)
    consts = [_row(gain_mix), w_gate.astype(BF16), wb[:RWKV_WIDTH], wb[RWKV_WIDTH:RWKV_WIDTH + HGRN_WIDTH],
              wb[RWKV_WIDTH + HGRN_WIDTH:], w_out.astype(BF16), _row(gain_ffn), w_ffn_in.astype(BF16),
              w_ffn_out.astype(BF16), _row(gain_final)]
    seqs = [x, ya, yb, yc]
    return pl.pallas_call(
        functools.partial(_merge_kernel, final_norm=final_norm),
        grid=(bsz, seq // ts),
        in_specs=[_seq_spec(ts, a.shape[-1]) for a in seqs] + [_const_spec(c) for c in consts],
        out_specs=_seq_spec(ts, D_MODEL),
        out_shape=jax.ShapeDtypeStruct((bsz, seq, D_MODEL), F32),
        compiler_params=_params(),
    )(*seqs, *consts)


def kernel(x, norm_mix, w_in, rwkv_mu, rwkv_w0, rwkv_w_up, rwkv_a0, rwkv_a_up, rwkv_g_up, rwkv_k_k, rwkv_k_a,
           rwkv_r_k, rwkv_gn_w, rwkv_gn_b, hgrn_lb_logits, hgrn_gn_w, ssm_conv_w, ssm_conv_b, ssm_dt_bias,
           ssm_a_log, ssm_d, ssm_gn_w, w_branch, w_out, norm_ffn, w_ffn_in, w_ffn_out, norm_final):
    bsz, seq, d = x.shape
    assert d == D_MODEL
    depth = w_in.shape[0]
    ts = min(SEQ_BLOCK, seq)
    tm = min(MERGE_BLOCK, seq)
    assert seq % ts == 0 and seq % tm == 0 and ts % CHUNK == 0

    rwkv_cols = 3 * RWKV_WIDTH + RWKV_DECAY_RANK + RWKV_ICL_RANK + RWKV_GATE_RANK
    off_hgrn = rwkv_cols
    off_ssm = off_hgrn + 4 * HGRN_WIDTH
    off_gate = off_ssm + SSM_WIDTH + SSM_CONV_DIM + SSM_HEADS

    cs = jnp.cumsum(jax.nn.softmax(hgrn_lb_logits.astype(F32), axis=0), axis=0)
    lbs = cs - cs[:1]

    x = x.astype(F32)
    for l in range(depth):
        wi = w_in[l]
        ya = _rwkv_branch(x, norm_mix[l], wi[:, :off_hgrn], rwkv_mu[l], rwkv_w0[l], rwkv_w_up[l], rwkv_a0[l],
                          rwkv_a_up[l], rwkv_g_up[l], rwkv_k_k[l], rwkv_k_a[l], rwkv_r_k[l], rwkv_gn_w[l],
                          rwkv_gn_b[l], ts)
        yb = _hgrn_branch(x, norm_mix[l], wi[:, off_hgrn:off_ssm], lbs[l], hgrn_gn_w[l], ts)
        yc = _ssm_branch(x, norm_mix[l], wi[:, off_ssm:off_gate], ssm_conv_w[l], ssm_conv_b[l], ssm_dt_bias[l],
                         ssm_a_log[l], ssm_d[l], ssm_gn_w[l], ts)
        x = _merge_ffn(x, ya, yb, yc, norm_mix[l], wi[:, off_gate:], w_branch[l], w_out[l], norm_ffn[l],
                       w_ffn_in[l], w_ffn_out[l], norm_final, l == depth - 1, tm)
    return x
```

```python
import functools

import numpy as np
import jax
import jax.numpy as jnp
from jax import lax
from jax.experimental import pallas as pl
from jax.experimental.pallas import tpu as pltpu

F32 = jnp.float32
BF16 = jnp.bfloat16

D_MODEL = 1024
CHUNK = 64
SUB = 16
HGRN_FAST_LOG_RANGE = 60.0
NORM_EPS = 1e-5
LANES = 128
SUBLANES = 8
MXU_TILE = 256
VMEM_LIMIT_BYTES = 56 * 1024 * 1024

RWKV_HEAD_DIM = 64
RWKV_WIDTH = D_MODEL
RWKV_DECAY_RANK = 64
RWKV_ICL_RANK = 64
RWKV_GATE_RANK = 160
RWKV_GATE_PAD = 256
RWKV_GN_EPS = 64e-5
RWKV_U_COLS = 3 * RWKV_WIDTH + LANES + RWKV_GATE_PAD

HGRN_HEAD_DIM = 128
HGRN_WIDTH = D_MODEL
HGRN_HEADS = HGRN_WIDTH // HGRN_HEAD_DIM

SSM_WIDTH = 2 * D_MODEL
SSM_HEAD_DIM = 64
SSM_HEADS = SSM_WIDTH // SSM_HEAD_DIM
SSM_GROUPS = 4
SSM_STATE = 128
SSM_CONV_WIDTH = 4
SSM_BC = SSM_GROUPS * SSM_STATE
SSM_CONV_DIM = SSM_WIDTH + 2 * SSM_BC
SSM_GROUP_WIDTH = SSM_WIDTH // SSM_GROUPS
SSM_U_COLS = SSM_WIDTH + SSM_CONV_DIM + LANES

FFN_HIDDEN = ((8 * D_MODEL + 3 * 256 - 1) // (3 * 256)) * 256

SEQ_BLOCK = 256
MERGE_BLOCK = 256


def _bf(x):
    return x if x.dtype == BF16 else x.astype(BF16)


def _dot(a, b):
    return jnp.dot(_bf(a), _bf(b), preferred_element_type=F32)


def _dot_nt(a, b):
    return lax.dot_general(_bf(a), _bf(b), (((1,), (1,)), ((), ())), preferred_element_type=F32)


def _dot_tn(a, b):
    return lax.dot_general(_bf(a), _bf(b), (((0,), (0,)), ((), ())), preferred_element_type=F32)


def _split(x, n):
    parts = []
    rest = x
    for i in range(n):
        p = rest.astype(BF16)
        parts.append(p)
        if i + 1 < n:
            rest = rest - p.astype(F32)
    return parts


def _dot_sel_left(sel, x, passes=3):
    out = None
    for p in _split(x, passes):
        t = jnp.dot(sel, p, preferred_element_type=F32)
        out = t if out is None else out + t
    return out


def _dot_sel_right(x, sel, passes=3):
    out = None
    for p in _split(x, passes):
        t = jnp.dot(p, sel, preferred_element_type=F32)
        out = t if out is None else out + t
    return out


def _rmsnorm(x, gain):
    ms = jnp.mean(x * x, axis=-1, keepdims=True)
    return x * lax.rsqrt(ms + NORM_EPS) * gain


def _sigmoid(x):
    return 1.0 / (1.0 + jnp.exp(-x))


def _silu(x):
    return x * _sigmoid(x)


def _softplus(x):
    return jnp.maximum(x, 0.0) + jnp.log1p(jnp.exp(-jnp.abs(x)))


def _iota(shape, axis):
    return lax.broadcasted_iota(jnp.int32, shape, axis)


def _rwkv_kernel(x_ref, gain_ref, w_ref, mu_ref, wlr_ref, blr_ref, gup_ref, pv_ref, hsum_ref, tri_ref,
                 o_ref, ush_ref, st_ref, r_s, lw_s, k_s, v_s, ka_s, kb_s, o_s, *, ts):
    pairs = RWKV_WIDTH // LANES
    W = RWKV_WIDTH

    @pl.when(pl.program_id(1) == 0)
    def _():
        ush_ref[0:SUBLANES, :] = jnp.zeros((SUBLANES, RWKV_U_COLS), F32)
        st_ref[...] = jnp.zeros(st_ref.shape, F32)

    h = _rmsnorm(x_ref[...], gain_ref[...])
    u = _dot(h, w_ref[...])
    ush_ref[SUBLANES:SUBLANES + ts, :] = u
    prev = ush_ref[SUBLANES - 1:SUBLANES - 1 + ts, :]
    ush_ref[0:SUBLANES, :] = u[ts - SUBLANES:ts, :]
    us = u + (prev - u) * mu_ref[...]

    r = us[:, 0:W]
    k = us[:, W:2 * W]
    v = us[:, 2 * W:3 * W]
    xwa = us[:, 3 * W:3 * W + LANES]
    xg = us[:, 3 * W + LANES:]
    lane = _iota(xwa.shape, 1)
    lr_in = jnp.where(lane < RWKV_DECAY_RANK, jnp.tanh(xwa), xwa)
    lr = _dot(lr_in, wlr_ref[...]) + blr_ref[...]
    log_w = -jnp.exp(-_softplus(-lr[:, 0:W]) - 0.5)
    a = _sigmoid(lr[:, W:])
    g = _dot(_sigmoid(xg), gup_ref[...])

    k_k = pv_ref[0:1, :]
    k_a = pv_ref[1:2, :]
    r_k = pv_ref[2:3, :]
    gn_w = pv_ref[3:4, :]
    gn_b = pv_ref[4:5, :]
    hsum = hsum_ref[...]

    def head_sum(t):
        return jnp.concatenate([_dot_sel_right(t[:, MXU_TILE * j:MXU_TILE * (j + 1)], hsum, 2)
                                for j in range(W // MXU_TILE)], axis=1)

    kk = k * k_k
    ss = head_sum(kk * kk)
    kk = kk / jnp.maximum(jnp.sqrt(ss), 1e-12)
    kmod = k * (1.0 + (a - 1.0) * k_a)

    r_s[...] = r
    lw_s[...] = log_w
    k_s[...] = kmod
    v_s[...] = v
    ka_s[...] = -kk
    kb_s[...] = kk * a

    tri = tri_ref[...]
    row2 = _iota((2 * CHUNK, LANES), 0)
    col2 = _iota((2 * CHUNK, LANES), 1)
    same_head = (row2 >= CHUNK) == (col2 >= CHUNK)
    strict = jnp.logical_and(same_head, row2 > col2)
    incl = jnp.logical_and(same_head, row2 >= col2)
    eye = jnp.where(row2 == col2, 1.0, 0.0)
    first_head = _iota((CHUNK, LANES), 1) < RWKV_HEAD_DIM

    def stack2(t):
        return jnp.concatenate([jnp.where(first_head, t, 0.0), jnp.where(first_head, 0.0, t)], axis=0)

    def chunk_body(c, carry):
        c0 = pl.multiple_of(c * CHUNK, CHUNK)
        sl = pl.ds(c0, CHUNK)
        lw = lw_s[sl, :]
        cum = _dot_sel_left(tri, lw, 3)
        e_last = jnp.exp(cum[CHUNK - 1:CHUNK, :])
        e_neg = jnp.exp(-cum)
        rt = r_s[sl, :] * jnp.exp(cum)
        at = ka_s[sl, :] * jnp.exp(cum - lw)
        bt = kb_s[sl, :] * e_neg
        kt = k_s[sl, :] * e_neg
        vv = v_s[sl, :]
        h2 = 2 * CHUNK
        lanes = [slice(LANES * p, LANES * (p + 1)) for p in range(pairs)]
        ar2 = [jnp.concatenate([stack2(at[:, ls]), stack2(rt[:, ls])], axis=0) for ls in lanes]
        bk2 = [jnp.concatenate([stack2(bt[:, ls]), stack2(kt[:, ls])], axis=0) for ls in lanes]
        v2 = [stack2(vv[:, ls]) for ls in lanes]
        gram = [_dot_nt(a, b) for a, b in zip(ar2, bk2)]
        n_ab = [jnp.where(strict, g[0:h2, 0:h2], 0.0) for g in gram]
        n_ak = [jnp.where(strict, g[0:h2, h2:], 0.0) for g in gram]
        n_rbk = [jnp.concatenate([jnp.where(incl, g[h2:, 0:h2], 0.0), jnp.where(incl, g[h2:, h2:], 0.0)], axis=1)
                 for g in gram]
        state = [st_ref[p] for p in range(pairs)]
        from_state = [_dot_nt(a, s) for a, s in zip(ar2, state)]
        rhs = [f[0:h2] + _dot(n, v) for f, n, v in zip(from_state, n_ak, v2)]
        prod = [eye + n for n in n_ab]
        npow = [_dot(n, n) for n in n_ab]
        for _ in range(int(np.log2(CHUNK)) - 2):
            both = [_dot(jnp.concatenate([nk, pr], axis=0), nk) for nk, pr in zip(npow, prod)]
            prod = [pr + b[h2:] for pr, b in zip(prod, both)]
            npow = [b[0:h2] for b in both]
        inv = [pr + _dot(pr, nk) for pr, nk in zip(prod, npow)]
        u2 = [_dot(t, r) for t, r in zip(inv, rhs)]
        uv2 = [jnp.concatenate([u, v], axis=0) for u, v in zip(u2, v2)]
        o2 = [f[h2:] + _dot(n, uv) for f, n, uv in zip(from_state, n_rbk, uv2)]
        for p, ls in enumerate(lanes):
            o_s[sl, ls] = o2[p][0:CHUNK] + o2[p][CHUNK:]
            st_ref[p] = state[p] * e_last[:, ls] + _dot_tn(uv2[p], bk2[p] * e_last[:, ls])
        return carry

    lax.fori_loop(0, ts // CHUNK, chunk_body, 0)

    o = o_s[...]
    inv_n = 1.0 / RWKV_HEAD_DIM
    mean = head_sum(o) * inv_n
    cen = o - mean
    var = head_sum(cen * cen) * inv_n
    o = cen * lax.rsqrt(var + RWKV_GN_EPS) * gn_w + gn_b
    bonus = head_sum(r_s[...] * k_s[...] * r_k)
    o = o + bonus * v_s[...]
    o_ref[...] = (o * g).astype(o_ref.dtype)


def _hgrn_kernel(x_ref, gain_ref, w_ref, pv_ref, tri_ref, ones_ref, sel_ref, bsel_ref,
                 o_ref, st_ref, q_s, k_s, v_s, lf_s, o_s, *, ts):
    W = HGRN_WIDTH
    nsub = CHUNK // SUB

    @pl.when(pl.program_id(1) == 0)
    def _():
        st_ref[...] = jnp.zeros(st_ref.shape, F32)

    h = _rmsnorm(x_ref[...], gain_ref[...])
    u = _dot(h, w_ref[...])
    log_lb = pv_ref[0:1, :]
    log1m_lb = pv_ref[1:2, :]
    gn_w = pv_ref[2:3, :]
    f_pre = u[:, W:2 * W]
    b = log1m_lb - _softplus(-f_pre)
    mx = jnp.maximum(log_lb, b)
    log_f = mx + jnp.log1p(jnp.exp(-jnp.abs(log_lb - b)))
    q_s[...] = _silu(u[:, 0:W])
    k_s[...] = pv_ref[3:4, :] * _sigmoid(-f_pre)
    v_s[...] = u[:, 2 * W:3 * W]
    lf_s[...] = log_f
    gate = u[:, 3 * W:]

    tri = tri_ref[...]
    ones_sq = ones_ref[...]
    sel = sel_ref[...]
    sub_shift = int(np.log2(SUB))
    rblk = _iota((CHUNK, CHUNK), 0) >> sub_shift
    cblk = _iota((CHUNK, CHUNK), 1) >> sub_shift
    off_mask = cblk < rblk
    rowblk = _iota((CHUNK, LANES), 0) >> sub_shift
    s_idx = _iota((SUB, LANES), 0)

    heads = range(HGRN_HEADS)
    lanes = [slice(LANES * hd, LANES * (hd + 1)) for hd in heads]
    causal = _iota((CHUNK, CHUNK), 1) <= _iota((CHUNK, CHUNK), 0)

    def fast_chunk_body(c, carry):
        c0 = pl.multiple_of(c * CHUNK, CHUNK)
        sl = pl.ds(c0, CHUNK)
        cum = _dot_sel_left(tri, lf_s[sl, :], 3)
        q = q_s[sl, :]
        k = k_s[sl, :]
        v = v_s[sl, :]
        starts = [None] + [cum[SUB * i - 1:SUB * i, :] for i in range(1, nsub)]
        cs = jnp.zeros_like(cum)
        rowblk_w = _iota(cum.shape, 0) >> sub_shift
        for i in range(1, nsub):
            cs = jnp.where(rowblk_w == i, starts[i], cs)
        q_sub = q * jnp.exp(cum - cs)
        k_own = k * jnp.exp(cs - cum)
        k_prev = [None] + [k[0:SUB * i, :] * jnp.exp(jnp.minimum(starts[i] - cum[0:SUB * i, :], 0.0))
                           for i in range(1, nsub)]
        last = cum[CHUNK - 1:CHUNK, :]
        q_in = q * jnp.exp(cum)
        k_end = k * jnp.exp(last - cum)
        e_last = jnp.exp(last)
        state = [st_ref[hd] for hd in heads]
        scores = []
        for ls in lanes:
            rows = []
            for i in range(nsub):
                parts = [k_own[SUB * i:SUB * (i + 1), ls]]
                if i > 0:
                    parts = [k_prev[i][:, ls]] + parts
                if i + 1 < nsub:
                    parts.append(jnp.zeros((CHUNK - SUB * (i + 1), LANES), F32))
                rows.append(_dot_nt(q_sub[SUB * i:SUB * (i + 1), ls], jnp.concatenate(parts, axis=0)))
            scores.append(jnp.where(causal, jnp.concatenate(rows, axis=0), 0.0))
        outs = [_dot(a, v[:, ls]) + _dot_nt(q_in[:, ls], s) for a, ls, s in zip(scores, lanes, state)]
        for hd, ls in enumerate(lanes):
            o_s[sl, ls] = outs[hd]
            st_ref[hd] = state[hd] * e_last[:, ls] + _dot_tn(v[:, ls], k_end[:, ls])
        return carry

    def chunk_body(c, carry):
        c0 = pl.multiple_of(c * CHUNK, CHUNK)
        sl = pl.ds(c0, CHUNK)
        cum_all = _dot_sel_left(tri, lf_s[sl, :], 3)
        q_all = q_s[sl, :]
        k_all = k_s[sl, :]
        v_all = v_s[sl, :]
        for hd in range(HGRN_HEADS):
            ls = slice(LANES * hd, LANES * (hd + 1))
            cum = cum_all[:, ls]
            q = q_all[:, ls]
            k = k_all[:, ls]
            v = v_all[:, ls]
            vb = _bf(v)
            last = cum[CHUNK - 1:CHUNK, :]
            starts = [None] + [cum[SUB * i - 1:SUB * i, :] for i in range(1, nsub)]
            cs = jnp.zeros_like(cum)
            for i in range(1, nsub):
                cs = jnp.where(rowblk == i, starts[i], cs)
            q_sub = q * jnp.exp(cum - cs)
            rows = [jnp.zeros((SUB, CHUNK), F32)]
            for i in range(1, nsub):
                k_i = k * jnp.exp(jnp.minimum(starts[i] - cum, 0.0))
                rows.append(_dot_nt(q_sub[SUB * i:SUB * (i + 1), :], k_i))
            a_off = jnp.where(off_mask, jnp.concatenate(rows, axis=0), 0.0)
            state = st_ref[hd]
            o = _dot(a_off, vb) + _dot_nt(q * jnp.exp(cum), state)
            diag = []
            for j in range(nsub):
                rs = slice(SUB * j, SUB * (j + 1))
                cum_b = cum[rs, :]
                q_b = q[rs, :]
                k_b = k[rs, :]
                pieces = []
                for t in range(SUB):
                    expo = jnp.where(s_idx <= t, cum_b[t:t + 1, :] - cum_b, -1e30)
                    pieces.append(jnp.exp(expo) * k_b * q_b[t:t + 1, :])
                pmat = jnp.concatenate(pieces, axis=0)
                score = _dot(pmat, ones_sq)
                wv = score * jnp.concatenate([v[rs, :]] * SUB, axis=0)
                diag.append(_dot(sel, wv))
            o = o + jnp.concatenate(diag, axis=0)
            o_s[sl, ls] = o
            k_end = k * jnp.exp(last - cum)
            st_ref[hd] = state * jnp.exp(last) + _dot_tn(vb, k_end)
        return carry

    sub_sums = _dot_sel_left(bsel_ref[...], log_f, 3)
    fast_ok = jnp.min(sub_sums) >= -HGRN_FAST_LOG_RANGE

    @pl.when(fast_ok)
    def _():
        lax.fori_loop(0, ts // CHUNK, fast_chunk_body, 0)

    @pl.when(jnp.logical_not(fast_ok))
    def _():
        lax.fori_loop(0, ts // CHUNK, chunk_body, 0)

    outs = []
    for hd in range(HGRN_HEADS):
        ls = slice(LANES * hd, LANES * (hd + 1))
        o = o_s[:, ls]
        outs.append(o * lax.rsqrt(jnp.mean(o * o, axis=-1, keepdims=True) + NORM_EPS))
    o = jnp.concatenate(outs, axis=1) * gn_w
    o_ref[...] = (o * _sigmoid(gate)).astype(o_ref.dtype)


def _ssm_kernel(x_ref, gain_ref, w_ref, cw_ref, cb_ref, hv_ref, wv_ref, tri_ref, exp_ref,
                o_ref, xb_ref, st_ref, ec_s, xd_s, b_s, c_s, y_s, *, ts):
    W = SSM_WIDTH
    gw = SSM_GROUP_WIDTH
    pairs = W // LANES

    @pl.when(pl.program_id(1) == 0)
    def _():
        xb_ref[0:SUBLANES, :] = jnp.zeros((SUBLANES, SSM_CONV_DIM), F32)
        st_ref[...] = jnp.zeros(st_ref.shape, F32)

    h = _rmsnorm(x_ref[...], gain_ref[...])
    u = _dot(h, w_ref[...])
    z = u[:, 0:W]
    xb_ref[SUBLANES:SUBLANES + ts, :] = u[:, W:W + SSM_CONV_DIM]
    conv = cb_ref[...]
    for j in range(SSM_CONV_WIDTH):
        off = SUBLANES - (SSM_CONV_WIDTH - 1) + j
        conv = conv + xb_ref[off:off + ts, :] * cw_ref[j:j + 1, :]
    xb_ref[0:SUBLANES, :] = xb_ref[ts:ts + SUBLANES, :]
    xbc = _silu(conv)
    xs = xbc[:, 0:W]
    b_s[...] = xbc[:, W:W + SSM_BC]
    c_s[...] = xbc[:, W + SSM_BC:]

    dt_bias = hv_ref[0:1, :]
    neg_a = hv_ref[1:2, :]
    dt = _softplus(u[:, W + SSM_CONV_DIM:] + dt_bias)
    log_a = dt * neg_a
    expand = exp_ref[...]
    tri = tri_ref[...]
    cums = [_dot_sel_left(tri, log_a[CHUNK * c:CHUNK * (c + 1), :], 3) for c in range(ts // CHUNK)]
    ec_s[...] = _dot_sel_right(jnp.concatenate(cums, axis=0), expand, 3)
    xd_s[...] = xs * _dot_sel_right(dt, expand, 3)

    t_idx = _iota((CHUNK, W), 0)
    s_idx = _iota((CHUNK, W), 1) & (CHUNK - 1)
    causal = s_idx <= t_idx
    on_diag = s_idx == t_idx
    first_head = (_iota((CHUNK, LANES), 1) < SSM_HEAD_DIM)

    def chunk_body(c, carry):
        c0 = pl.multiple_of(c * CHUNK, CHUNK)
        sl = pl.ds(c0, CHUNK)
        ec = ec_s[sl, :]
        xd = xd_s[sl, :]
        bm = b_s[sl, :]
        cm = c_s[sl, :]
        last = ec[CHUNK - 1:CHUNK, :]
        by_src = jnp.sum(jnp.where(on_diag, ec, 0.0), axis=0, keepdims=True)
        decay = jnp.exp(jnp.where(causal, ec - by_src, -1e30))
        xw = xd * jnp.exp(last - ec)
        e_in = jnp.exp(ec)
        e_last = jnp.exp(last)
        for g in range(SSM_GROUPS):
            gl = slice(gw * g, gw * (g + 1))
            sl_n = slice(SSM_STATE * g, SSM_STATE * (g + 1))
            c_g = _bf(cm[:, sl_n])
            b_g = _bf(bm[:, sl_n])
            cb2 = _dot_nt(c_g, jnp.concatenate([b_g, b_g], axis=0))
            state = st_ref[g]
            y_in = _dot(c_g, state) * e_in[:, gl]
            ys = []
            for p in range(gw // LANES):
                ls = slice(gw * g + LANES * p, gw * g + LANES * (p + 1))
                m = decay[:, ls] * cb2
                xp = xd[:, ls]
                x2 = jnp.concatenate([jnp.where(first_head, xp, 0.0), jnp.where(first_head, 0.0, xp)], axis=0)
                ys.append(_dot(m, x2))
            y_s[sl, gl] = y_in + jnp.concatenate(ys, axis=1)
            st_ref[g] = state * e_last[:, gl] + _dot_tn(b_g, xw[:, gl])
        return carry

    lax.fori_loop(0, ts // CHUNK, chunk_body, 0)

    d_skip = wv_ref[0:1, :]
    gn_w = wv_ref[1:2, :]
    y = (y_s[...] + d_skip * xs) * _silu(z)
    outs = []
    for g in range(SSM_GROUPS):
        yg = y[:, gw * g:gw * (g + 1)]
        outs.append(yg * lax.rsqrt(jnp.mean(yg * yg, axis=-1, keepdims=True) + NORM_EPS))
    o_ref[...] = (jnp.concatenate(outs, axis=1) * gn_w).astype(o_ref.dtype)


def _merge_kernel(x_ref, ya_ref, yb_ref, yc_ref, gmix_ref, wg_ref, wba_ref, wbb_ref, wbc_ref, wout_ref,
                  gffn_ref, wfi_ref, wfo_ref, gfin_ref, o_ref, *, final_norm):
    x = x_ref[...]
    h = _rmsnorm(x, gmix_ref[...])
    gates = _sigmoid(_dot(h, wg_ref[...]))
    merged = (gates[:, 0:D_MODEL] * jnp.dot(ya_ref[...], wba_ref[...], preferred_element_type=F32)
              + gates[:, D_MODEL:2 * D_MODEL] * jnp.dot(yb_ref[...], wbb_ref[...], preferred_element_type=F32)
              + gates[:, 2 * D_MODEL:] * jnp.dot(yc_ref[...], wbc_ref[...], preferred_element_type=F32))
    x = x + _dot(merged, wout_ref[...])
    h = _rmsnorm(x, gffn_ref[...])
    gu = _dot(h, wfi_ref[...])
    act = _silu(gu[:, 0:FFN_HIDDEN]) * gu[:, FFN_HIDDEN:]
    x = x + _dot(act, wfo_ref[...])
    if final_norm:
        x = _rmsnorm(x, gfin_ref[...])
    o_ref[...] = x


def _const_spec(arr):
    nd = arr.ndim
    return pl.BlockSpec(arr.shape, lambda b, s, _nd=nd: (0,) * _nd, pipeline_mode=pl.Buffered(1))


def _seq_spec(ts, width):
    return pl.BlockSpec((None, ts, width), lambda b, s: (b, s, 0))


def _params():
    return pltpu.CompilerParams(dimension_semantics=("arbitrary", "arbitrary"),
                                vmem_limit_bytes=VMEM_LIMIT_BYTES)


def _run_mixer(kernel_fn, x, consts, out_width, scratch, ts):
    bsz, seq, _ = x.shape
    return pl.pallas_call(
        functools.partial(kernel_fn, ts=ts),
        grid=(bsz, seq // ts),
        in_specs=[_seq_spec(ts, D_MODEL)] + [_const_spec(c) for c in consts],
        out_specs=_seq_spec(ts, out_width),
        out_shape=jax.ShapeDtypeStruct((bsz, seq, out_width), BF16),
        scratch_shapes=scratch,
        compiler_params=_params(),
    )(x, *consts)


def _row(v):
    return v.astype(F32).reshape(1, -1)


def _pad_cols(m, width):
    return jnp.pad(m, ((0, 0), (0, width - m.shape[1])))


def _rows8(rows, width):
    m = jnp.concatenate([_row(r) for r in rows], axis=0)
    return jnp.pad(m, ((0, SUBLANES - m.shape[0]), (0, width - m.shape[1])))


def _head_sum_matrix(width, head_dim):
    idx = np.arange(width) // head_dim
    return jnp.asarray(idx[:, None] == idx[None, :], dtype=BF16)


def _tri_ones(n):
    return jnp.asarray(np.tril(np.ones((n, n), np.float32)), dtype=BF16)


def _rwkv_branch(x, gain, w_in, mu, w0, w_up, a0, a_up, g_up, k_k, k_a, r_k, gn_w, gn_b, ts):
    W = RWKV_WIDTH
    dr, ir, gr = RWKV_DECAY_RANK, RWKV_ICL_RANK, RWKV_GATE_RANK
    pad = RWKV_GATE_PAD - gr
    w = _pad_cols(w_in, RWKV_U_COLS).astype(BF16)
    mu_p = _pad_cols(_row(mu), RWKV_U_COLS)
    wlr = jnp.zeros((LANES, 2 * W), F32).at[0:dr, 0:W].set(w_up).at[dr:dr + ir, W:].set(a_up).astype(BF16)
    blr = jnp.concatenate([_row(w0), _row(a0)], axis=1)
    gup = jnp.pad(g_up, ((0, pad), (0, 0))).astype(BF16)
    pv = _rows8([k_k, k_a, r_k.reshape(-1), gn_w, gn_b], W)
    consts = [_row(gain), w, mu_p, wlr, blr, gup, pv, _head_sum_matrix(MXU_TILE, RWKV_HEAD_DIM), _tri_ones(CHUNK)]
    scratch = ([pltpu.VMEM((ts + SUBLANES, RWKV_U_COLS), F32),
                pltpu.VMEM((W // LANES, LANES, LANES), F32)]
               + [pltpu.VMEM((ts, W), F32) for _ in range(7)])
    return _run_mixer(_rwkv_kernel, x, consts, W, scratch, ts)


def _hgrn_branch(x, gain, w_in, lb, gn_w, ts):
    W = HGRN_WIDTH
    lb = lb.astype(F32)
    pv = _rows8([jnp.log(lb), jnp.log1p(-lb), gn_w, 1.0 - lb], W)
    t_i = np.arange(SUB)[:, None]
    pair = np.arange(SUB * SUB)[None, :]
    sel = jnp.asarray((pair // SUB == t_i) & (pair % SUB <= t_i), dtype=BF16)
    bsel = jnp.asarray(np.arange(ts // SUB)[:, None] == (np.arange(ts) // SUB)[None, :], dtype=BF16)
    consts = [_row(gain), w_in.astype(BF16), pv, _tri_ones(CHUNK), jnp.ones((LANES, LANES), BF16), sel, bsel]
    scratch = ([pltpu.VMEM((HGRN_HEADS, HGRN_HEAD_DIM, HGRN_HEAD_DIM), F32)]
               + [pltpu.VMEM((ts, W), F32) for _ in range(5)])
    return _run_mixer(_hgrn_kernel, x, consts, W, scratch, ts)


def _ssm_branch(x, gain, w_in, conv_w, conv_b, dt_bias, a_log, d_skip, gn_w, ts):
    W = SSM_WIDTH
    w = _pad_cols(w_in, SSM_U_COLS).astype(BF16)
    cw = jnp.pad(conv_w.astype(F32).T, ((0, SUBLANES - SSM_CONV_WIDTH), (0, 0)))
    hv = _rows8([dt_bias, -jnp.exp(a_log.astype(F32))], LANES)
    wv = _rows8([jnp.repeat(d_skip.astype(F32), SSM_HEAD_DIM), gn_w], W)
    head_of_lane = np.arange(W) // SSM_HEAD_DIM
    expand = jnp.asarray(np.arange(LANES)[:, None] == head_of_lane[None, :], dtype=BF16)
    consts = [_row(gain), w, cw, _row(conv_b), hv, wv, _tri_ones(CHUNK), expand]
    scratch = [pltpu.VMEM((ts + SUBLANES, SSM_CONV_DIM), F32),
               pltpu.VMEM((SSM_GROUPS, SSM_STATE, SSM_GROUP_WIDTH), F32),
               pltpu.VMEM((ts, W), F32), pltpu.VMEM((ts, W), F32),
               pltpu.VMEM((ts, SSM_BC), F32), pltpu.VMEM((ts, SSM_BC), F32),
               pltpu.VMEM((ts, W), F32)]
    return _run_mixer(_ssm_kernel, x, consts, W, scratch, ts)


def _merge_ffn(x, ya, yb, yc, gain_mix, w_gate, w_branch, w_out, gain_ffn, w_ffn_in, w_ffn_out, gain_final,
               final_norm, ts):
    bsz, seq, _ = x.shape
    wb = w_branch.astype(BF16)
    consts = [_row(gain_mix), w_gate.astype(BF16), wb[:RWKV_WIDTH], wb[RWKV_WIDTH:RWKV_WIDTH + HGRN_WIDTH],
              wb[RWKV_WIDTH + HGRN_WIDTH:], w_out.astype(BF16), _row(gain_ffn), w_ffn_in.astype(BF16),
              w_ffn_out.astype(BF16), _row(gain_final)]
    seqs = [x, ya, yb, yc]
    return pl.pallas_call(
        functools.partial(_merge_kernel, final_norm=final_norm),
        grid=(bsz, seq // ts),
        in_specs=[_seq_spec(ts, a.shape[-1]) for a in seqs] + [_const_spec(c) for c in consts],
        out_specs=_seq_spec(ts, D_MODEL),
        out_shape=jax.ShapeDtypeStruct((bsz, seq, D_MODEL), F32),
        compiler_params=_params(),
    )(*seqs, *consts)


def kernel(x, norm_mix, w_in, rwkv_mu, rwkv_w0, rwkv_w_up, rwkv_a0, rwkv_a_up, rwkv_g_up, rwkv_k_k, rwkv_k_a,
           rwkv_r_k, rwkv_gn_w, rwkv_gn_b, hgrn_lb_logits, hgrn_gn_w, ssm_conv_w, ssm_conv_b, ssm_dt_bias,
           ssm_a_log, ssm_d, ssm_gn_w, w_branch, w_out, norm_ffn, w_ffn_in, w_ffn_out, norm_final):
    bsz, seq, d = x.shape
    assert d == D_MODEL
    depth = w_in.shape[0]
    ts = min(SEQ_BLOCK, seq)
    tm = min(MERGE_BLOCK, seq)
    assert seq % ts == 0 and seq % tm == 0 and ts % CHUNK == 0

    rwkv_cols = 3 * RWKV_WIDTH + RWKV_DECAY_RANK + RWKV_ICL_RANK + RWKV_GATE_RANK
    off_hgrn = rwkv_cols
    off_ssm = off_hgrn + 4 * HGRN_WIDTH
    off_gate = off_ssm + SSM_WIDTH + SSM_CONV_DIM + SSM_HEADS

    cs = jnp.cumsum(jax.nn.softmax(hgrn_lb_logits.astype(F32), axis=0), axis=0)
    lbs = cs - cs[:1]

    x = x.astype(F32)
    for l in range(depth):
        wi = w_in[l]
        ya = _rwkv_branch(x, norm_mix[l], wi[:, :off_hgrn], rwkv_mu[l], rwkv_w0[l], rwkv_w_up[l], rwkv_a0[l],
                          rwkv_a_up[l], rwkv_g_up[l], rwkv_k_k[l], rwkv_k_a[l], rwkv_r_k[l], rwkv_gn_w[l],
                          rwkv_gn_b[l], ts)
        yb = _hgrn_branch(x, norm_mix[l], wi[:, off_hgrn:off_ssm], lbs[l], hgrn_gn_w[l], ts)
        yc = _ssm_branch(x, norm_mix[l], wi[:, off_ssm:off_gate], ssm_conv_w[l], ssm_conv_b[l], ssm_dt_bias[l],
                         ssm_a_log[l], ssm_d[l], ssm_gn_w[l], ts)
        x = _merge_ffn(x, ya, yb, yc, norm_mix[l], wi[:, off_gate:], w_branch[l], w_out[l], norm_ffn[l],
                       w_ffn_in[l], w_ffn_out[l], norm_final, l == depth - 1, tm)
    return x
```

```python
import functools

import numpy as np
import jax
import jax.numpy as jnp
from jax import lax
from jax.experimental import pallas as pl
from jax.experimental.pallas import tpu as pltpu

F32 = jnp.float32
BF16 = jnp.bfloat16

D_MODEL = 1024
CHUNK = 64
SUB = 16
HGRN_FAST_LOG_RANGE = 60.0
NORM_EPS = 1e-5
LANES = 128
SUBLANES = 8
MXU_TILE = 256
VMEM_LIMIT_BYTES = 56 * 1024 * 1024

RWKV_HEAD_DIM = 64
RWKV_WIDTH = D_MODEL
RWKV_DECAY_RANK = 64
RWKV_ICL_RANK = 64
RWKV_GATE_RANK = 160
RWKV_GATE_PAD = 256
RWKV_GN_EPS = 64e-5
RWKV_U_COLS = 3 * RWKV_WIDTH + LANES + RWKV_GATE_PAD

HGRN_HEAD_DIM = 128
HGRN_WIDTH = D_MODEL
HGRN_HEADS = HGRN_WIDTH // HGRN_HEAD_DIM

SSM_WIDTH = 2 * D_MODEL
SSM_HEAD_DIM = 64
SSM_HEADS = SSM_WIDTH // SSM_HEAD_DIM
SSM_GROUPS = 4
SSM_STATE = 128
SSM_CONV_WIDTH = 4
SSM_BC = SSM_GROUPS * SSM_STATE
SSM_CONV_DIM = SSM_WIDTH + 2 * SSM_BC
SSM_GROUP_WIDTH = SSM_WIDTH // SSM_GROUPS
SSM_U_COLS = SSM_WIDTH + SSM_CONV_DIM + LANES

FFN_HIDDEN = ((8 * D_MODEL + 3 * 256 - 1) // (3 * 256)) * 256

SEQ_BLOCK = 256
MERGE_BLOCK = 256


def _bf(x):
    return x if x.dtype == BF16 else x.astype(BF16)


def _dot(a, b):
    return jnp.dot(_bf(a), _bf(b), preferred_element_type=F32)


def _dot_nt(a, b):
    return lax.dot_general(_bf(a), _bf(b), (((1,), (1,)), ((), ())), preferred_element_type=F32)


def _dot_tn(a, b):
    return lax.dot_general(_bf(a), _bf(b), (((0,), (0,)), ((), ())), preferred_element_type=F32)


def _split(x, n):
    parts = []
    rest = x
    for i in range(n):
        p = rest.astype(BF16)
        parts.append(p)
        if i + 1 < n:
            rest = rest - p.astype(F32)
    return parts


def _dot_sel_left(sel, x, passes=3):
    out = None
    for p in _split(x, passes):
        t = jnp.dot(sel, p, preferred_element_type=F32)
        out = t if out is None else out + t
    return out


def _dot_sel_right(x, sel, passes=3):
    out = None
    for p in _split(x, passes):
        t = jnp.dot(p, sel, preferred_element_type=F32)
        out = t if out is None else out + t
    return out


def _rmsnorm(x, gain):
    ms = jnp.mean(x * x, axis=-1, keepdims=True)
    return x * lax.rsqrt(ms + NORM_EPS) * gain


def _sigmoid(x):
    return 1.0 / (1.0 + jnp.exp(-x))


def _silu(x):
    return x * _sigmoid(x)


def _softplus(x):
    return jnp.maximum(x, 0.0) + jnp.log(1.0 + jnp.exp(-jnp.abs(x)))


def _iota(shape, axis):
    return lax.broadcasted_iota(jnp.int32, shape, axis)


def _rwkv_kernel(x_ref, gain_ref, w_ref, mu_ref, wlr_ref, blr_ref, gup_ref, pv_ref, hsum_ref, tri_ref,
                 o_ref, ush_ref, st_ref, r_s, lw_s, k_s, v_s, ka_s, kb_s, o_s, g_s, *, ts):
    pairs = RWKV_WIDTH // LANES
    W = RWKV_WIDTH

    @pl.when(pl.program_id(1) == 0)
    def _():
        ush_ref[0:SUBLANES, :] = jnp.zeros((SUBLANES, RWKV_U_COLS), F32)
        st_ref[...] = jnp.zeros(st_ref.shape, F32)

    h = _bf(_rmsnorm(x_ref[...], gain_ref[...]))

    def shifted_slab(lo, hi):
        u = _dot(h, w_ref[:, lo:hi])
        ush_ref[SUBLANES:SUBLANES + ts, lo:hi] = u
        prev = ush_ref[SUBLANES - 1:SUBLANES - 1 + ts, lo:hi]
        ush_ref[0:SUBLANES, lo:hi] = u[ts - SUBLANES:ts, :]
        return u + (prev - u) * mu_ref[:, lo:hi]

    low_rank = shifted_slab(3 * W, RWKV_U_COLS)
    xwa = low_rank[:, 0:LANES]
    xg = low_rank[:, LANES:]
    lane = _iota(xwa.shape, 1)
    lr_in = jnp.where(lane < RWKV_DECAY_RANK, jnp.tanh(xwa), xwa)
    lr = _dot(lr_in, wlr_ref[...]) + blr_ref[...]
    log_w = -jnp.exp(-_softplus(-lr[:, 0:W]) - 0.5)
    a = _sigmoid(lr[:, W:])
    g_s[...] = _dot(_sigmoid(xg), gup_ref[...])
    k = shifted_slab(W, 2 * W)

    k_k = pv_ref[0:1, :]
    k_a = pv_ref[1:2, :]
    r_k = pv_ref[2:3, :]
    gn_w = pv_ref[3:4, :]
    gn_b = pv_ref[4:5, :]
    hsum = hsum_ref[...]

    def head_sum(t):
        return jnp.concatenate([_dot(t[:, MXU_TILE * j:MXU_TILE * (j + 1)], hsum)
                                for j in range(W // MXU_TILE)], axis=1)

    kk = k * k_k
    ss = head_sum(kk * kk)
    kk = kk * lax.rsqrt(jnp.maximum(ss, 1e-24))
    kmod = k * (1.0 + (a - 1.0) * k_a)

    lw_s[...] = log_w
    k_s[...] = kmod
    ka_s[...] = -kk
    kb_s[...] = kk * a
    r_s[...] = shifted_slab(0, W)
    v_s[...] = shifted_slab(2 * W, 3 * W)

    tri = tri_ref[...]
    row2 = _iota((2 * CHUNK, LANES), 0)
    col2 = _iota((2 * CHUNK, LANES), 1)
    same_head = (row2 >= CHUNK) == (col2 >= CHUNK)
    strict = jnp.logical_and(same_head, row2 > col2)
    incl = jnp.logical_and(same_head, row2 >= col2)
    eye = jnp.where(row2 == col2, 1.0, 0.0)
    first_head = _iota((CHUNK, LANES), 1) < RWKV_HEAD_DIM

    def stack2(t):
        return jnp.concatenate([jnp.where(first_head, t, 0.0), jnp.where(first_head, 0.0, t)], axis=0)

    def chunk_body(c, carry):
        c0 = pl.multiple_of(c * CHUNK, CHUNK)
        sl = pl.ds(c0, CHUNK)
        lw = lw_s[sl, :]
        cum = _dot_sel_left(tri, lw, 2)
        e_last = jnp.exp(cum[CHUNK - 1:CHUNK, :])
        e_neg = jnp.exp(-cum)
        rt = r_s[sl, :] * jnp.exp(cum)
        at = ka_s[sl, :] * jnp.exp(cum - lw)
        bt = kb_s[sl, :] * e_neg
        kt = k_s[sl, :] * e_neg
        vv = v_s[sl, :]
        h2 = 2 * CHUNK
        lanes = [slice(LANES * p, LANES * (p + 1)) for p in range(pairs)]
        ar2 = [jnp.concatenate([stack2(at[:, ls]), stack2(rt[:, ls])], axis=0) for ls in lanes]
        bk2 = [jnp.concatenate([stack2(bt[:, ls]), stack2(kt[:, ls])], axis=0) for ls in lanes]
        v2 = [stack2(vv[:, ls]) for ls in lanes]
        gram = [_dot_nt(a, b) for a, b in zip(ar2, bk2)]
        n_ab = [jnp.where(strict, g[0:h2, 0:h2], 0.0) for g in gram]
        n_ak = [jnp.where(strict, g[0:h2, h2:], 0.0) for g in gram]
        n_rbk = [jnp.concatenate([jnp.where(incl, g[h2:, 0:h2], 0.0), jnp.where(incl, g[h2:, h2:], 0.0)], axis=1)
                 for g in gram]
        state = [st_ref[p] for p in range(pairs)]
        from_state = [_dot_nt(a, s) for a, s in zip(ar2, state)]
        rhs = [f[0:h2] + _dot(n, v) for f, n, v in zip(from_state, n_ak, v2)]
        prod = [eye + n for n in n_ab]
        npow = [_dot(n, n) for n in n_ab]
        for _ in range(int(np.log2(CHUNK)) - 2):
            both = [_dot(jnp.concatenate([nk, pr], axis=0), nk) for nk, pr in zip(npow, prod)]
            prod = [pr + b[h2:] for pr, b in zip(prod, both)]
            npow = [b[0:h2] for b in both]
        inv = [pr + _dot(pr, nk) for pr, nk in zip(prod, npow)]
        u2 = [_dot(t, r) for t, r in zip(inv, rhs)]
        uv2 = [jnp.concatenate([u, v], axis=0) for u, v in zip(u2, v2)]
        o2 = [f[h2:] + _dot(n, uv) for f, n, uv in zip(from_state, n_rbk, uv2)]
        for p, ls in enumerate(lanes):
            o_s[sl, ls] = o2[p][0:CHUNK] + o2[p][CHUNK:]
            st_ref[p] = state[p] * e_last[:, ls] + _dot_tn(uv2[p], bk2[p] * e_last[:, ls])
        return carry

    lax.fori_loop(0, ts // CHUNK, chunk_body, 0)

    o = o_s[...]
    inv_n = 1.0 / RWKV_HEAD_DIM
    mean = head_sum(o) * inv_n
    cen = o - mean
    var = head_sum(cen * cen) * inv_n
    o = cen * lax.rsqrt(var + RWKV_GN_EPS) * gn_w + gn_b
    bonus = head_sum(r_s[...] * k_s[...] * r_k)
    o = o + bonus * v_s[...]
    o_ref[...] = (o * g_s[...]).astype(o_ref.dtype)


def _hgrn_kernel(x_ref, gain_ref, w_ref, pv_ref, tri_ref, ones_ref, sel_ref, bsel_ref,
                 o_ref, st_ref, h_s, q_s, k_s, v_s, lf_s, o_s, *, ts):
    W = HGRN_WIDTH
    nsub = CHUNK // SUB

    @pl.when(pl.program_id(1) == 0)
    def _():
        st_ref[...] = jnp.zeros(st_ref.shape, F32)

    h = _bf(_rmsnorm(x_ref[...], gain_ref[...]))
    h_s[...] = h
    log_lb = pv_ref[0:1, :]
    log1m_lb = pv_ref[1:2, :]
    gn_w = pv_ref[2:3, :]
    f_pre = _dot(h, w_ref[:, W:2 * W])
    b = log1m_lb - _softplus(-f_pre)
    mx = jnp.maximum(log_lb, b)
    log_f = mx + jnp.log(1.0 + jnp.exp(-jnp.abs(log_lb - b)))
    lf_s[...] = log_f
    k_s[...] = pv_ref[3:4, :] * _sigmoid(-f_pre)
    q_s[...] = _silu(_dot(h, w_ref[:, 0:W]))
    v_s[...] = _dot(h, w_ref[:, 2 * W:3 * W])

    tri = tri_ref[...]
    ones_sq = ones_ref[...]
    sel = sel_ref[...]
    sub_shift = int(np.log2(SUB))
    rblk = _iota((CHUNK, CHUNK), 0) >> sub_shift
    cblk = _iota((CHUNK, CHUNK), 1) >> sub_shift
    off_mask = cblk < rblk
    rowblk = _iota((CHUNK, LANES), 0) >> sub_shift
    s_idx = _iota((SUB, LANES), 0)

    heads = range(HGRN_HEADS)
    lanes = [slice(LANES * hd, LANES * (hd + 1)) for hd in heads]
    causal = _iota((CHUNK, CHUNK), 1) <= _iota((CHUNK, CHUNK), 0)

    def fast_chunk_body(c, carry):
        c0 = pl.multiple_of(c * CHUNK, CHUNK)
        sl = pl.ds(c0, CHUNK)
        cum = _dot_sel_left(tri, lf_s[sl, :], 3)
        q = q_s[sl, :]
        k = k_s[sl, :]
        v = v_s[sl, :]
        starts = [None] + [cum[SUB * i - 1:SUB * i, :] for i in range(1, nsub)]
        cs = jnp.zeros_like(cum)
        rowblk_w = _iota(cum.shape, 0) >> sub_shift
        for i in range(1, nsub):
            cs = jnp.where(rowblk_w == i, starts[i], cs)
        q_sub = q * jnp.exp(cum - cs)
        k_own = k * jnp.exp(cs - cum)
        k_prev = [None] + [k[0:SUB * i, :] * jnp.exp(jnp.minimum(starts[i] - cum[0:SUB * i, :], 0.0))
                           for i in range(1, nsub)]
        last = cum[CHUNK - 1:CHUNK, :]
        q_in = q * jnp.exp(cum)
        k_end = k * jnp.exp(last - cum)
        e_last = jnp.exp(last)
        state = [st_ref[hd] for hd in heads]
        scores = []
        for ls in lanes:
            rows = []
            for i in range(nsub):
                parts = [k_own[SUB * i:SUB * (i + 1), ls]]
                if i > 0:
                    parts = [k_prev[i][:, ls]] + parts
                if i + 1 < nsub:
                    parts.append(jnp.zeros((CHUNK - SUB * (i + 1), LANES), F32))
                rows.append(_dot_nt(q_sub[SUB * i:SUB * (i + 1), ls], jnp.concatenate(parts, axis=0)))
            scores.append(jnp.where(causal, jnp.concatenate(rows, axis=0), 0.0))
        outs = [_dot(a, v[:, ls]) + _dot_nt(q_in[:, ls], s) for a, ls, s in zip(scores, lanes, state)]
        for hd, ls in enumerate(lanes):
            o_s[sl, ls] = outs[hd]
            st_ref[hd] = state[hd] * e_last[:, ls] + _dot_tn(v[:, ls], k_end[:, ls])
        return carry

    def chunk_body(c, carry):
        c0 = pl.multiple_of(c * CHUNK, CHUNK)
        sl = pl.ds(c0, CHUNK)
        cum_all = _dot_sel_left(tri, lf_s[sl, :], 3)
        q_all = q_s[sl, :]
        k_all = k_s[sl, :]
        v_all = v_s[sl, :]
        for hd in range(HGRN_HEADS):
            ls = slice(LANES * hd, LANES * (hd + 1))
            cum = cum_all[:, ls]
            q = q_all[:, ls]
            k = k_all[:, ls]
            v = v_all[:, ls]
            vb = _bf(v)
            last = cum[CHUNK - 1:CHUNK, :]
            starts = [None] + [cum[SUB * i - 1:SUB * i, :] for i in range(1, nsub)]
            cs = jnp.zeros_like(cum)
            for i in range(1, nsub):
                cs = jnp.where(rowblk == i, starts[i], cs)
            q_sub = q * jnp.exp(cum - cs)
            rows = [jnp.zeros((SUB, CHUNK), F32)]
            for i in range(1, nsub):
                k_i = k * jnp.exp(jnp.minimum(starts[i] - cum, 0.0))
                rows.append(_dot_nt(q_sub[SUB * i:SUB * (i + 1), :], k_i))
            a_off = jnp.where(off_mask, jnp.concatenate(rows, axis=0), 0.0)
            state = st_ref[hd]
            o = _dot(a_off, vb) + _dot_nt(q * jnp.exp(cum), state)
            diag = []
            for j in range(nsub):
                rs = slice(SUB * j, SUB * (j + 1))
                cum_b = cum[rs, :]
                q_b = q[rs, :]
                k_b = k[rs, :]
                pieces = []
                for t in range(SUB):
                    expo = jnp.where(s_idx <= t, cum_b[t:t + 1, :] - cum_b, -1e30)
                    pieces.append(jnp.exp(expo) * k_b * q_b[t:t + 1, :])
                pmat = jnp.concatenate(pieces, axis=0)
                score = _dot(pmat, ones_sq)
                wv = score * jnp.concatenate([v[rs, :]] * SUB, axis=0)
                diag.append(_dot(sel, wv))
            o = o + jnp.concatenate(diag, axis=0)
            o_s[sl, ls] = o
            k_end = k * jnp.exp(last - cum)
            st_ref[hd] = state * jnp.exp(last) + _dot_tn(vb, k_end)
        return carry

    sub_sums = _dot_sel_left(bsel_ref[...], log_f, 3)
    fast_ok = jnp.min(sub_sums) >= -HGRN_FAST_LOG_RANGE

    @pl.when(fast_ok)
    def _():
        lax.fori_loop(0, ts // CHUNK, fast_chunk_body, 0)

    @pl.when(jnp.logical_not(fast_ok))
    def _():
        lax.fori_loop(0, ts // CHUNK, chunk_body, 0)

    gate = _sigmoid(_dot(h_s[...], w_ref[:, 3 * W:]))
    for hd in range(HGRN_HEADS):
        ls = slice(LANES * hd, LANES * (hd + 1))
        o = o_s[:, ls]
        o = o * lax.rsqrt(jnp.mean(o * o, axis=-1, keepdims=True) + NORM_EPS) * gn_w[:, ls]
        o_ref[:, ls] = (o * gate[:, ls]).astype(o_ref.dtype)


def _ssm_kernel(x_ref, gain_ref, w_ref, cw_ref, cb_ref, hv_ref, wv_ref, tri_ref, exp_ref,
                o_ref, xb_ref, st_ref, h_s, xs_s, ec_s, xd_s, b_s, c_s, y_s, *, ts):
    W = SSM_WIDTH
    gw = SSM_GROUP_WIDTH
    pairs = W // LANES

    @pl.when(pl.program_id(1) == 0)
    def _():
        xb_ref[0:SUBLANES, :] = jnp.zeros((SUBLANES, SSM_CONV_DIM), F32)
        st_ref[...] = jnp.zeros(st_ref.shape, F32)

    h = _bf(_rmsnorm(x_ref[...], gain_ref[...]))
    h_s[...] = h

    dt_bias = hv_ref[0:1, :]
    neg_a = hv_ref[1:2, :]
    dt = _softplus(_dot(h, w_ref[:, W + SSM_CONV_DIM:]) + dt_bias)
    log_a = dt * neg_a
    tri = tri_ref[...]
    cum = jnp.concatenate([_dot_sel_left(tri, log_a[CHUNK * c:CHUNK * (c + 1), :], 3)
                           for c in range(ts // CHUNK)], axis=0)
    cum_parts = _split(cum, 3)
    dt_parts = _split(dt, 2)

    def conv_slab(lo, hi):
        xb_ref[SUBLANES:SUBLANES + ts, lo:hi] = _dot(h, w_ref[:, W + lo:W + hi])
        conv = cb_ref[:, lo:hi]
        for j in range(SSM_CONV_WIDTH):
            off = SUBLANES - (SSM_CONV_WIDTH - 1) + j
            conv = conv + xb_ref[off:off + ts, lo:hi] * cw_ref[j:j + 1, lo:hi]
        xb_ref[0:SUBLANES, lo:hi] = xb_ref[ts:ts + SUBLANES, lo:hi]
        return _silu(conv)

    for g in range(SSM_GROUPS):
        lo, hi = gw * g, gw * (g + 1)
        expand = exp_ref[:, lo:hi]
        ec_s[:, lo:hi] = sum(jnp.dot(p, expand, preferred_element_type=F32) for p in cum_parts)
        xs = conv_slab(lo, hi)
        xs_s[:, lo:hi] = xs
        xd_s[:, lo:hi] = xs * sum(jnp.dot(p, expand, preferred_element_type=F32) for p in dt_parts)
    b_s[...] = conv_slab(W, W + SSM_BC)
    c_s[...] = conv_slab(W + SSM_BC, W + 2 * SSM_BC)

    t_idx = _iota((CHUNK, W), 0)
    s_idx = _iota((CHUNK, W), 1) & (CHUNK - 1)
    causal = s_idx <= t_idx
    on_diag = s_idx == t_idx
    first_head = (_iota((CHUNK, LANES), 1) < SSM_HEAD_DIM)

    def chunk_body(c, carry):
        c0 = pl.multiple_of(c * CHUNK, CHUNK)
        sl = pl.ds(c0, CHUNK)
        ec = ec_s[sl, :]
        xd = xd_s[sl, :]
        bm = b_s[sl, :]
        cm = c_s[sl, :]
        last = ec[CHUNK - 1:CHUNK, :]
        by_src = jnp.sum(jnp.where(on_diag, ec, 0.0), axis=0, keepdims=True)
        decay = jnp.exp(jnp.where(causal, ec - by_src, -1e30))
        xw = xd * jnp.exp(last - ec)
        e_in = jnp.exp(ec)
        e_last = jnp.exp(last)
        groups = range(SSM_GROUPS)
        glanes = [slice(gw * g, gw * (g + 1)) for g in groups]
        c_g = [_bf(cm[:, SSM_STATE * g:SSM_STATE * (g + 1)]) for g in groups]
        b_g = [_bf(bm[:, SSM_STATE * g:SSM_STATE * (g + 1)]) for g in groups]
        cb2 = [_dot_nt(c, jnp.concatenate([b, b], axis=0)) for c, b in zip(c_g, b_g)]
        state = [st_ref[g] for g in groups]
        y_in = [_dot(c, s) for c, s in zip(c_g, state)]
        ys = []
        for p in range(pairs):
            ls = slice(LANES * p, LANES * (p + 1))
            m = decay[:, ls] * cb2[(LANES * p) // gw]
            xp = xd[:, ls]
            x2 = jnp.concatenate([jnp.where(first_head, xp, 0.0), jnp.where(first_head, 0.0, xp)], axis=0)
            ys.append(_dot(m, x2))
        per_group = gw // LANES
        for g, gl in enumerate(glanes):
            y_s[sl, gl] = (y_in[g] * e_in[:, gl]
                           + jnp.concatenate(ys[per_group * g:per_group * (g + 1)], axis=1))
            st_ref[g] = state[g] * e_last[:, gl] + _dot_tn(b_g[g], xw[:, gl])
        return carry

    lax.fori_loop(0, ts // CHUNK, chunk_body, 0)

    d_skip = wv_ref[0:1, :]
    gn_w = wv_ref[1:2, :]
    h = h_s[...]
    for g in range(SSM_GROUPS):
        gl = slice(gw * g, gw * (g + 1))
        z = _dot(h, w_ref[:, gl])
        yg = (y_s[:, gl] + d_skip[:, gl] * xs_s[:, gl]) * _silu(z)
        yg = yg * lax.rsqrt(jnp.mean(yg * yg, axis=-1, keepdims=True) + NORM_EPS)
        o_ref[:, gl] = (yg * gn_w[:, gl]).astype(o_ref.dtype)


def _merge_kernel(x_ref, ya_ref, yb_ref, yc_ref, gmix_ref, wg_ref, wba_ref, wbb_ref, wbc_ref, wout_ref,
                  gffn_ref, wfi_ref, wfo_ref, gfin_ref, o_ref, *, final_norm):
    x = x_ref[...]
    h = _rmsnorm(x, gmix_ref[...])
    gates = _sigmoid(_dot(h, wg_ref[...]))
    merged = (gates[:, 0:D_MODEL] * jnp.dot(ya_ref[...], wba_ref[...], preferred_element_type=F32)
              + gates[:, D_MODEL:2 * D_MODEL] * jnp.dot(yb_ref[...], wbb_ref[...], preferred_element_type=F32)
              + gates[:, 2 * D_MODEL:] * jnp.dot(yc_ref[...], wbc_ref[...], preferred_element_type=F32))
    x = x + _dot(merged, wout_ref[...])
    h = _rmsnorm(x, gffn_ref[...])
    gu = _dot(h, wfi_ref[...])
    act = _silu(gu[:, 0:FFN_HIDDEN]) * gu[:, FFN_HIDDEN:]
    x = x + _dot(act, wfo_ref[...])
    if final_norm:
        x = _rmsnorm(x, gfin_ref[...])
    o_ref[...] = x


def _const_spec(arr):
    nd = arr.ndim
    return pl.BlockSpec(arr.shape, lambda b, s, _nd=nd: (0,) * _nd, pipeline_mode=pl.Buffered(1))


def _seq_spec(ts, width):
    return pl.BlockSpec((None, ts, width), lambda b, s: (b, s, 0))


def _params():
    return pltpu.CompilerParams(dimension_semantics=("arbitrary", "arbitrary"),
                                vmem_limit_bytes=VMEM_LIMIT_BYTES)


def _run_mixer(kernel_fn, x, consts, out_width, scratch, ts):
    bsz, seq, _ = x.shape
    return pl.pallas_call(
        functools.partial(kernel_fn, ts=ts),
        grid=(bsz, seq // ts),
        in_specs=[_seq_spec(ts, D_MODEL)] + [_const_spec(c) for c in consts],
        out_specs=_seq_spec(ts, out_width),
        out_shape=jax.ShapeDtypeStruct((bsz, seq, out_width), BF16),
        scratch_shapes=scratch,
        compiler_params=_params(),
    )(x, *consts)


def _row(v):
    return v.astype(F32).reshape(1, -1)


def _pad_cols(m, width):
    return jnp.pad(m, ((0, 0), (0, width - m.shape[1])))


def _rows8(rows, width):
    m = jnp.concatenate([_row(r) for r in rows], axis=0)
    return jnp.pad(m, ((0, SUBLANES - m.shape[0]), (0, width - m.shape[1])))


def _head_sum_matrix(width, head_dim):
    idx = np.arange(width) // head_dim
    return jnp.asarray(idx[:, None] == idx[None, :], dtype=BF16)


def _tri_ones(n):
    return jnp.asarray(np.tril(np.ones((n, n), np.float32)), dtype=BF16)


def _rwkv_branch(x, gain, w_in, mu, w0, w_up, a0, a_up, g_up, k_k, k_a, r_k, gn_w, gn_b, ts):
    W = RWKV_WIDTH
    dr, ir, gr = RWKV_DECAY_RANK, RWKV_ICL_RANK, RWKV_GATE_RANK
    pad = RWKV_GATE_PAD - gr
    w = _pad_cols(w_in, RWKV_U_COLS).astype(BF16)
    mu_p = _pad_cols(_row(mu), RWKV_U_COLS)
    wlr = jnp.zeros((LANES, 2 * W), F32).at[0:dr, 0:W].set(w_up).at[dr:dr + ir, W:].set(a_up).astype(BF16)
    blr = jnp.concatenate([_row(w0), _row(a0)], axis=1)
    gup = jnp.pad(g_up, ((0, pad), (0, 0))).astype(BF16)
    pv = _rows8([k_k, k_a, r_k.reshape(-1), gn_w, gn_b], W)
    consts = [_row(gain), w, mu_p, wlr, blr, gup, pv, _head_sum_matrix(MXU_TILE, RWKV_HEAD_DIM), _tri_ones(CHUNK)]
    scratch = ([pltpu.VMEM((ts + SUBLANES, RWKV_U_COLS), F32),
                pltpu.VMEM((W // LANES, LANES, LANES), F32)]
               + [pltpu.VMEM((ts, W), F32) for _ in range(8)])
    return _run_mixer(_rwkv_kernel, x, consts, W, scratch, ts)


def _hgrn_branch(x, gain, w_in, lb, gn_w, ts):
    W = HGRN_WIDTH
    lb = lb.astype(F32)
    pv = _rows8([jnp.log(lb), jnp.log1p(-lb), gn_w, 1.0 - lb], W)
    t_i = np.arange(SUB)[:, None]
    pair = np.arange(SUB * SUB)[None, :]
    sel = jnp.asarray((pair // SUB == t_i) & (pair % SUB <= t_i), dtype=BF16)
    bsel = jnp.asarray(np.arange(ts // SUB)[:, None] == (np.arange(ts) // SUB)[None, :], dtype=BF16)
    consts = [_row(gain), w_in.astype(BF16), pv, _tri_ones(CHUNK), jnp.ones((LANES, LANES), BF16), sel, bsel]
    scratch = ([pltpu.VMEM((HGRN_HEADS, HGRN_HEAD_DIM, HGRN_HEAD_DIM), F32), pltpu.VMEM((ts, D_MODEL), BF16)]
               + [pltpu.VMEM((ts, W), F32) for _ in range(5)])
    return _run_mixer(_hgrn_kernel, x, consts, W, scratch, ts)


def _ssm_branch(x, gain, w_in, conv_w, conv_b, dt_bias, a_log, d_skip, gn_w, ts):
    W = SSM_WIDTH
    w = _pad_cols(w_in, SSM_U_COLS).astype(BF16)
    cw = jnp.pad(conv_w.astype(F32).T, ((0, SUBLANES - SSM_CONV_WIDTH), (0, 0)))
    hv = _rows8([dt_bias, -jnp.exp(a_log.astype(F32))], LANES)
    wv = _rows8([jnp.repeat(d_skip.astype(F32), SSM_HEAD_DIM), gn_w], W)
    head_of_lane = np.arange(W) // SSM_HEAD_DIM
    expand = jnp.asarray(np.arange(LANES)[:, None] == head_of_lane[None, :], dtype=BF16)
    consts = [_row(gain), w, cw, _row(conv_b), hv, wv, _tri_ones(CHUNK), expand]
    scratch = [pltpu.VMEM((ts + SUBLANES, SSM_CONV_DIM), F32),
               pltpu.VMEM((SSM_GROUPS, SSM_STATE, SSM_GROUP_WIDTH), F32),
               pltpu.VMEM((ts, D_MODEL), BF16), pltpu.VMEM((ts, W), F32),
               pltpu.VMEM((ts, W), F32), pltpu.VMEM((ts, W), F32),
               pltpu.VMEM((ts, SSM_BC), F32), pltpu.VMEM((ts, SSM_BC), F32),
               pltpu.VMEM((ts, W), F32)]
    return _run_mixer(_ssm_kernel, x, consts, W, scratch, ts)


def _merge_ffn(x, ya, yb, yc, gain_mix, w_gate, w_branch, w_out, gain_ffn, w_ffn_in, w_ffn_out, gain_final,
               final_norm, ts):
    bsz, seq, _ = x.shape
    wb = w_branch.astype(BF16)
    consts = [_row(gain_mix), w_gate.astype(BF16), wb[:RWKV_WIDTH], wb[RWKV_WIDTH:RWKV_WIDTH + HGRN_WIDTH],
              wb[RWKV_WIDTH + HGRN_WIDTH:], w_out.astype(BF16), _row(gain_ffn), w_ffn_in.astype(BF16),
              w_ffn_out.astype(BF16), _row(gain_final)]
    seqs = [x, ya, yb, yc]
    return pl.pallas_call(
        functools.partial(_merge_kernel, final_norm=final_norm),
        grid=(bsz, seq // ts),
        in_specs=[_seq_spec(ts, a.shape[-1]) for a in seqs] + [_const_spec(c) for c in consts],
        out_specs=_seq_spec(ts, D_MODEL),
        out_shape=jax.ShapeDtypeStruct((bsz, seq, D_MODEL), F32),
        compiler_params=_params(),
    )(*seqs, *consts)


def kernel(x, norm_mix, w_in, rwkv_mu, rwkv_w0, rwkv_w_up, rwkv_a0, rwkv_a_up, rwkv_g_up, rwkv_k_k, rwkv_k_a,
           rwkv_r_k, rwkv_gn_w, rwkv_gn_b, hgrn_lb_logits, hgrn_gn_w, ssm_conv_w, ssm_conv_b, ssm_dt_bias,
           ssm_a_log, ssm_d, ssm_gn_w, w_branch, w_out, norm_ffn, w_ffn_in, w_ffn_out, norm_final):
    bsz, seq, d = x.shape
    assert d == D_MODEL
    depth = w_in.shape[0]
    ts = min(SEQ_BLOCK, seq)
    tm = min(MERGE_BLOCK, seq)
    assert seq % ts == 0 and seq % tm == 0 and ts % CHUNK == 0

    rwkv_cols = 3 * RWKV_WIDTH + RWKV_DECAY_RANK + RWKV_ICL_RANK + RWKV_GATE_RANK
    off_hgrn = rwkv_cols
    off_ssm = off_hgrn + 4 * HGRN_WIDTH
    off_gate = off_ssm + SSM_WIDTH + SSM_CONV_DIM + SSM_HEADS

    cs = jnp.cumsum(jax.nn.softmax(hgrn_lb_logits.astype(F32), axis=0), axis=0)
    lbs = cs - cs[:1]

    x = x.astype(F32)
    for l in range(depth):
        wi = w_in[l]
        ya = _rwkv_branch(x, norm_mix[l], wi[:, :off_hgrn], rwkv_mu[l], rwkv_w0[l], rwkv_w_up[l], rwkv_a0[l],
                          rwkv_a_up[l], rwkv_g_up[l], rwkv_k_k[l], rwkv_k_a[l], rwkv_r_k[l], rwkv_gn_w[l],
                          rwkv_gn_b[l], ts)
        yb = _hgrn_branch(x, norm_mix[l], wi[:, off_hgrn:off_ssm], lbs[l], hgrn_gn_w[l], ts)
        yc = _ssm_branch(x, norm_mix[l], wi[:, off_ssm:off_gate], ssm_conv_w[l], ssm_conv_b[l], ssm_dt_bias[l],
                         ssm_a_log[l], ssm_d[l], ssm_gn_w[l], ts)
        x = _merge_ffn(x, ya, yb, yc, norm_mix[l], wi[:, off_gate:], w_branch[l], w_out[l], norm_ffn[l],
                       w_ffn_in[l], w_ffn_out[l], norm_final, l == depth - 1, tm)
    return x
```

```python
import functools

import numpy as np
import jax
import jax.numpy as jnp
from jax import lax
from jax.experimental import pallas as pl
from jax.experimental.pallas import tpu as pltpu

F32 = jnp.float32
BF16 = jnp.bfloat16

D_MODEL = 1024
CHUNK = 64
SUB = 16
HGRN_FAST_LOG_RANGE = 60.0
NORM_EPS = 1e-5
LANES = 128
SUBLANES = 8
MXU_TILE = 256
VMEM_LIMIT_BYTES = 56 * 1024 * 1024

RWKV_HEAD_DIM = 64
RWKV_WIDTH = D_MODEL
RWKV_DECAY_RANK = 64
RWKV_ICL_RANK = 64
RWKV_GATE_RANK = 160
RWKV_GATE_PAD = 256
RWKV_GN_EPS = 64e-5
RWKV_U_COLS = 3 * RWKV_WIDTH + LANES + RWKV_GATE_PAD

HGRN_HEAD_DIM = 128
HGRN_WIDTH = D_MODEL
HGRN_HEADS = HGRN_WIDTH // HGRN_HEAD_DIM

SSM_WIDTH = 2 * D_MODEL
SSM_HEAD_DIM = 64
SSM_HEADS = SSM_WIDTH // SSM_HEAD_DIM
SSM_GROUPS = 4
SSM_STATE = 128
SSM_CONV_WIDTH = 4
SSM_BC = SSM_GROUPS * SSM_STATE
SSM_CONV_DIM = SSM_WIDTH + 2 * SSM_BC
SSM_GROUP_WIDTH = SSM_WIDTH // SSM_GROUPS
SSM_U_COLS = SSM_WIDTH + SSM_CONV_DIM + LANES

FFN_HIDDEN = ((8 * D_MODEL + 3 * 256 - 1) // (3 * 256)) * 256

SEQ_BLOCK = 256
MERGE_BLOCK = 256


def _bf(x):
    return x if x.dtype == BF16 else x.astype(BF16)


def _dot(a, b):
    return jnp.dot(_bf(a), _bf(b), preferred_element_type=F32)


def _dot_nt(a, b):
    return lax.dot_general(_bf(a), _bf(b), (((1,), (1,)), ((), ())), preferred_element_type=F32)


def _dot_tn(a, b):
    return lax.dot_general(_bf(a), _bf(b), (((0,), (0,)), ((), ())), preferred_element_type=F32)


def _split(x, n):
    parts = []
    rest = x
    for i in range(n):
        p = rest.astype(BF16)
        parts.append(p)
        if i + 1 < n:
            rest = rest - p.astype(F32)
    return parts


def _dot_sel_left(sel, x, passes=3):
    out = None
    for p in _split(x, passes):
        t = jnp.dot(sel, p, preferred_element_type=F32)
        out = t if out is None else out + t
    return out


def _dot_sel_right(x, sel, passes=3):
    out = None
    for p in _split(x, passes):
        t = jnp.dot(p, sel, preferred_element_type=F32)
        out = t if out is None else out + t
    return out


def _rmsnorm(x, gain):
    ms = jnp.mean(x * x, axis=-1, keepdims=True)
    return x * lax.rsqrt(ms + NORM_EPS) * gain


def _sigmoid(x):
    return 1.0 / (1.0 + jnp.exp(-x))


def _silu(x):
    return x * _sigmoid(x)


def _softplus(x):
    return jnp.maximum(x, 0.0) + jnp.log(1.0 + jnp.exp(-jnp.abs(x)))


def _iota(shape, axis):
    return lax.broadcasted_iota(jnp.int32, shape, axis)


def _rwkv_kernel(x_ref, gain_ref, w_ref, mu_ref, wlr_ref, blr_ref, gup_ref, pv_ref, hsum_ref, tri_ref,
                 o_ref, ush_ref, st_ref, r_s, lw_s, k_s, v_s, ka_s, kb_s, o_s, g_s, *, ts):
    pairs = RWKV_WIDTH // LANES
    W = RWKV_WIDTH

    @pl.when(pl.program_id(1) == 0)
    def _():
        ush_ref[0:SUBLANES, :] = jnp.zeros((SUBLANES, RWKV_U_COLS), F32)
        st_ref[...] = jnp.zeros(st_ref.shape, F32)

    h = _bf(_rmsnorm(x_ref[...], gain_ref[...]))

    def project(lo, hi):
        ush_ref[SUBLANES:SUBLANES + ts, lo:hi] = _dot(h, w_ref[:, lo:hi])

    def token_shift(lo, hi):
        full = ush_ref[0:ts + SUBLANES, lo:hi]
        u = full[SUBLANES:, :]
        prev = pltpu.roll(full, 1, 0)[SUBLANES:, :]
        ush_ref[0:SUBLANES, lo:hi] = u[ts - SUBLANES:ts, :]
        return u + (prev - u) * mu_ref[:, lo:hi]

    project(3 * W, RWKV_U_COLS)
    project(W, 2 * W)
    low_rank = token_shift(3 * W, RWKV_U_COLS)
    xwa = low_rank[:, 0:LANES]
    xg = low_rank[:, LANES:]
    lane = _iota(xwa.shape, 1)
    lr_in = jnp.where(lane < RWKV_DECAY_RANK, jnp.tanh(xwa), xwa)
    lr = _dot(lr_in, wlr_ref[...]) + blr_ref[...]
    log_w = -jnp.exp(-_softplus(-lr[:, 0:W]) - 0.5)
    a = _sigmoid(lr[:, W:])
    g_s[...] = _dot(_sigmoid(xg), gup_ref[...])
    project(0, W)
    k = token_shift(W, 2 * W)

    k_k = pv_ref[0:1, :]
    k_a = pv_ref[1:2, :]
    r_k = pv_ref[2:3, :]
    gn_w = pv_ref[3:4, :]
    gn_b = pv_ref[4:5, :]
    hsum = hsum_ref[...]

    def head_sum(t):
        return jnp.concatenate([_dot(t[:, MXU_TILE * j:MXU_TILE * (j + 1)], hsum)
                                for j in range(W // MXU_TILE)], axis=1)

    kk = k * k_k
    ss = head_sum(kk * kk)
    kk = kk * lax.rsqrt(jnp.maximum(ss, 1e-24))
    kmod = k * (1.0 + (a - 1.0) * k_a)

    lw_s[...] = log_w
    k_s[...] = kmod
    ka_s[...] = -kk
    kb_s[...] = kk * a
    project(2 * W, 3 * W)
    r_s[...] = token_shift(0, W)
    v_s[...] = token_shift(2 * W, 3 * W)

    tri = tri_ref[...]
    row2 = _iota((2 * CHUNK, LANES), 0)
    col2 = _iota((2 * CHUNK, LANES), 1)
    same_head = (row2 >= CHUNK) == (col2 >= CHUNK)
    strict = jnp.logical_and(same_head, row2 > col2)
    incl = jnp.logical_and(same_head, row2 >= col2)
    eye = jnp.where(row2 == col2, 1.0, 0.0)
    first_head = _iota((CHUNK, LANES), 1) < RWKV_HEAD_DIM

    def stack2(t):
        return jnp.concatenate([jnp.where(first_head, t, 0.0), jnp.where(first_head, 0.0, t)], axis=0)

    def chunk_body(c, carry):
        c0 = pl.multiple_of(c * CHUNK, CHUNK)
        sl = pl.ds(c0, CHUNK)
        lw = lw_s[sl, :]
        cum = _dot_sel_left(tri, lw, 2)
        e_last = jnp.exp(cum[CHUNK - 1:CHUNK, :])
        e_neg = jnp.exp(-cum)
        rt = r_s[sl, :] * jnp.exp(cum)
        at = ka_s[sl, :] * jnp.exp(cum - lw)
        bt = kb_s[sl, :] * e_neg
        kt = k_s[sl, :] * e_neg
        vv = v_s[sl, :]
        h2 = 2 * CHUNK
        lanes = [slice(LANES * p, LANES * (p + 1)) for p in range(pairs)]
        ar2 = [jnp.concatenate([stack2(at[:, ls]), stack2(rt[:, ls])], axis=0) for ls in lanes]
        bk2 = [jnp.concatenate([stack2(bt[:, ls]), stack2(kt[:, ls])], axis=0) for ls in lanes]
        v2 = [stack2(vv[:, ls]) for ls in lanes]
        gram = [_dot_nt(a, b) for a, b in zip(ar2, bk2)]
        n_ab = [jnp.where(strict, g[0:h2, 0:h2], 0.0) for g in gram]
        n_ak = [jnp.where(strict, g[0:h2, h2:], 0.0) for g in gram]
        n_rbk = [jnp.concatenate([jnp.where(incl, g[h2:, 0:h2], 0.0), jnp.where(incl, g[h2:, h2:], 0.0)], axis=1)
                 for g in gram]
        state = [st_ref[p] for p in range(pairs)]
        from_state = [_dot_nt(a, s) for a, s in zip(ar2, state)]
        rhs = [f[0:h2] + _dot(n, v) for f, n, v in zip(from_state, n_ak, v2)]
        prod = [eye + n for n in n_ab]
        npow = [_dot(n, n) for n in n_ab]
        for _ in range(int(np.log2(CHUNK)) - 2):
            both = [_dot(jnp.concatenate([nk, pr], axis=0), nk) for nk, pr in zip(npow, prod)]
            prod = [pr + b[h2:] for pr, b in zip(prod, both)]
            npow = [b[0:h2] for b in both]
        inv = [pr + _dot(pr, nk) for pr, nk in zip(prod, npow)]
        u2 = [_dot(t, r) for t, r in zip(inv, rhs)]
        uv2 = [jnp.concatenate([u, v], axis=0) for u, v in zip(u2, v2)]
        o2 = [f[h2:] + _dot(n, uv) for f, n, uv in zip(from_state, n_rbk, uv2)]
        for p, ls in enumerate(lanes):
            o_s[sl, ls] = o2[p][0:CHUNK] + o2[p][CHUNK:]
            st_ref[p] = state[p] * e_last[:, ls] + _dot_tn(uv2[p], bk2[p] * e_last[:, ls])
        return carry

    lax.fori_loop(0, ts // CHUNK, chunk_body, 0)

    o = o_s[...]
    inv_n = 1.0 / RWKV_HEAD_DIM
    mean = head_sum(o) * inv_n
    cen = o - mean
    var = head_sum(cen * cen) * inv_n
    o = cen * lax.rsqrt(var + RWKV_GN_EPS) * gn_w + gn_b
    bonus = head_sum(r_s[...] * k_s[...] * r_k)
    o = o + bonus * v_s[...]
    o_ref[...] = (o * g_s[...]).astype(o_ref.dtype)


def _hgrn_kernel(x_ref, gain_ref, w_ref, pv_ref, tri_ref, ones_ref, sel_ref, bsel_ref,
                 o_ref, st_ref, h_s, q_s, k_s, v_s, lf_s, o_s, *, ts):
    W = HGRN_WIDTH
    nsub = CHUNK // SUB

    @pl.when(pl.program_id(1) == 0)
    def _():
        st_ref[...] = jnp.zeros(st_ref.shape, F32)

    h = _bf(_rmsnorm(x_ref[...], gain_ref[...]))
    h_s[...] = h
    log_lb = pv_ref[0:1, :]
    log1m_lb = pv_ref[1:2, :]
    gn_w = pv_ref[2:3, :]
    f_pre = _dot(h, w_ref[:, W:2 * W])
    q_pre = _dot(h, w_ref[:, 0:W])
    b = log1m_lb - _softplus(-f_pre)
    mx = jnp.maximum(log_lb, b)
    log_f = mx + jnp.log(1.0 + jnp.exp(-jnp.abs(log_lb - b)))
    lf_s[...] = log_f
    k_s[...] = pv_ref[3:4, :] * _sigmoid(-f_pre)
    v_s[...] = _dot(h, w_ref[:, 2 * W:3 * W])
    q_s[...] = _silu(q_pre)

    tri = tri_ref[...]
    ones_sq = ones_ref[...]
    sel = sel_ref[...]
    sub_shift = int(np.log2(SUB))
    rblk = _iota((CHUNK, CHUNK), 0) >> sub_shift
    cblk = _iota((CHUNK, CHUNK), 1) >> sub_shift
    off_mask = cblk < rblk
    rowblk = _iota((CHUNK, LANES), 0) >> sub_shift
    s_idx = _iota((SUB, LANES), 0)

    heads = range(HGRN_HEADS)
    lanes = [slice(LANES * hd, LANES * (hd + 1)) for hd in heads]
    causal = _iota((CHUNK, CHUNK), 1) <= _iota((CHUNK, CHUNK), 0)

    def fast_chunk_body(c, carry):
        c0 = pl.multiple_of(c * CHUNK, CHUNK)
        sl = pl.ds(c0, CHUNK)
        cum = _dot_sel_left(tri, lf_s[sl, :], 3)
        q = q_s[sl, :]
        k = k_s[sl, :]
        v = v_s[sl, :]
        starts = [None] + [cum[SUB * i - 1:SUB * i, :] for i in range(1, nsub)]
        cs = jnp.zeros_like(cum)
        rowblk_w = _iota(cum.shape, 0) >> sub_shift
        for i in range(1, nsub):
            cs = jnp.where(rowblk_w == i, starts[i], cs)
        q_sub = q * jnp.exp(cum - cs)
        k_own = k * jnp.exp(cs - cum)
        k_prev = [None] + [k[0:SUB * i, :] * jnp.exp(jnp.minimum(starts[i] - cum[0:SUB * i, :], 0.0))
                           for i in range(1, nsub)]
        last = cum[CHUNK - 1:CHUNK, :]
        q_in = q * jnp.exp(cum)
        k_end = k * jnp.exp(last - cum)
        e_last = jnp.exp(last)
        state = [st_ref[hd] for hd in heads]
        scores = []
        for ls in lanes:
            rows = []
            for i in range(nsub):
                parts = [k_own[SUB * i:SUB * (i + 1), ls]]
                if i > 0:
                    parts = [k_prev[i][:, ls]] + parts
                if i + 1 < nsub:
                    parts.append(jnp.zeros((CHUNK - SUB * (i + 1), LANES), F32))
                rows.append(_dot_nt(q_sub[SUB * i:SUB * (i + 1), ls], jnp.concatenate(parts, axis=0)))
            scores.append(jnp.where(causal, jnp.concatenate(rows, axis=0), 0.0))
        outs = [_dot(a, v[:, ls]) + _dot_nt(q_in[:, ls], s) for a, ls, s in zip(scores, lanes, state)]
        for hd, ls in enumerate(lanes):
            o_s[sl, ls] = outs[hd]
            st_ref[hd] = state[hd] * e_last[:, ls] + _dot_tn(v[:, ls], k_end[:, ls])
        return carry

    def chunk_body(c, carry):
        c0 = pl.multiple_of(c * CHUNK, CHUNK)
        sl = pl.ds(c0, CHUNK)
        cum_all = _dot_sel_left(tri, lf_s[sl, :], 3)
        q_all = q_s[sl, :]
        k_all = k_s[sl, :]
        v_all = v_s[sl, :]
        for hd in range(HGRN_HEADS):
            ls = slice(LANES * hd, LANES * (hd + 1))
            cum = cum_all[:, ls]
            q = q_all[:, ls]
            k = k_all[:, ls]
            v = v_all[:, ls]
            vb = _bf(v)
            last = cum[CHUNK - 1:CHUNK, :]
            starts = [None] + [cum[SUB * i - 1:SUB * i, :] for i in range(1, nsub)]
            cs = jnp.zeros_like(cum)
            for i in range(1, nsub):
                cs = jnp.where(rowblk == i, starts[i], cs)
            q_sub = q * jnp.exp(cum - cs)
            rows = [jnp.zeros((SUB, CHUNK), F32)]
            for i in range(1, nsub):
                k_i = k * jnp.exp(jnp.minimum(starts[i] - cum, 0.0))
                rows.append(_dot_nt(q_sub[SUB * i:SUB * (i + 1), :], k_i))
            a_off = jnp.where(off_mask, jnp.concatenate(rows, axis=0), 0.0)
            state = st_ref[hd]
            o = _dot(a_off, vb) + _dot_nt(q * jnp.exp(cum), state)
            diag = []
            for j in range(nsub):
                rs = slice(SUB * j, SUB * (j + 1))
                cum_b = cum[rs, :]
                q_b = q[rs, :]
                k_b = k[rs, :]
                pieces = []
                for t in range(SUB):
                    expo = jnp.where(s_idx <= t, cum_b[t:t + 1, :] - cum_b, -1e30)
                    pieces.append(jnp.exp(expo) * k_b * q_b[t:t + 1, :])
                pmat = jnp.concatenate(pieces, axis=0)
                score = _dot(pmat, ones_sq)
                wv = score * jnp.concatenate([v[rs, :]] * SUB, axis=0)
                diag.append(_dot(sel, wv))
            o = o + jnp.concatenate(diag, axis=0)
            o_s[sl, ls] = o
            k_end = k * jnp.exp(last - cum)
            st_ref[hd] = state * jnp.exp(last) + _dot_tn(vb, k_end)
        return carry

    sub_sums = _dot_sel_left(bsel_ref[...], log_f, 3)
    fast_ok = jnp.min(sub_sums) >= -HGRN_FAST_LOG_RANGE

    @pl.when(fast_ok)
    def _():
        lax.fori_loop(0, ts // CHUNK, fast_chunk_body, 0)

    @pl.when(jnp.logical_not(fast_ok))
    def _():
        lax.fori_loop(0, ts // CHUNK, chunk_body, 0)

    gate = _sigmoid(_dot(h_s[...], w_ref[:, 3 * W:]))
    for hd in range(HGRN_HEADS):
        ls = slice(LANES * hd, LANES * (hd + 1))
        o = o_s[:, ls]
        o = o * lax.rsqrt(jnp.mean(o * o, axis=-1, keepdims=True) + NORM_EPS) * gn_w[:, ls]
        o_ref[:, ls] = (o * gate[:, ls]).astype(o_ref.dtype)


def _ssm_kernel(x_ref, gain_ref, w_ref, cw_ref, cb_ref, hv_ref, wv_ref, tri_ref, exp_ref,
                o_ref, xb_ref, st_ref, h_s, xs_s, ec_s, xd_s, b_s, c_s, y_s, *, ts):
    W = SSM_WIDTH
    gw = SSM_GROUP_WIDTH
    pairs = W // LANES

    @pl.when(pl.program_id(1) == 0)
    def _():
        xb_ref[0:SUBLANES, :] = jnp.zeros((SUBLANES, SSM_CONV_DIM), F32)
        st_ref[...] = jnp.zeros(st_ref.shape, F32)

    h = _bf(_rmsnorm(x_ref[...], gain_ref[...]))
    h_s[...] = h

    dt_bias = hv_ref[0:1, :]
    neg_a = hv_ref[1:2, :]
    dt = _softplus(_dot(h, w_ref[:, W + SSM_CONV_DIM:]) + dt_bias)
    log_a = dt * neg_a
    tri = tri_ref[...]
    cum = jnp.concatenate([_dot_sel_left(tri, log_a[CHUNK * c:CHUNK * (c + 1), :], 3)
                           for c in range(ts // CHUNK)], axis=0)
    cum_parts = _split(cum, 3)
    dt_parts = _split(dt, 2)

    def project(lo, hi):
        xb_ref[SUBLANES:SUBLANES + ts, lo:hi] = _dot(h, w_ref[:, W + lo:W + hi])

    def conv_silu(lo, hi):
        full = xb_ref[0:ts + SUBLANES, lo:hi]
        conv = cb_ref[:, lo:hi] + full[SUBLANES:, :] * cw_ref[SSM_CONV_WIDTH - 1:SSM_CONV_WIDTH, lo:hi]
        for k in range(1, SSM_CONV_WIDTH):
            j = SSM_CONV_WIDTH - 1 - k
            conv = conv + pltpu.roll(full, k, 0)[SUBLANES:, :] * cw_ref[j:j + 1, lo:hi]
        xb_ref[0:SUBLANES, lo:hi] = xb_ref[ts:ts + SUBLANES, lo:hi]
        return _silu(conv)

    def finish(idx):
        lo, hi = slabs[idx]
        act = conv_silu(lo, hi)
        if idx < SSM_GROUPS:
            expand = exp_ref[:, lo:hi]
            xs_s[:, lo:hi] = act
            xd_s[:, lo:hi] = act * sum(jnp.dot(p, expand, preferred_element_type=F32) for p in dt_parts)
            ec_s[:, lo:hi] = sum(jnp.dot(p, expand, preferred_element_type=F32) for p in cum_parts)
        elif idx == SSM_GROUPS:
            b_s[...] = act
        else:
            c_s[...] = act

    slabs = [(gw * g, gw * (g + 1)) for g in range(SSM_GROUPS)] + [(W, W + SSM_BC), (W + SSM_BC, W + 2 * SSM_BC)]
    project(*slabs[0])
    for idx in range(len(slabs)):
        if idx + 1 < len(slabs):
            project(*slabs[idx + 1])
        finish(idx)

    t_idx = _iota((CHUNK, W), 0)
    s_idx = _iota((CHUNK, W), 1) & (CHUNK - 1)
    causal = s_idx <= t_idx
    on_diag = s_idx == t_idx
    first_head = (_iota((CHUNK, LANES), 1) < SSM_HEAD_DIM)

    def chunk_body(c, carry):
        c0 = pl.multiple_of(c * CHUNK, CHUNK)
        sl = pl.ds(c0, CHUNK)
        ec = ec_s[sl, :]
        xd = xd_s[sl, :]
        bm = b_s[sl, :]
        cm = c_s[sl, :]
        last = ec[CHUNK - 1:CHUNK, :]
        by_src = jnp.sum(jnp.where(on_diag, ec, 0.0), axis=0, keepdims=True)
        decay = jnp.exp(jnp.where(causal, ec - by_src, -1e30))
        xw = xd * jnp.exp(last - ec)
        e_in = jnp.exp(ec)
        e_last = jnp.exp(last)
        groups = range(SSM_GROUPS)
        glanes = [slice(gw * g, gw * (g + 1)) for g in groups]
        c_g = [_bf(cm[:, SSM_STATE * g:SSM_STATE * (g + 1)]) for g in groups]
        b_g = [_bf(bm[:, SSM_STATE * g:SSM_STATE * (g + 1)]) for g in groups]
        cb2 = [_dot_nt(c, jnp.concatenate([b, b], axis=0)) for c, b in zip(c_g, b_g)]
        state = [st_ref[g] for g in groups]
        y_in = [_dot(c, s) for c, s in zip(c_g, state)]
        ys = []
        for p in range(pairs):
            ls = slice(LANES * p, LANES * (p + 1))
            m = decay[:, ls] * cb2[(LANES * p) // gw]
            xp = xd[:, ls]
            x2 = jnp.concatenate([jnp.where(first_head, xp, 0.0), jnp.where(first_head, 0.0, xp)], axis=0)
            ys.append(_dot(m, x2))
        per_group = gw // LANES
        for g, gl in enumerate(glanes):
            y_s[sl, gl] = (y_in[g] * e_in[:, gl]
                           + jnp.concatenate(ys[per_group * g:per_group * (g + 1)], axis=1))
            st_ref[g] = state[g] * e_last[:, gl] + _dot_tn(b_g[g], xw[:, gl])
        return carry

    lax.fori_loop(0, ts // CHUNK, chunk_body, 0)

    d_skip = wv_ref[0:1, :]
    gn_w = wv_ref[1:2, :]
    h = h_s[...]
    for g in range(SSM_GROUPS):
        gl = slice(gw * g, gw * (g + 1))
        z = _dot(h, w_ref[:, gl])
        yg = (y_s[:, gl] + d_skip[:, gl] * xs_s[:, gl]) * _silu(z)
        yg = yg * lax.rsqrt(jnp.mean(yg * yg, axis=-1, keepdims=True) + NORM_EPS)
        o_ref[:, gl] = (yg * gn_w[:, gl]).astype(o_ref.dtype)


def _merge_kernel(x_ref, ya_ref, yb_ref, yc_ref, gmix_ref, wg_ref, wba_ref, wbb_ref, wbc_ref, wout_ref,
                  gffn_ref, wfi_ref, wfo_ref, gfin_ref, o_ref, *, final_norm):
    x = x_ref[...]
    h = _rmsnorm(x, gmix_ref[...])
    gates = _sigmoid(_dot(h, wg_ref[...]))
    merged = (gates[:, 0:D_MODEL] * jnp.dot(ya_ref[...], wba_ref[...], preferred_element_type=F32)
              + gates[:, D_MODEL:2 * D_MODEL] * jnp.dot(yb_ref[...], wbb_ref[...], preferred_element_type=F32)
              + gates[:, 2 * D_MODEL:] * jnp.dot(yc_ref[...], wbc_ref[...], preferred_element_type=F32))
    x = x + _dot(merged, wout_ref[...])
    h = _rmsnorm(x, gffn_ref[...])
    gu = _dot(h, wfi_ref[...])
    act = _silu(gu[:, 0:FFN_HIDDEN]) * gu[:, FFN_HIDDEN:]
    x = x + _dot(act, wfo_ref[...])
    if final_norm:
        x = _rmsnorm(x, gfin_ref[...])
    o_ref[...] = x


def _const_spec(arr):
    nd = arr.ndim
    return pl.BlockSpec(arr.shape, lambda b, s, _nd=nd: (0,) * _nd, pipeline_mode=pl.Buffered(1))


def _seq_spec(ts, width):
    return pl.BlockSpec((None, ts, width), lambda b, s: (b, s, 0))


def _params():
    return pltpu.CompilerParams(dimension_semantics=("arbitrary", "arbitrary"),
                                vmem_limit_bytes=VMEM_LIMIT_BYTES)


def _run_mixer(kernel_fn, x, consts, out_width, scratch, ts):
    bsz, seq, _ = x.shape
    return pl.pallas_call(
        functools.partial(kernel_fn, ts=ts),
        grid=(bsz, seq // ts),
        in_specs=[_seq_spec(ts, D_MODEL)] + [_const_spec(c) for c in consts],
        out_specs=_seq_spec(ts, out_width),
        out_shape=jax.ShapeDtypeStruct((bsz, seq, out_width), BF16),
        scratch_shapes=scratch,
        compiler_params=_params(),
    )(x, *consts)


def _row(v):
    return v.astype(F32).reshape(1, -1)


def _pad_cols(m, width):
    return jnp.pad(m, ((0, 0), (0, width - m.shape[1])))


def _rows8(rows, width):
    m = jnp.concatenate([_row(r) for r in rows], axis=0)
    return jnp.pad(m, ((0, SUBLANES - m.shape[0]), (0, width - m.shape[1])))


def _head_sum_matrix(width, head_dim):
    idx = np.arange(width) // head_dim
    return jnp.asarray(idx[:, None] == idx[None, :], dtype=BF16)


def _tri_ones(n):
    return jnp.asarray(np.tril(np.ones((n, n), np.float32)), dtype=BF16)


def _rwkv_branch(x, gain, w_in, mu, w0, w_up, a0, a_up, g_up, k_k, k_a, r_k, gn_w, gn_b, ts):
    W = RWKV_WIDTH
    dr, ir, gr = RWKV_DECAY_RANK, RWKV_ICL_RANK, RWKV_GATE_RANK
    pad = RWKV_GATE_PAD - gr
    w = _pad_cols(w_in, RWKV_U_COLS).astype(BF16)
    mu_p = _pad_cols(_row(mu), RWKV_U_COLS)
    wlr = jnp.zeros((LANES, 2 * W), F32).at[0:dr, 0:W].set(w_up).at[dr:dr + ir, W:].set(a_up).astype(BF16)
    blr = jnp.concatenate([_row(w0), _row(a0)], axis=1)
    gup = jnp.pad(g_up, ((0, pad), (0, 0))).astype(BF16)
    pv = _rows8([k_k, k_a, r_k.reshape(-1), gn_w, gn_b], W)
    consts = [_row(gain), w, mu_p, wlr, blr, gup, pv, _head_sum_matrix(MXU_TILE, RWKV_HEAD_DIM), _tri_ones(CHUNK)]
    scratch = ([pltpu.VMEM((ts + SUBLANES, RWKV_U_COLS), F32),
                pltpu.VMEM((W // LANES, LANES, LANES), F32)]
               + [pltpu.VMEM((ts, W), F32) for _ in range(8)])
    return _run_mixer(_rwkv_kernel, x, consts, W, scratch, ts)


def _hgrn_branch(x, gain, w_in, lb, gn_w, ts):
    W = HGRN_WIDTH
    lb = lb.astype(F32)
    pv = _rows8([jnp.log(lb), jnp.log1p(-lb), gn_w, 1.0 - lb], W)
    t_i = np.arange(SUB)[:, None]
    pair = np.arange(SUB * SUB)[None, :]
    sel = jnp.asarray((pair // SUB == t_i) & (pair % SUB <= t_i), dtype=BF16)
    bsel = jnp.asarray(np.arange(ts // SUB)[:, None] == (np.arange(ts) // SUB)[None, :], dtype=BF16)
    consts = [_row(gain), w_in.astype(BF16), pv, _tri_ones(CHUNK), jnp.ones((LANES, LANES), BF16), sel, bsel]
    scratch = ([pltpu.VMEM((HGRN_HEADS, HGRN_HEAD_DIM, HGRN_HEAD_DIM), F32), pltpu.VMEM((ts, D_MODEL), BF16)]
               + [pltpu.VMEM((ts, W), F32) for _ in range(5)])
    return _run_mixer(_hgrn_kernel, x, consts, W, scratch, ts)


def _ssm_branch(x, gain, w_in, conv_w, conv_b, dt_bias, a_log, d_skip, gn_w, ts):
    W = SSM_WIDTH
    w = _pad_cols(w_in, SSM_U_COLS).astype(BF16)
    cw = jnp.pad(conv_w.astype(F32).T, ((0, SUBLANES - SSM_CONV_WIDTH), (0, 0)))
    hv = _rows8([dt_bias, -jnp.exp(a_log.astype(F32))], LANES)
    wv = _rows8([jnp.repeat(d_skip.astype(F32), SSM_HEAD_DIM), gn_w], W)
    head_of_lane = np.arange(W) // SSM_HEAD_DIM
    expand = jnp.asarray(np.arange(LANES)[:, None] == head_of_lane[None, :], dtype=BF16)
    consts = [_row(gain), w, cw, _row(conv_b), hv, wv, _tri_ones(CHUNK), expand]
    scratch = [pltpu.VMEM((ts + SUBLANES, SSM_CONV_DIM), F32),
               pltpu.VMEM((SSM_GROUPS, SSM_STATE, SSM_GROUP_WIDTH), F32),
               pltpu.VMEM((ts, D_MODEL), BF16), pltpu.VMEM((ts, W), F32),
               pltpu.VMEM((ts, W), F32), pltpu.VMEM((ts, W), F32),
               pltpu.VMEM((ts, SSM_BC), F32), pltpu.VMEM((ts, SSM_BC), F32),
               pltpu.VMEM((ts, W), F32)]
    return _run_mixer(_ssm_kernel, x, consts, W, scratch, ts)


def _merge_ffn(x, ya, yb, yc, gain_mix, w_gate, w_branch, w_out, gain_ffn, w_ffn_in, w_ffn_out, gain_final,
               final_norm, ts):
    bsz, seq, _ = x.shape
    wb = w_branch.astype(BF16)
    consts = [_row(gain_mix), w_gate.astype(BF16), wb[:RWKV_WIDTH], wb[RWKV_WIDTH:RWKV_WIDTH + HGRN_WIDTH],
              wb[RWKV_WIDTH + HGRN_WIDTH:], w_out.astype(BF16), _row(gain_ffn), w_ffn_in.astype(BF16),
              w_ffn_out.astype(BF16), _row(gain_final)]
    seqs = [x, ya, yb, yc]
    return pl.pallas_call(
        functools.partial(_merge_kernel, final_norm=final_norm),
        grid=(bsz, seq // ts),
        in_specs=[_seq_spec(ts, a.shape[-1]) for a in seqs] + [_const_spec(c) for c in consts],
        out_specs=_seq_spec(ts, D_MODEL),
        out_shape=jax.ShapeDtypeStruct((bsz, seq, D_MODEL), F32),
        compiler_params=_params(),
    )(*seqs, *consts)


def kernel(x, norm_mix, w_in, rwkv_mu, rwkv_w0, rwkv_w_up, rwkv_a0, rwkv_a_up, rwkv_g_up, rwkv_k_k, rwkv_k_a,
           rwkv_r_k, rwkv_gn_w, rwkv_gn_b, hgrn_lb_logits, hgrn_gn_w, ssm_conv_w, ssm_conv_b, ssm_dt_bias,
           ssm_a_log, ssm_d, ssm_gn_w, w_branch, w_out, norm_ffn, w_ffn_in, w_ffn_out, norm_final):
    bsz, seq, d = x.shape
    assert d == D_MODEL
    depth = w_in.shape[0]
    ts = min(SEQ_BLOCK, seq)
    tm = min(MERGE_BLOCK, seq)
    assert seq % ts == 0 and seq % tm == 0 and ts % CHUNK == 0

    rwkv_cols = 3 * RWKV_WIDTH + RWKV_DECAY_RANK + RWKV_ICL_RANK + RWKV_GATE_RANK
    off_hgrn = rwkv_cols
    off_ssm = off_hgrn + 4 * HGRN_WIDTH
    off_gate = off_ssm + SSM_WIDTH + SSM_CONV_DIM + SSM_HEADS

    cs = jnp.cumsum(jax.nn.softmax(hgrn_lb_logits.astype(F32), axis=0), axis=0)
    lbs = cs - cs[:1]

    x = x.astype(F32)
    for l in range(depth):
        wi = w_in[l]
        ya = _rwkv_branch(x, norm_mix[l], wi[:, :off_hgrn], rwkv_mu[l], rwkv_w0[l], rwkv_w_up[l], rwkv_a0[l],
                          rwkv_a_up[l], rwkv_g_up[l], rwkv_k_k[l], rwkv_k_a[l], rwkv_r_k[l], rwkv_gn_w[l],
                          rwkv_gn_b[l], ts)
        yb = _hgrn_branch(x, norm_mix[l], wi[:, off_hgrn:off_ssm], lbs[l], hgrn_gn_w[l], ts)
        yc = _ssm_branch(x, norm_mix[l], wi[:, off_ssm:off_gate], ssm_conv_w[l], ssm_conv_b[l], ssm_dt_bias[l],
                         ssm_a_log[l], ssm_d[l], ssm_gn_w[l], ts)
        x = _merge_ffn(x, ya, yb, yc, norm_mix[l], wi[:, off_gate:], w_branch[l], w_out[l], norm_ffn[l],
                       w_ffn_in[l], w_ffn_out[l], norm_final, l == depth - 1, tm)
    return x
```

```python
import functools

import numpy as np
import jax
import jax.numpy as jnp
from jax import lax
from jax.experimental import pallas as pl
from jax.experimental.pallas import tpu as pltpu

F32 = jnp.float32
BF16 = jnp.bfloat16

D_MODEL = 1024
CHUNK = 64
SUB = 16
HGRN_FAST_LOG_RANGE = 60.0
NORM_EPS = 1e-5
LANES = 128
SUBLANES = 8
MXU_TILE = 256
VMEM_LIMIT_BYTES = 56 * 1024 * 1024

RWKV_HEAD_DIM = 64
RWKV_WIDTH = D_MODEL
RWKV_DECAY_RANK = 64
RWKV_ICL_RANK = 64
RWKV_GATE_RANK = 160
RWKV_GATE_PAD = 256
RWKV_GN_EPS = 64e-5
RWKV_U_COLS = 3 * RWKV_WIDTH + LANES + RWKV_GATE_PAD

HGRN_HEAD_DIM = 128
HGRN_WIDTH = D_MODEL
HGRN_HEADS = HGRN_WIDTH // HGRN_HEAD_DIM

SSM_WIDTH = 2 * D_MODEL
SSM_HEAD_DIM = 64
SSM_HEADS = SSM_WIDTH // SSM_HEAD_DIM
SSM_GROUPS = 4
SSM_STATE = 128
SSM_CONV_WIDTH = 4
SSM_BC = SSM_GROUPS * SSM_STATE
SSM_CONV_DIM = SSM_WIDTH + 2 * SSM_BC
SSM_GROUP_WIDTH = SSM_WIDTH // SSM_GROUPS
SSM_U_COLS = SSM_WIDTH + SSM_CONV_DIM + LANES

FFN_HIDDEN = ((8 * D_MODEL + 3 * 256 - 1) // (3 * 256)) * 256

SEQ_BLOCK = 512
MERGE_BLOCK = 256


def _bf(x):
    return x if x.dtype == BF16 else x.astype(BF16)


def _dot(a, b):
    return jnp.dot(_bf(a), _bf(b), preferred_element_type=F32)


def _dot_nt(a, b):
    return lax.dot_general(_bf(a), _bf(b), (((1,), (1,)), ((), ())), preferred_element_type=F32)


def _dot_tn(a, b):
    return lax.dot_general(_bf(a), _bf(b), (((0,), (0,)), ((), ())), preferred_element_type=F32)


def _split(x, n):
    parts = []
    rest = x
    for i in range(n):
        p = rest.astype(BF16)
        parts.append(p)
        if i + 1 < n:
            rest = rest - p.astype(F32)
    return parts


def _dot_sel_left(sel, x, passes=3):
    out = None
    for p in _split(x, passes):
        t = jnp.dot(sel, p, preferred_element_type=F32)
        out = t if out is None else out + t
    return out


def _dot_sel_right(x, sel, passes=3):
    out = None
    for p in _split(x, passes):
        t = jnp.dot(p, sel, preferred_element_type=F32)
        out = t if out is None else out + t
    return out


def _rmsnorm(x, gain):
    ms = jnp.mean(x * x, axis=-1, keepdims=True)
    return x * lax.rsqrt(ms + NORM_EPS) * gain


def _sigmoid(x):
    return 1.0 / (1.0 + jnp.exp(-x))


def _silu(x):
    return x * _sigmoid(x)


def _softplus(x):
    return jnp.maximum(x, 0.0) + jnp.log(1.0 + jnp.exp(-jnp.abs(x)))


def _iota(shape, axis):
    return lax.broadcasted_iota(jnp.int32, shape, axis)


def _rwkv_kernel(x_ref, gain_ref, w_ref, mu_ref, wlr_ref, blr_ref, gup_ref, pv_ref, hsum_ref, tri_ref,
                 o_ref, ush_ref, st_ref, r_s, lw_s, k_s, v_s, ka_s, kb_s, o_s, g_s, *, ts):
    pairs = RWKV_WIDTH // LANES
    W = RWKV_WIDTH

    @pl.when(pl.program_id(1) == 0)
    def _():
        ush_ref[0:SUBLANES, :] = jnp.zeros((SUBLANES, RWKV_U_COLS), F32)
        st_ref[...] = jnp.zeros(st_ref.shape, F32)

    h = _bf(_rmsnorm(x_ref[...], gain_ref[...]))

    def project(lo, hi):
        ush_ref[SUBLANES:SUBLANES + ts, lo:hi] = _dot(h, w_ref[:, lo:hi])

    def token_shift(lo, hi):
        full = ush_ref[0:ts + SUBLANES, lo:hi]
        u = full[SUBLANES:, :]
        prev = pltpu.roll(full, 1, 0)[SUBLANES:, :]
        ush_ref[0:SUBLANES, lo:hi] = u[ts - SUBLANES:ts, :]
        return u + (prev - u) * mu_ref[:, lo:hi]

    project(3 * W, RWKV_U_COLS)
    project(W, 2 * W)
    low_rank = token_shift(3 * W, RWKV_U_COLS)
    xwa = low_rank[:, 0:LANES]
    xg = low_rank[:, LANES:]
    lane = _iota(xwa.shape, 1)
    lr_in = jnp.where(lane < RWKV_DECAY_RANK, jnp.tanh(xwa), xwa)
    lr = _dot(lr_in, wlr_ref[...]) + blr_ref[...]
    log_w = -jnp.exp(-_softplus(-lr[:, 0:W]) - 0.5)
    a = _sigmoid(lr[:, W:])
    g_s[...] = _dot(_sigmoid(xg), gup_ref[...])
    project(0, W)
    k = token_shift(W, 2 * W)

    k_k = pv_ref[0:1, :]
    k_a = pv_ref[1:2, :]
    r_k = pv_ref[2:3, :]
    gn_w = pv_ref[3:4, :]
    gn_b = pv_ref[4:5, :]
    hsum = hsum_ref[...]

    def head_sum(t):
        return jnp.concatenate([_dot(t[:, MXU_TILE * j:MXU_TILE * (j + 1)], hsum)
                                for j in range(W // MXU_TILE)], axis=1)

    kk = k * k_k
    ss = head_sum(kk * kk)
    kk = kk * lax.rsqrt(jnp.maximum(ss, 1e-24))
    kmod = k * (1.0 + (a - 1.0) * k_a)

    lw_s[...] = log_w
    k_s[...] = kmod
    ka_s[...] = -kk
    kb_s[...] = kk * a
    project(2 * W, 3 * W)
    r_s[...] = token_shift(0, W)
    v_s[...] = token_shift(2 * W, 3 * W)

    tri = tri_ref[...]
    row2 = _iota((2 * CHUNK, LANES), 0)
    col2 = _iota((2 * CHUNK, LANES), 1)
    same_head = (row2 >= CHUNK) == (col2 >= CHUNK)
    strict = jnp.logical_and(same_head, row2 > col2)
    incl = jnp.logical_and(same_head, row2 >= col2)
    eye = jnp.where(row2 == col2, 1.0, 0.0)
    first_head = _iota((CHUNK, LANES), 1) < RWKV_HEAD_DIM

    def stack2(t):
        return jnp.concatenate([jnp.where(first_head, t, 0.0), jnp.where(first_head, 0.0, t)], axis=0)

    def chunk_body(c, carry):
        c0 = pl.multiple_of(c * CHUNK, CHUNK)
        sl = pl.ds(c0, CHUNK)
        lw = lw_s[sl, :]
        cum = _dot_sel_left(tri, lw, 2)
        e_last = jnp.exp(cum[CHUNK - 1:CHUNK, :])
        e_neg = jnp.exp(-cum)
        rt = r_s[sl, :] * jnp.exp(cum)
        at = ka_s[sl, :] * jnp.exp(cum - lw)
        bt = kb_s[sl, :] * e_neg
        kt = k_s[sl, :] * e_neg
        vv = v_s[sl, :]
        h2 = 2 * CHUNK
        lanes = [slice(LANES * p, LANES * (p + 1)) for p in range(pairs)]
        ar2 = [jnp.concatenate([stack2(at[:, ls]), stack2(rt[:, ls])], axis=0) for ls in lanes]
        bk2 = [jnp.concatenate([stack2(bt[:, ls]), stack2(kt[:, ls])], axis=0) for ls in lanes]
        v2 = [stack2(vv[:, ls]) for ls in lanes]
        gram = [_dot_nt(a, b) for a, b in zip(ar2, bk2)]
        n_ab = [jnp.where(strict, g[0:h2, 0:h2], 0.0) for g in gram]
        n_ak = [jnp.where(strict, g[0:h2, h2:], 0.0) for g in gram]
        n_rbk = [jnp.concatenate([jnp.where(incl, g[h2:, 0:h2], 0.0), jnp.where(incl, g[h2:, h2:], 0.0)], axis=1)
                 for g in gram]
        state = [st_ref[p] for p in range(pairs)]
        from_state = [_dot_nt(a, s) for a, s in zip(ar2, state)]
        rhs = [f[0:h2] + _dot(n, v) for f, n, v in zip(from_state, n_ak, v2)]
        prod = [eye + n for n in n_ab]
        npow = [_dot(n, n) for n in n_ab]
        for _ in range(int(np.log2(CHUNK)) - 2):
            both = [_dot(jnp.concatenate([nk, pr], axis=0), nk) for nk, pr in zip(npow, prod)]
            prod = [pr + b[h2:] for pr, b in zip(prod, both)]
            npow = [b[0:h2] for b in both]
        inv = [pr + _dot(pr, nk) for pr, nk in zip(prod, npow)]
        u2 = [_dot(t, r) for t, r in zip(inv, rhs)]
        uv2 = [jnp.concatenate([u, v], axis=0) for u, v in zip(u2, v2)]
        o2 = [f[h2:] + _dot(n, uv) for f, n, uv in zip(from_state, n_rbk, uv2)]
        for p, ls in enumerate(lanes):
            o_s[sl, ls] = o2[p][0:CHUNK] + o2[p][CHUNK:]
            st_ref[p] = state[p] * e_last[:, ls] + _dot_tn(uv2[p], bk2[p] * e_last[:, ls])
        return carry

    lax.fori_loop(0, ts // CHUNK, chunk_body, 0)

    o = o_s[...]
    inv_n = 1.0 / RWKV_HEAD_DIM
    mean = head_sum(o) * inv_n
    cen = o - mean
    var = head_sum(cen * cen) * inv_n
    o = cen * lax.rsqrt(var + RWKV_GN_EPS) * gn_w + gn_b
    bonus = head_sum(r_s[...] * k_s[...] * r_k)
    o = o + bonus * v_s[...]
    o_ref[...] = (o * g_s[...]).astype(o_ref.dtype)


def _hgrn_kernel(x_ref, gain_ref, w_ref, pv_ref, tri_ref, ones_ref, sel_ref, bsel_ref,
                 o_ref, st_ref, h_s, q_s, k_s, v_s, lf_s, o_s, *, ts):
    W = HGRN_WIDTH
    nsub = CHUNK // SUB

    @pl.when(pl.program_id(1) == 0)
    def _():
        st_ref[...] = jnp.zeros(st_ref.shape, F32)

    h = _bf(_rmsnorm(x_ref[...], gain_ref[...]))
    h_s[...] = h
    log_lb = pv_ref[0:1, :]
    log1m_lb = pv_ref[1:2, :]
    gn_w = pv_ref[2:3, :]
    f_pre = _dot(h, w_ref[:, W:2 * W])
    q_pre = _dot(h, w_ref[:, 0:W])
    e = jnp.exp(-jnp.abs(f_pre))
    b = log1m_lb - (jnp.maximum(-f_pre, 0.0) + jnp.log(1.0 + e))
    mx = jnp.maximum(log_lb, b)
    log_f = mx + jnp.log(1.0 + jnp.exp(-jnp.abs(log_lb - b)))
    lf_s[...] = log_f
    k_s[...] = pv_ref[3:4, :] * jnp.where(f_pre > 0.0, e, 1.0) / (1.0 + e)
    v_s[...] = _dot(h, w_ref[:, 2 * W:3 * W])
    q_s[...] = _silu(q_pre)

    tri = tri_ref[...]
    ones_sq = ones_ref[...]
    sel = sel_ref[...]
    sub_shift = int(np.log2(SUB))
    rblk = _iota((CHUNK, CHUNK), 0) >> sub_shift
    cblk = _iota((CHUNK, CHUNK), 1) >> sub_shift
    off_mask = cblk < rblk
    rowblk = _iota((CHUNK, LANES), 0) >> sub_shift
    s_idx = _iota((SUB, LANES), 0)

    heads = range(HGRN_HEADS)
    lanes = [slice(LANES * hd, LANES * (hd + 1)) for hd in heads]
    causal = _iota((CHUNK, CHUNK), 1) <= _iota((CHUNK, CHUNK), 0)

    def fast_chunk_body(c, carry):
        c0 = pl.multiple_of(c * CHUNK, CHUNK)
        sl = pl.ds(c0, CHUNK)
        cum = _dot_sel_left(tri, lf_s[sl, :], 3)
        q = q_s[sl, :]
        k = k_s[sl, :]
        v = v_s[sl, :]
        starts = [None] + [cum[SUB * i - 1:SUB * i, :] for i in range(1, nsub)]
        cs = jnp.zeros_like(cum)
        rowblk_w = _iota(cum.shape, 0) >> sub_shift
        for i in range(1, nsub):
            cs = jnp.where(rowblk_w == i, starts[i], cs)
        q_sub = q * jnp.exp(cum - cs)
        k_own = k * jnp.exp(cs - cum)
        k_prev = [None] + [k[0:SUB * i, :] * jnp.exp(jnp.minimum(starts[i] - cum[0:SUB * i, :], 0.0))
                           for i in range(1, nsub)]
        last = cum[CHUNK - 1:CHUNK, :]
        q_in = q * jnp.exp(cum)
        k_end = k * jnp.exp(last - cum)
        e_last = jnp.exp(last)
        state = [st_ref[hd] for hd in heads]
        scores = []
        for ls in lanes:
            rows = []
            for i in range(nsub):
                parts = [k_own[SUB * i:SUB * (i + 1), ls]]
                if i > 0:
                    parts = [k_prev[i][:, ls]] + parts
                if i + 1 < nsub:
                    parts.append(jnp.zeros((CHUNK - SUB * (i + 1), LANES), F32))
                rows.append(_dot_nt(q_sub[SUB * i:SUB * (i + 1), ls], jnp.concatenate(parts, axis=0)))
            scores.append(jnp.where(causal, jnp.concatenate(rows, axis=0), 0.0))
        outs = [_dot(a, v[:, ls]) + _dot_nt(q_in[:, ls], s) for a, ls, s in zip(scores, lanes, state)]
        for hd, ls in enumerate(lanes):
            o_s[sl, ls] = outs[hd]
            st_ref[hd] = state[hd] * e_last[:, ls] + _dot_tn(v[:, ls], k_end[:, ls])
        return carry

    def chunk_body(c, carry):
        c0 = pl.multiple_of(c * CHUNK, CHUNK)
        sl = pl.ds(c0, CHUNK)
        cum_all = _dot_sel_left(tri, lf_s[sl, :], 3)
        q_all = q_s[sl, :]
        k_all = k_s[sl, :]
        v_all = v_s[sl, :]
        for hd in range(HGRN_HEADS):
            ls = slice(LANES * hd, LANES * (hd + 1))
            cum = cum_all[:, ls]
            q = q_all[:, ls]
            k = k_all[:, ls]
            v = v_all[:, ls]
            vb = _bf(v)
            last = cum[CHUNK - 1:CHUNK, :]
            starts = [None] + [cum[SUB * i - 1:SUB * i, :] for i in range(1, nsub)]
            cs = jnp.zeros_like(cum)
            for i in range(1, nsub):
                cs = jnp.where(rowblk == i, starts[i], cs)
            q_sub = q * jnp.exp(cum - cs)
            rows = [jnp.zeros((SUB, CHUNK), F32)]
            for i in range(1, nsub):
                k_i = k * jnp.exp(jnp.minimum(starts[i] - cum, 0.0))
                rows.append(_dot_nt(q_sub[SUB * i:SUB * (i + 1), :], k_i))
            a_off = jnp.where(off_mask, jnp.concatenate(rows, axis=0), 0.0)
            state = st_ref[hd]
            o = _dot(a_off, vb) + _dot_nt(q * jnp.exp(cum), state)
            diag = []
            for j in range(nsub):
                rs = slice(SUB * j, SUB * (j + 1))
                cum_b = cum[rs, :]
                q_b = q[rs, :]
                k_b = k[rs, :]
                pieces = []
                for t in range(SUB):
                    expo = jnp.where(s_idx <= t, cum_b[t:t + 1, :] - cum_b, -1e30)
                    pieces.append(jnp.exp(expo) * k_b * q_b[t:t + 1, :])
                pmat = jnp.concatenate(pieces, axis=0)
                score = _dot(pmat, ones_sq)
                wv = score * jnp.concatenate([v[rs, :]] * SUB, axis=0)
                diag.append(_dot(sel, wv))
            o = o + jnp.concatenate(diag, axis=0)
            o_s[sl, ls] = o
            k_end = k * jnp.exp(last - cum)
            st_ref[hd] = state * jnp.exp(last) + _dot_tn(vb, k_end)
        return carry

    sub_sums = _dot_sel_left(bsel_ref[...], log_f, 3)
    fast_ok = jnp.min(sub_sums) >= -HGRN_FAST_LOG_RANGE

    @pl.when(fast_ok)
    def _():
        lax.fori_loop(0, ts // CHUNK, fast_chunk_body, 0)

    @pl.when(jnp.logical_not(fast_ok))
    def _():
        lax.fori_loop(0, ts // CHUNK, chunk_body, 0)

    gate = _sigmoid(_dot(h_s[...], w_ref[:, 3 * W:]))
    for hd in range(HGRN_HEADS):
        ls = slice(LANES * hd, LANES * (hd + 1))
        o = o_s[:, ls]
        o = o * lax.rsqrt(jnp.mean(o * o, axis=-1, keepdims=True) + NORM_EPS) * gn_w[:, ls]
        o_ref[:, ls] = (o * gate[:, ls]).astype(o_ref.dtype)


def _ssm_kernel(x_ref, gain_ref, w_ref, cw_ref, cb_ref, hv_ref, wv_ref, tri_ref, exp_ref,
                o_ref, xb_ref, st_ref, h_s, xs_s, ec_s, xd_s, b_s, c_s, y_s, *, ts):
    W = SSM_WIDTH
    gw = SSM_GROUP_WIDTH
    pairs = W // LANES

    @pl.when(pl.program_id(1) == 0)
    def _():
        xb_ref[0:SUBLANES, :] = jnp.zeros((SUBLANES, SSM_CONV_DIM), F32)
        st_ref[...] = jnp.zeros(st_ref.shape, F32)

    h = _bf(_rmsnorm(x_ref[...], gain_ref[...]))
    h_s[...] = h

    dt_bias = hv_ref[0:1, :]
    neg_a = hv_ref[1:2, :]
    dt = _softplus(_dot(h, w_ref[:, W + SSM_CONV_DIM:]) + dt_bias)
    log_a = dt * neg_a
    tri = tri_ref[...]
    cum = jnp.concatenate([_dot_sel_left(tri, log_a[CHUNK * c:CHUNK * (c + 1), :], 3)
                           for c in range(ts // CHUNK)], axis=0)
    cum_parts = _split(cum, 3)
    dt_parts = _split(dt, 2)

    def project(lo, hi):
        xb_ref[SUBLANES:SUBLANES + ts, lo:hi] = _dot(h, w_ref[:, W + lo:W + hi])

    def conv_silu(lo, hi):
        full = xb_ref[0:ts + SUBLANES, lo:hi]
        conv = cb_ref[:, lo:hi] + full[SUBLANES:, :] * cw_ref[SSM_CONV_WIDTH - 1:SSM_CONV_WIDTH, lo:hi]
        for k in range(1, SSM_CONV_WIDTH):
            j = SSM_CONV_WIDTH - 1 - k
            conv = conv + pltpu.roll(full, k, 0)[SUBLANES:, :] * cw_ref[j:j + 1, lo:hi]
        xb_ref[0:SUBLANES, lo:hi] = xb_ref[ts:ts + SUBLANES, lo:hi]
        return _silu(conv)

    def finish(idx):
        lo, hi = slabs[idx]
        act = conv_silu(lo, hi)
        if idx < SSM_GROUPS:
            expand = exp_ref[:, lo:hi]
            xs_s[:, lo:hi] = act
            xd_s[:, lo:hi] = act * sum(jnp.dot(p, expand, preferred_element_type=F32) for p in dt_parts)
            ec_s[:, lo:hi] = sum(jnp.dot(p, expand, preferred_element_type=F32) for p in cum_parts)
        elif idx == SSM_GROUPS:
            b_s[...] = act
        else:
            c_s[...] = act

    slabs = [(gw * g, gw * (g + 1)) for g in range(SSM_GROUPS)] + [(W, W + SSM_BC), (W + SSM_BC, W + 2 * SSM_BC)]
    project(*slabs[0])
    for idx in range(len(slabs)):
        if idx + 1 < len(slabs):
            project(*slabs[idx + 1])
        finish(idx)

    t_idx = _iota((CHUNK, W), 0)
    s_idx = _iota((CHUNK, W), 1) & (CHUNK - 1)
    causal = s_idx <= t_idx
    on_diag = s_idx == t_idx
    first_head = (_iota((CHUNK, LANES), 1) < SSM_HEAD_DIM)

    def chunk_body(c, carry):
        c0 = pl.multiple_of(c * CHUNK, CHUNK)
        sl = pl.ds(c0, CHUNK)
        ec = ec_s[sl, :]
        xd = xd_s[sl, :]
        bm = b_s[sl, :]
        cm = c_s[sl, :]
        last = ec[CHUNK - 1:CHUNK, :]
        by_src = jnp.sum(jnp.where(on_diag, ec, 0.0), axis=0, keepdims=True)
        decay = jnp.exp(jnp.where(causal, ec - by_src, -1e30))
        xw = xd * jnp.exp(last - ec)
        e_in = jnp.exp(ec)
        e_last = jnp.exp(last)
        groups = range(SSM_GROUPS)
        glanes = [slice(gw * g, gw * (g + 1)) for g in groups]
        c_g = [_bf(cm[:, SSM_STATE * g:SSM_STATE * (g + 1)]) for g in groups]
        b_g = [_bf(bm[:, SSM_STATE * g:SSM_STATE * (g + 1)]) for g in groups]
        cb2 = [_dot_nt(c, jnp.concatenate([b, b], axis=0)) for c, b in zip(c_g, b_g)]
        state = [st_ref[g] for g in groups]
        y_in = [_dot(c, s) for c, s in zip(c_g, state)]
        ys = []
        for p in range(pairs):
            ls = slice(LANES * p, LANES * (p + 1))
            m = decay[:, ls] * cb2[(LANES * p) // gw]
            xp = xd[:, ls]
            x2 = jnp.concatenate([jnp.where(first_head, xp, 0.0), jnp.where(first_head, 0.0, xp)], axis=0)
            ys.append(_dot(m, x2))
        per_group = gw // LANES
        for g, gl in enumerate(glanes):
            y_s[sl, gl] = (y_in[g] * e_in[:, gl]
                           + jnp.concatenate(ys[per_group * g:per_group * (g + 1)], axis=1))
            st_ref[g] = state[g] * e_last[:, gl] + _dot_tn(b_g[g], xw[:, gl])
        return carry

    lax.fori_loop(0, ts // CHUNK, chunk_body, 0)

    d_skip = wv_ref[0:1, :]
    gn_w = wv_ref[1:2, :]
    h = h_s[...]
    for g in range(SSM_GROUPS):
        gl = slice(gw * g, gw * (g + 1))
        z = _dot(h, w_ref[:, gl])
        yg = (y_s[:, gl] + d_skip[:, gl] * xs_s[:, gl]) * _silu(z)
        yg = yg * lax.rsqrt(jnp.mean(yg * yg, axis=-1, keepdims=True) + NORM_EPS)
        o_ref[:, gl] = (yg * gn_w[:, gl]).astype(o_ref.dtype)


def _merge_kernel(x_ref, ya_ref, yb_ref, yc_ref, gmix_ref, wg_ref, wba_ref, wbb_ref, wbc_ref, wout_ref,
                  gffn_ref, wfi_ref, wfo_ref, gfin_ref, o_ref, *, final_norm):
    x = x_ref[...]
    h = _rmsnorm(x, gmix_ref[...])
    gates = _sigmoid(_dot(h, wg_ref[...]))
    merged = (gates[:, 0:D_MODEL] * jnp.dot(ya_ref[...], wba_ref[...], preferred_element_type=F32)
              + gates[:, D_MODEL:2 * D_MODEL] * jnp.dot(yb_ref[...], wbb_ref[...], preferred_element_type=F32)
              + gates[:, 2 * D_MODEL:] * jnp.dot(yc_ref[...], wbc_ref[...], preferred_element_type=F32))
    x = x + _dot(merged, wout_ref[...])
    h = _rmsnorm(x, gffn_ref[...])
    gu = _dot(h, wfi_ref[...])
    act = _silu(gu[:, 0:FFN_HIDDEN]) * gu[:, FFN_HIDDEN:]
    x = x + _dot(act, wfo_ref[...])
    if final_norm:
        x = _rmsnorm(x, gfin_ref[...])
    o_ref[...] = x


def _const_spec(arr):
    nd = arr.ndim
    return pl.BlockSpec(arr.shape, lambda b, s, _nd=nd: (0,) * _nd, pipeline_mode=pl.Buffered(1))


def _seq_spec(ts, width):
    return pl.BlockSpec((None, ts, width), lambda b, s: (b, s, 0))


def _params():
    return pltpu.CompilerParams(dimension_semantics=("arbitrary", "arbitrary"),
                                vmem_limit_bytes=VMEM_LIMIT_BYTES)


def _run_mixer(kernel_fn, x, consts, out_width, scratch, ts):
    bsz, seq, _ = x.shape
    return pl.pallas_call(
        functools.partial(kernel_fn, ts=ts),
        grid=(bsz, seq // ts),
        in_specs=[_seq_spec(ts, D_MODEL)] + [_const_spec(c) for c in consts],
        out_specs=_seq_spec(ts, out_width),
        out_shape=jax.ShapeDtypeStruct((bsz, seq, out_width), BF16),
        scratch_shapes=scratch,
        compiler_params=_params(),
    )(x, *consts)


def _row(v):
    return v.astype(F32).reshape(1, -1)


def _pad_cols(m, width):
    return jnp.pad(m, ((0, 0), (0, width - m.shape[1])))


def _rows8(rows, width):
    m = jnp.concatenate([_row(r) for r in rows], axis=0)
    return jnp.pad(m, ((0, SUBLANES - m.shape[0]), (0, width - m.shape[1])))


def _head_sum_matrix(width, head_dim):
    idx = np.arange(width) // head_dim
    return jnp.asarray(idx[:, None] == idx[None, :], dtype=BF16)


def _tri_ones(n):
    return jnp.asarray(np.tril(np.ones((n, n), np.float32)), dtype=BF16)


def _rwkv_branch(x, gain, w_in, mu, w0, w_up, a0, a_up, g_up, k_k, k_a, r_k, gn_w, gn_b, ts):
    W = RWKV_WIDTH
    dr, ir, gr = RWKV_DECAY_RANK, RWKV_ICL_RANK, RWKV_GATE_RANK
    pad = RWKV_GATE_PAD - gr
    w = _pad_cols(w_in, RWKV_U_COLS).astype(BF16)
    mu_p = _pad_cols(_row(mu), RWKV_U_COLS)
    wlr = jnp.zeros((LANES, 2 * W), F32).at[0:dr, 0:W].set(w_up).at[dr:dr + ir, W:].set(a_up).astype(BF16)
    blr = jnp.concatenate([_row(w0), _row(a0)], axis=1)
    gup = jnp.pad(g_up, ((0, pad), (0, 0))).astype(BF16)
    pv = _rows8([k_k, k_a, r_k.reshape(-1), gn_w, gn_b], W)
    consts = [_row(gain), w, mu_p, wlr, blr, gup, pv, _head_sum_matrix(MXU_TILE, RWKV_HEAD_DIM), _tri_ones(CHUNK)]
    scratch = ([pltpu.VMEM((ts + SUBLANES, RWKV_U_COLS), F32),
                pltpu.VMEM((W // LANES, LANES, LANES), F32)]
               + [pltpu.VMEM((ts, W), F32) for _ in range(8)])
    return _run_mixer(_rwkv_kernel, x, consts, W, scratch, ts)


def _hgrn_branch(x, gain, w_in, lb, gn_w, ts):
    W = HGRN_WIDTH
    lb = lb.astype(F32)
    pv = _rows8([jnp.log(lb), jnp.log1p(-lb), gn_w, 1.0 - lb], W)
    t_i = np.arange(SUB)[:, None]
    pair = np.arange(SUB * SUB)[None, :]
    sel = jnp.asarray((pair // SUB == t_i) & (pair % SUB <= t_i), dtype=BF16)
    bsel = jnp.asarray(np.arange(ts // SUB)[:, None] == (np.arange(ts) // SUB)[None, :], dtype=BF16)
    consts = [_row(gain), w_in.astype(BF16), pv, _tri_ones(CHUNK), jnp.ones((LANES, LANES), BF16), sel, bsel]
    scratch = ([pltpu.VMEM((HGRN_HEADS, HGRN_HEAD_DIM, HGRN_HEAD_DIM), F32), pltpu.VMEM((ts, D_MODEL), BF16)]
               + [pltpu.VMEM((ts, W), F32) for _ in range(5)])
    return _run_mixer(_hgrn_kernel, x, consts, W, scratch, ts)


def _ssm_branch(x, gain, w_in, conv_w, conv_b, dt_bias, a_log, d_skip, gn_w, ts):
    W = SSM_WIDTH
    w = _pad_cols(w_in, SSM_U_COLS).astype(BF16)
    cw = jnp.pad(conv_w.astype(F32).T, ((0, SUBLANES - SSM_CONV_WIDTH), (0, 0)))
    hv = _rows8([dt_bias, -jnp.exp(a_log.astype(F32))], LANES)
    wv = _rows8([jnp.repeat(d_skip.astype(F32), SSM_HEAD_DIM), gn_w], W)
    head_of_lane = np.arange(W) // SSM_HEAD_DIM
    expand = jnp.asarray(np.arange(LANES)[:, None] == head_of_lane[None, :], dtype=BF16)
    consts = [_row(gain), w, cw, _row(conv_b), hv, wv, _tri_ones(CHUNK), expand]
    scratch = [pltpu.VMEM((ts + SUBLANES, SSM_CONV_DIM), F32),
               pltpu.VMEM((SSM_GROUPS, SSM_STATE, SSM_GROUP_WIDTH), F32),
               pltpu.VMEM((ts, D_MODEL), BF16), pltpu.VMEM((ts, W), F32),
               pltpu.VMEM((ts, W), F32), pltpu.VMEM((ts, W), F32),
               pltpu.VMEM((ts, SSM_BC), F32), pltpu.VMEM((ts, SSM_BC), F32),
               pltpu.VMEM((ts, W), F32)]
    return _run_mixer(_ssm_kernel, x, consts, W, scratch, ts)


def _merge_ffn(x, ya, yb, yc, gain_mix, w_gate, w_branch, w_out, gain_ffn, w_ffn_in, w_ffn_out, gain_final,
               final_norm, ts):
    bsz, seq, _ = x.shape
    wb = w_branch.astype(BF16)
    consts = [_row(gain_mix), w_gate.astype(BF16), wb[:RWKV_WIDTH], wb[RWKV_WIDTH:RWKV_WIDTH + HGRN_WIDTH],
              wb[RWKV_WIDTH + HGRN_WIDTH:], w_out.astype(BF16), _row(gain_ffn), w_ffn_in.astype(BF16),
              w_ffn_out.astype(BF16), _row(gain_final)]
    seqs = [x, ya, yb, yc]
    return pl.pallas_call(
        functools.partial(_merge_kernel, final_norm=final_norm),
        grid=(bsz, seq // ts),
        in_specs=[_seq_spec(ts, a.shape[-1]) for a in seqs] + [_const_spec(c) for c in consts],
        out_specs=_seq_spec(ts, D_MODEL),
        out_shape=jax.ShapeDtypeStruct((bsz, seq, D_MODEL), F32),
        compiler_params=_params(),
    )(*seqs, *consts)


def kernel(x, norm_mix, w_in, rwkv_mu, rwkv_w0, rwkv_w_up, rwkv_a0, rwkv_a_up, rwkv_g_up, rwkv_k_k, rwkv_k_a,
           rwkv_r_k, rwkv_gn_w, rwkv_gn_b, hgrn_lb_logits, hgrn_gn_w, ssm_conv_w, ssm_conv_b, ssm_dt_bias,
           ssm_a_log, ssm_d, ssm_gn_w, w_branch, w_out, norm_ffn, w_ffn_in, w_ffn_out, norm_final):
    bsz, seq, d = x.shape
    assert d == D_MODEL
    depth = w_in.shape[0]
    ts = min(SEQ_BLOCK, seq)
    tm = min(MERGE_BLOCK, seq)
    assert seq % ts == 0 and seq % tm == 0 and ts % CHUNK == 0

    rwkv_cols = 3 * RWKV_WIDTH + RWKV_DECAY_RANK + RWKV_ICL_RANK + RWKV_GATE_RANK
    off_hgrn = rwkv_cols
    off_ssm = off_hgrn + 4 * HGRN_WIDTH
    off_gate = off_ssm + SSM_WIDTH + SSM_CONV_DIM + SSM_HEADS

    cs = jnp.cumsum(jax.nn.softmax(hgrn_lb_logits.astype(F32), axis=0), axis=0)
    lbs = cs - cs[:1]

    x = x.astype(F32)
    w_in_bf = w_in.astype(BF16)
    for l in range(depth):
        wi = w_in_bf[l]
        ya = _rwkv_branch(x, norm_mix[l], wi[:, :off_hgrn], rwkv_mu[l], rwkv_w0[l], rwkv_w_up[l], rwkv_a0[l],
                          rwkv_a_up[l], rwkv_g_up[l], rwkv_k_k[l], rwkv_k_a[l], rwkv_r_k[l], rwkv_gn_w[l],
                          rwkv_gn_b[l], ts)
        yb = _hgrn_branch(x, norm_mix[l], wi[:, off_hgrn:off_ssm], lbs[l], hgrn_gn_w[l], ts)
        yc = _ssm_branch(x, norm_mix[l], wi[:, off_ssm:off_gate], ssm_conv_w[l], ssm_conv_b[l], ssm_dt_bias[l],
                         ssm_a_log[l], ssm_d[l], ssm_gn_w[l], ts)
        x = _merge_ffn(x, ya, yb, yc, norm_mix[l], wi[:, off_gate:], w_branch[l], w_out[l], norm_ffn[l],
                       w_ffn_in[l], w_ffn_out[l], norm_final, l == depth - 1, tm)
    return x
```

```python
import functools

import numpy as np
import jax
import jax.numpy as jnp
from jax import lax
from jax.experimental import pallas as pl
from jax.experimental.pallas import tpu as pltpu

F32 = jnp.float32
BF16 = jnp.bfloat16

D_MODEL = 1024
CHUNK = 64
LOOP_CHUNKS = 2
SUB = 16
HGRN_FAST_LOG_RANGE = 60.0
NORM_EPS = 1e-5
LANES = 128
SUBLANES = 8
MXU_TILE = 256
VMEM_LIMIT_BYTES = 56 * 1024 * 1024

RWKV_HEAD_DIM = 64
RWKV_WIDTH = D_MODEL
RWKV_DECAY_RANK = 64
RWKV_ICL_RANK = 64
RWKV_GATE_RANK = 160
RWKV_GATE_PAD = 256
RWKV_GN_EPS = 64e-5
RWKV_LOOP_CHUNKS = 4
RWKV_GROUP_LANES = MXU_TILE
RWKV_U_COLS = 3 * RWKV_WIDTH + LANES + RWKV_GATE_PAD

HGRN_HEAD_DIM = 128
HGRN_WIDTH = D_MODEL
HGRN_HEADS = HGRN_WIDTH // HGRN_HEAD_DIM

SSM_WIDTH = 2 * D_MODEL
SSM_HEAD_DIM = 64
SSM_HEADS = SSM_WIDTH // SSM_HEAD_DIM
SSM_GROUPS = 4
SSM_STATE = 128
SSM_CONV_WIDTH = 4
SSM_BC = SSM_GROUPS * SSM_STATE
SSM_CONV_DIM = SSM_WIDTH + 2 * SSM_BC
SSM_GROUP_WIDTH = SSM_WIDTH // SSM_GROUPS
SSM_U_COLS = SSM_WIDTH + SSM_CONV_DIM + LANES

FFN_HIDDEN = ((8 * D_MODEL + 3 * 256 - 1) // (3 * 256)) * 256

SEQ_BLOCK = 512
MERGE_BLOCK = 256


def _bf(x):
    return x if x.dtype == BF16 else x.astype(BF16)


def _dot(a, b):
    return jnp.dot(_bf(a), _bf(b), preferred_element_type=F32)


def _dot_nt(a, b):
    return lax.dot_general(_bf(a), _bf(b), (((1,), (1,)), ((), ())), preferred_element_type=F32)


def _dot_tn(a, b):
    return lax.dot_general(_bf(a), _bf(b), (((0,), (0,)), ((), ())), preferred_element_type=F32)


def _split(x, n):
    parts = []
    rest = x
    for i in range(n):
        p = rest.astype(BF16)
        parts.append(p)
        if i + 1 < n:
            rest = rest - p.astype(F32)
    return parts


def _dot_sel_left(sel, x, passes=3):
    out = None
    for p in _split(x, passes):
        t = jnp.dot(sel, p, preferred_element_type=F32)
        out = t if out is None else out + t
    return out


def _dot_sel_right(x, sel, passes=3):
    out = None
    for p in _split(x, passes):
        t = jnp.dot(p, sel, preferred_element_type=F32)
        out = t if out is None else out + t
    return out


def _rmsnorm(x, gain):
    ms = jnp.mean(x * x, axis=-1, keepdims=True)
    return x * lax.rsqrt(ms + NORM_EPS) * gain


def _sigmoid(x):
    return 1.0 / (1.0 + jnp.exp(-x))


def _silu(x):
    return x * _sigmoid(x)


def _softplus(x):
    return jnp.maximum(x, 0.0) + jnp.log(1.0 + jnp.exp(-jnp.abs(x)))


def _iota(shape, axis):
    return lax.broadcasted_iota(jnp.int32, shape, axis)


def _interleave(*programs):
    live = list(programs)
    while live:
        for gen in list(live):
            try:
                next(gen)
            except StopIteration:
                live.remove(gen)


def _rwkv_kernel(x_ref, gain_ref, w_ref, mu_ref, wlr_ref, blr_ref, gup_ref, pv_ref, hsum_ref, tri_ref,
                 o_ref, ush_ref, st_ref, r_s, lw_s, k_s, v_s, ka_s, kb_s, o_s, g_s, *, ts):
    W = RWKV_WIDTH

    @pl.when(pl.program_id(1) == 0)
    def _():
        ush_ref[0:SUBLANES, :] = jnp.zeros((SUBLANES, RWKV_U_COLS), F32)
        st_ref[...] = jnp.zeros(st_ref.shape, F32)

    h = _bf(_rmsnorm(x_ref[...], gain_ref[...]))

    def project(lo, hi):
        ush_ref[SUBLANES:SUBLANES + ts, lo:hi] = _dot(h, w_ref[:, lo:hi])

    def token_shift(lo, hi):
        full = ush_ref[0:ts + SUBLANES, lo:hi]
        u = full[SUBLANES:, :]
        prev = pltpu.roll(full, 1, 0)[SUBLANES:, :]
        ush_ref[0:SUBLANES, lo:hi] = u[ts - SUBLANES:ts, :]
        return u + (prev - u) * mu_ref[:, lo:hi]

    project(3 * W, RWKV_U_COLS)
    project(W, 2 * W)
    low_rank = token_shift(3 * W, RWKV_U_COLS)
    xwa = low_rank[:, 0:LANES]
    xg = low_rank[:, LANES:]
    lane = _iota(xwa.shape, 1)
    lr_in = jnp.where(lane < RWKV_DECAY_RANK, jnp.tanh(xwa), xwa)
    lr = _dot(lr_in, wlr_ref[...]) + blr_ref[...]
    log_w = -jnp.exp(-_softplus(-lr[:, 0:W]) - 0.5)
    a = _sigmoid(lr[:, W:])
    g_s[...] = _dot(_sigmoid(xg), gup_ref[...])
    project(0, W)
    k = token_shift(W, 2 * W)

    k_k = pv_ref[0:1, :]
    k_a = pv_ref[1:2, :]
    r_k = pv_ref[2:3, :]
    gn_w = pv_ref[3:4, :]
    gn_b = pv_ref[4:5, :]
    hsum = hsum_ref[...]

    def head_sum(t):
        return jnp.concatenate([_dot(t[:, MXU_TILE * j:MXU_TILE * (j + 1)], hsum)
                                for j in range(W // MXU_TILE)], axis=1)

    kk = k * k_k
    ss = head_sum(kk * kk)
    kk = kk * lax.rsqrt(jnp.maximum(ss, 1e-24))
    kmod = k * (1.0 + (a - 1.0) * k_a)

    lw_s[...] = log_w
    k_s[...] = kmod
    ka_s[...] = -kk
    kb_s[...] = kk * a
    project(2 * W, 3 * W)
    r_s[...] = token_shift(0, W)
    v_s[...] = token_shift(2 * W, 3 * W)

    tri = tri_ref[...]
    gl = RWKV_GROUP_LANES
    n_groups = W // gl
    heads_per_group = gl // RWKV_HEAD_DIM
    row_w = _iota((CHUNK, gl), 0)
    col_w = _iota((CHUNK, gl), 1) & (RWKV_HEAD_DIM - 1)
    strict = row_w > col_w
    incl = row_w >= col_w
    eye_w = jnp.where(row_w == col_w, 1.0, 0.0)
    first_head = _iota((CHUNK, LANES), 1) < RWKV_HEAD_DIM
    head_shift = int(np.log2(RWKV_HEAD_DIM))
    same_head = (_iota((gl, gl), 0) >> head_shift) == (_iota((gl, gl), 1) >> head_shift)
    zeros_tile = jnp.zeros((CHUNK, LANES), F32)

    def blockdiag(t):
        blocks = []
        for hd in range(heads_per_group):
            tile = t[:, LANES * (hd // 2):LANES * (hd // 2 + 1)]
            kept = jnp.where(first_head, tile, 0.0) if hd % 2 == 0 else jnp.where(first_head, 0.0, tile)
            tiles = [kept if j == hd // 2 else zeros_tile for j in range(gl // LANES)]
            blocks.append(jnp.concatenate(tiles, axis=1))
        return jnp.concatenate(blocks, axis=0)

    lanes = [slice(gl * g, gl * (g + 1)) for g in range(n_groups)]

    def state_free(c, ctx):
        c0 = pl.multiple_of(c * CHUNK, CHUNK)
        sl = pl.ds(c0, CHUNK)
        lw = lw_s[sl, :]
        cum = _dot_sel_left(tri, lw, 2)
        e_last = jnp.exp(cum[CHUNK - 1:CHUNK, :])
        e_neg = jnp.exp(-cum)
        rt = r_s[sl, :] * jnp.exp(cum)
        at = ka_s[sl, :] * jnp.exp(cum - lw)
        bt = kb_s[sl, :] * e_neg
        kt = k_s[sl, :] * e_neg
        vv = v_s[sl, :]
        ar = [jnp.concatenate([at[:, ls], rt[:, ls]], axis=0) for ls in lanes]
        bk_bd = [jnp.concatenate([blockdiag(bt[:, ls]), blockdiag(kt[:, ls])], axis=0) for ls in lanes]
        v_bd = [blockdiag(vv[:, ls]) for ls in lanes]
        bk_end = [jnp.concatenate([bt[:, ls], kt[:, ls]], axis=0) * e_last[:, ls] for ls in lanes]
        yield
        gram = [_dot_nt(a, b) for a, b in zip(ar, bk_bd)]
        n_ab = [jnp.where(strict, g[0:CHUNK, 0:gl], 0.0) for g in gram]
        n_ak = [jnp.where(strict, g[0:CHUNK, gl:], 0.0) for g in gram]
        n_rbk = [jnp.concatenate([jnp.where(incl, g[CHUNK:, 0:gl], 0.0), jnp.where(incl, g[CHUNK:, gl:], 0.0)], axis=1)
                 for g in gram]
        yield
        ak_v = [_dot(n, v) for n, v in zip(n_ak, v_bd)]
        prod = [eye_w + n for n in n_ab]
        npow = [_dot(n, blockdiag(n)) for n in n_ab]
        for _ in range(int(np.log2(CHUNK)) - 2):
            yield
            both = [_dot(jnp.concatenate([nk, pr], axis=0), blockdiag(nk)) for nk, pr in zip(npow, prod)]
            prod = [pr + b[CHUNK:] for pr, b in zip(prod, both)]
            npow = [b[0:CHUNK] for b in both]
        yield
        inv = [pr + _dot(pr, blockdiag(nk)) for pr, nk in zip(prod, npow)]
        ctx.update(sl=sl, e_last=e_last, ar=ar, vv=vv, v_bd=v_bd, bk_end=bk_end, ak_v=ak_v, n_rbk=n_rbk, inv=inv)

    def state_bound(ctx, states):
        sl, e_last = ctx["sl"], ctx["e_last"]
        from_state = [_dot_nt(a, s) for a, s in zip(ctx["ar"], states)]
        yield
        u = [_dot(t, blockdiag(f[0:CHUNK] + r)) for t, f, r in zip(ctx["inv"], from_state, ctx["ak_v"])]
        yield
        for g, ls in enumerate(lanes):
            uv = jnp.concatenate([u[g], ctx["vv"][:, ls]], axis=0)
            states[g] = states[g] * e_last[:, ls] + jnp.where(same_head, _dot_tn(uv, ctx["bk_end"][g]), 0.0)
        yield
        for g, ls in enumerate(lanes):
            o_s[sl, ls] = from_state[g][CHUNK:] + _dot(ctx["n_rbk"][g],
                                                       jnp.concatenate([blockdiag(u[g]), ctx["v_bd"][g]], axis=0))

    def chained(*programs):
        for program in programs:
            yield from program

    def chunk_body(i, carry):
        n = RWKV_LOOP_CHUNKS
        states = [st_ref[g] for g in range(n_groups)]
        ctxs = [dict() for _ in range(n)]
        pending = []
        for j in range(0, n, 2):
            frees = [state_free(n * i + j + d, ctxs[j + d]) for d in range(2)]
            _interleave(*frees, *([chained(*pending)] if pending else []))
            pending = [state_bound(ctxs[j + d], states) for d in range(2)]
        _interleave(chained(*pending))
        for g in range(n_groups):
            st_ref[g] = states[g]
        return carry

    lax.fori_loop(0, ts // (CHUNK * RWKV_LOOP_CHUNKS), chunk_body, 0)

    o = o_s[...]
    inv_n = 1.0 / RWKV_HEAD_DIM
    mean = head_sum(o) * inv_n
    cen = o - mean
    var = head_sum(cen * cen) * inv_n
    o = cen * lax.rsqrt(var + RWKV_GN_EPS) * gn_w + gn_b
    bonus = head_sum(r_s[...] * k_s[...] * r_k)
    o = o + bonus * v_s[...]
    o_ref[...] = (o * g_s[...]).astype(o_ref.dtype)


def _hgrn_kernel(x_ref, gain_ref, w_ref, pv_ref, tri_ref, ones_ref, sel_ref, bsel_ref,
                 o_ref, st_ref, h_s, q_s, k_s, v_s, lf_s, o_s, *, ts):
    W = HGRN_WIDTH
    nsub = CHUNK // SUB

    @pl.when(pl.program_id(1) == 0)
    def _():
        st_ref[...] = jnp.zeros(st_ref.shape, F32)

    h = _bf(_rmsnorm(x_ref[...], gain_ref[...]))
    h_s[...] = h
    log_lb = pv_ref[0:1, :]
    log1m_lb = pv_ref[1:2, :]
    gn_w = pv_ref[2:3, :]
    f_pre = _dot(h, w_ref[:, W:2 * W])
    q_pre = _dot(h, w_ref[:, 0:W])
    e = jnp.exp(-jnp.abs(f_pre))
    b = log1m_lb - (jnp.maximum(-f_pre, 0.0) + jnp.log(1.0 + e))
    mx = jnp.maximum(log_lb, b)
    log_f = mx + jnp.log(1.0 + jnp.exp(-jnp.abs(log_lb - b)))
    lf_s[...] = log_f
    k_s[...] = pv_ref[3:4, :] * jnp.where(f_pre > 0.0, e, 1.0) / (1.0 + e)
    v_s[...] = _dot(h, w_ref[:, 2 * W:3 * W])
    q_s[...] = _silu(q_pre)

    tri = tri_ref[...]
    ones_sq = ones_ref[...]
    sel = sel_ref[...]
    sub_shift = int(np.log2(SUB))
    rblk = _iota((CHUNK, CHUNK), 0) >> sub_shift
    cblk = _iota((CHUNK, CHUNK), 1) >> sub_shift
    off_mask = cblk < rblk
    rowblk = _iota((CHUNK, LANES), 0) >> sub_shift
    s_idx = _iota((SUB, LANES), 0)

    heads = range(HGRN_HEADS)
    lanes = [slice(LANES * hd, LANES * (hd + 1)) for hd in heads]
    causal = _iota((CHUNK, CHUNK), 1) <= _iota((CHUNK, CHUNK), 0)

    def fast_chunk(c, states):
        c0 = pl.multiple_of(c * CHUNK, CHUNK)
        sl = pl.ds(c0, CHUNK)
        cum = _dot_sel_left(tri, lf_s[sl, :], 3)
        q = q_s[sl, :]
        k = k_s[sl, :]
        v = v_s[sl, :]
        yield
        starts = [None] + [cum[SUB * i - 1:SUB * i, :] for i in range(1, nsub)]
        cs = jnp.zeros_like(cum)
        rowblk_w = _iota(cum.shape, 0) >> sub_shift
        for i in range(1, nsub):
            cs = jnp.where(rowblk_w == i, starts[i], cs)
        q_sub = q * jnp.exp(cum - cs)
        k_own = k * jnp.exp(cs - cum)
        k_prev = [None] + [k[0:SUB * i, :] * jnp.exp(jnp.minimum(starts[i] - cum[0:SUB * i, :], 0.0))
                           for i in range(1, nsub)]
        yield
        last = cum[CHUNK - 1:CHUNK, :]
        q_in = q * jnp.exp(cum)
        k_end = k * jnp.exp(last - cum)
        e_last = jnp.exp(last)
        before = list(states)
        for hd, ls in enumerate(lanes):
            states[hd] = before[hd] * e_last[:, ls] + _dot_tn(v[:, ls], k_end[:, ls])
        yield
        scores = []
        for n, ls in enumerate(lanes):
            rows = []
            for i in range(nsub):
                parts = [k_own[SUB * i:SUB * (i + 1), ls]]
                if i > 0:
                    parts = [k_prev[i][:, ls]] + parts
                if i + 1 < nsub:
                    parts.append(jnp.zeros((CHUNK - SUB * (i + 1), LANES), F32))
                rows.append(_dot_nt(q_sub[SUB * i:SUB * (i + 1), ls], jnp.concatenate(parts, axis=0)))
            scores.append(jnp.where(causal, jnp.concatenate(rows, axis=0), 0.0))
            if n % 4 == 3:
                yield
        outs = [_dot(a, v[:, ls]) + _dot_nt(q_in[:, ls], s) for a, ls, s in zip(scores, lanes, before)]
        yield
        for hd, ls in enumerate(lanes):
            o_s[sl, ls] = outs[hd]

    def fast_chunk_body(i, carry):
        states = [st_ref[hd] for hd in heads]
        _interleave(*[fast_chunk(LOOP_CHUNKS * i + j, states) for j in range(LOOP_CHUNKS)])
        for hd in heads:
            st_ref[hd] = states[hd]
        return carry

    def chunk_body(c, carry):
        c0 = pl.multiple_of(c * CHUNK, CHUNK)
        sl = pl.ds(c0, CHUNK)
        cum_all = _dot_sel_left(tri, lf_s[sl, :], 3)
        q_all = q_s[sl, :]
        k_all = k_s[sl, :]
        v_all = v_s[sl, :]
        for hd in range(HGRN_HEADS):
            ls = slice(LANES * hd, LANES * (hd + 1))
            cum = cum_all[:, ls]
            q = q_all[:, ls]
            k = k_all[:, ls]
            v = v_all[:, ls]
            vb = _bf(v)
            last = cum[CHUNK - 1:CHUNK, :]
            starts = [None] + [cum[SUB * i - 1:SUB * i, :] for i in range(1, nsub)]
            cs = jnp.zeros_like(cum)
            for i in range(1, nsub):
                cs = jnp.where(rowblk == i, starts[i], cs)
            q_sub = q * jnp.exp(cum - cs)
            rows = [jnp.zeros((SUB, CHUNK), F32)]
            for i in range(1, nsub):
                k_i = k * jnp.exp(jnp.minimum(starts[i] - cum, 0.0))
                rows.append(_dot_nt(q_sub[SUB * i:SUB * (i + 1), :], k_i))
            a_off = jnp.where(off_mask, jnp.concatenate(rows, axis=0), 0.0)
            state = st_ref[hd]
            o = _dot(a_off, vb) + _dot_nt(q * jnp.exp(cum), state)
            diag = []
            for j in range(nsub):
                rs = slice(SUB * j, SUB * (j + 1))
                cum_b = cum[rs, :]
                q_b = q[rs, :]
                k_b = k[rs, :]
                pieces = []
                for t in range(SUB):
                    expo = jnp.where(s_idx <= t, cum_b[t:t + 1, :] - cum_b, -1e30)
                    pieces.append(jnp.exp(expo) * k_b * q_b[t:t + 1, :])
                pmat = jnp.concatenate(pieces, axis=0)
                score = _dot(pmat, ones_sq)
                wv = score * jnp.concatenate([v[rs, :]] * SUB, axis=0)
                diag.append(_dot(sel, wv))
            o = o + jnp.concatenate(diag, axis=0)
            o_s[sl, ls] = o
            k_end = k * jnp.exp(last - cum)
            st_ref[hd] = state * jnp.exp(last) + _dot_tn(vb, k_end)
        return carry

    sub_sums = _dot_sel_left(bsel_ref[...], log_f, 3)
    fast_ok = jnp.min(sub_sums) >= -HGRN_FAST_LOG_RANGE

    @pl.when(fast_ok)
    def _():
        lax.fori_loop(0, ts // (CHUNK * LOOP_CHUNKS), fast_chunk_body, 0)

    @pl.when(jnp.logical_not(fast_ok))
    def _():
        lax.fori_loop(0, ts // CHUNK, chunk_body, 0)

    gate = _sigmoid(_dot(h_s[...], w_ref[:, 3 * W:]))
    for hd in range(HGRN_HEADS):
        ls = slice(LANES * hd, LANES * (hd + 1))
        o = o_s[:, ls]
        o = o * lax.rsqrt(jnp.mean(o * o, axis=-1, keepdims=True) + NORM_EPS) * gn_w[:, ls]
        o_ref[:, ls] = (o * gate[:, ls]).astype(o_ref.dtype)


def _ssm_kernel(x_ref, gain_ref, w_ref, cw_ref, cb_ref, hv_ref, wv_ref, tri_ref, exp_ref,
                o_ref, xb_ref, st_ref, h_s, xs_s, ec_s, xd_s, b_s, c_s, y_s, *, ts):
    W = SSM_WIDTH
    gw = SSM_GROUP_WIDTH
    pairs = W // LANES

    @pl.when(pl.program_id(1) == 0)
    def _():
        xb_ref[0:SUBLANES, :] = jnp.zeros((SUBLANES, SSM_CONV_DIM), F32)
        st_ref[...] = jnp.zeros(st_ref.shape, F32)

    h = _bf(_rmsnorm(x_ref[...], gain_ref[...]))
    h_s[...] = h

    dt_bias = hv_ref[0:1, :]
    neg_a = hv_ref[1:2, :]
    dt = _softplus(_dot(h, w_ref[:, W + SSM_CONV_DIM:]) + dt_bias)
    log_a = dt * neg_a
    tri = tri_ref[...]
    cum = jnp.concatenate([_dot_sel_left(tri, log_a[CHUNK * c:CHUNK * (c + 1), :], 3)
                           for c in range(ts // CHUNK)], axis=0)
    cum_parts = _split(cum, 3)
    dt_parts = _split(dt, 2)

    def project(lo, hi):
        xb_ref[SUBLANES:SUBLANES + ts, lo:hi] = _dot(h, w_ref[:, W + lo:W + hi])

    def conv_silu(lo, hi):
        full = xb_ref[0:ts + SUBLANES, lo:hi]
        conv = cb_ref[:, lo:hi] + full[SUBLANES:, :] * cw_ref[SSM_CONV_WIDTH - 1:SSM_CONV_WIDTH, lo:hi]
        for k in range(1, SSM_CONV_WIDTH):
            j = SSM_CONV_WIDTH - 1 - k
            conv = conv + pltpu.roll(full, k, 0)[SUBLANES:, :] * cw_ref[j:j + 1, lo:hi]
        xb_ref[0:SUBLANES, lo:hi] = xb_ref[ts:ts + SUBLANES, lo:hi]
        return _silu(conv)

    def finish(idx):
        lo, hi = slabs[idx]
        act = conv_silu(lo, hi)
        if idx < SSM_GROUPS:
            expand = exp_ref[:, lo:hi]
            xs_s[:, lo:hi] = act
            xd_s[:, lo:hi] = act * sum(jnp.dot(p, expand, preferred_element_type=F32) for p in dt_parts)
            ec_s[:, lo:hi] = sum(jnp.dot(p, expand, preferred_element_type=F32) for p in cum_parts)
        elif idx == SSM_GROUPS:
            b_s[...] = act
        else:
            c_s[...] = act

    slabs = [(gw * g, gw * (g + 1)) for g in range(SSM_GROUPS)] + [(W, W + SSM_BC), (W + SSM_BC, W + 2 * SSM_BC)]
    project(*slabs[0])
    for idx in range(len(slabs)):
        if idx + 1 < len(slabs):
            project(*slabs[idx + 1])
        finish(idx)

    t_idx = _iota((CHUNK, W), 0)
    s_idx = _iota((CHUNK, W), 1) & (CHUNK - 1)
    causal = s_idx <= t_idx
    on_diag = s_idx == t_idx
    first_head = (_iota((CHUNK, LANES), 1) < SSM_HEAD_DIM)

    groups = range(SSM_GROUPS)
    glanes = [slice(gw * g, gw * (g + 1)) for g in groups]
    per_group = gw // LANES

    def chunk(c, states):
        c0 = pl.multiple_of(c * CHUNK, CHUNK)
        sl = pl.ds(c0, CHUNK)
        ec = ec_s[sl, :]
        xd = xd_s[sl, :]
        bm = b_s[sl, :]
        cm = c_s[sl, :]
        last = ec[CHUNK - 1:CHUNK, :]
        by_src = jnp.sum(jnp.where(on_diag, ec, 0.0), axis=0, keepdims=True)
        decay = jnp.exp(jnp.where(causal, ec - by_src, -1e30))
        yield
        xw = xd * jnp.exp(last - ec)
        e_in = jnp.exp(ec)
        e_last = jnp.exp(last)
        c_g = [_bf(cm[:, SSM_STATE * g:SSM_STATE * (g + 1)]) for g in groups]
        b_g = [_bf(bm[:, SSM_STATE * g:SSM_STATE * (g + 1)]) for g in groups]
        cb2 = [_dot_nt(c, jnp.concatenate([b, b], axis=0)) for c, b in zip(c_g, b_g)]
        yield
        before = list(states)
        for g, gl in enumerate(glanes):
            states[g] = before[g] * e_last[:, gl] + _dot_tn(b_g[g], xw[:, gl])
        y_in = [_dot(c, s) for c, s in zip(c_g, before)]
        yield
        ys = []
        for p in range(pairs):
            ls = slice(LANES * p, LANES * (p + 1))
            m = decay[:, ls] * cb2[(LANES * p) // gw]
            xp = xd[:, ls]
            x2 = jnp.concatenate([jnp.where(first_head, xp, 0.0), jnp.where(first_head, 0.0, xp)], axis=0)
            ys.append(_dot(m, x2))
            if p % 4 == 3:
                yield
        for g, gl in enumerate(glanes):
            y_s[sl, gl] = (y_in[g] * e_in[:, gl]
                           + jnp.concatenate(ys[per_group * g:per_group * (g + 1)], axis=1))

    def chunk_body(i, carry):
        states = [st_ref[g] for g in groups]
        _interleave(*[chunk(LOOP_CHUNKS * i + j, states) for j in range(LOOP_CHUNKS)])
        for g in groups:
            st_ref[g] = states[g]
        return carry

    lax.fori_loop(0, ts // (CHUNK * LOOP_CHUNKS), chunk_body, 0)

    d_skip = wv_ref[0:1, :]
    gn_w = wv_ref[1:2, :]
    h = h_s[...]
    for g in range(SSM_GROUPS):
        gl = slice(gw * g, gw * (g + 1))
        z = _dot(h, w_ref[:, gl])
        yg = (y_s[:, gl] + d_skip[:, gl] * xs_s[:, gl]) * _silu(z)
        yg = yg * lax.rsqrt(jnp.mean(yg * yg, axis=-1, keepdims=True) + NORM_EPS)
        o_ref[:, gl] = (yg * gn_w[:, gl]).astype(o_ref.dtype)


def _merge_kernel(x_ref, ya_ref, yb_ref, yc_ref, gmix_ref, wg_ref, wba_ref, wbb_ref, wbc_ref, wout_ref,
                  gffn_ref, wfi_ref, wfo_ref, gfin_ref, o_ref, *, final_norm):
    x = x_ref[...]
    h = _rmsnorm(x, gmix_ref[...])
    gates = _sigmoid(_dot(h, wg_ref[...]))
    merged = (gates[:, 0:D_MODEL] * jnp.dot(ya_ref[...], wba_ref[...], preferred_element_type=F32)
              + gates[:, D_MODEL:2 * D_MODEL] * jnp.dot(yb_ref[...], wbb_ref[...], preferred_element_type=F32)
              + gates[:, 2 * D_MODEL:] * jnp.dot(yc_ref[...], wbc_ref[...], preferred_element_type=F32))
    x = x + _dot(merged, wout_ref[...])
    h = _rmsnorm(x, gffn_ref[...])
    gu = _dot(h, wfi_ref[...])
    act = _silu(gu[:, 0:FFN_HIDDEN]) * gu[:, FFN_HIDDEN:]
    x = x + _dot(act, wfo_ref[...])
    if final_norm:
        x = _rmsnorm(x, gfin_ref[...])
    o_ref[...] = x


def _const_spec(arr):
    nd = arr.ndim
    return pl.BlockSpec(arr.shape, lambda b, s, _nd=nd: (0,) * _nd, pipeline_mode=pl.Buffered(1))


def _seq_spec(ts, width):
    return pl.BlockSpec((None, ts, width), lambda b, s: (b, s, 0))


def _params():
    return pltpu.CompilerParams(dimension_semantics=("arbitrary", "arbitrary"),
                                vmem_limit_bytes=VMEM_LIMIT_BYTES)


def _run_mixer(kernel_fn, x, consts, out_width, scratch, ts):
    bsz, seq, _ = x.shape
    return pl.pallas_call(
        functools.partial(kernel_fn, ts=ts),
        grid=(bsz, seq // ts),
        in_specs=[_seq_spec(ts, D_MODEL)] + [_const_spec(c) for c in consts],
        out_specs=_seq_spec(ts, out_width),
        out_shape=jax.ShapeDtypeStruct((bsz, seq, out_width), BF16),
        scratch_shapes=scratch,
        compiler_params=_params(),
    )(x, *consts)


def _row(v):
    return v.astype(F32).reshape(1, -1)


def _pad_cols(m, width):
    return jnp.pad(m, ((0, 0), (0, width - m.shape[1])))


def _rows8(rows, width):
    m = jnp.concatenate([_row(r) for r in rows], axis=0)
    return jnp.pad(m, ((0, SUBLANES - m.shape[0]), (0, width - m.shape[1])))


def _head_sum_matrix(width, head_dim):
    idx = np.arange(width) // head_dim
    return jnp.asarray(idx[:, None] == idx[None, :], dtype=BF16)


def _tri_ones(n):
    return jnp.asarray(np.tril(np.ones((n, n), np.float32)), dtype=BF16)


def _rwkv_branch(x, gain, w_in, mu, w0, w_up, a0, a_up, g_up, k_k, k_a, r_k, gn_w, gn_b, ts):
    W = RWKV_WIDTH
    dr, ir, gr = RWKV_DECAY_RANK, RWKV_ICL_RANK, RWKV_GATE_RANK
    pad = RWKV_GATE_PAD - gr
    w = _pad_cols(w_in, RWKV_U_COLS).astype(BF16)
    mu_p = _pad_cols(_row(mu), RWKV_U_COLS)
    wlr = jnp.zeros((LANES, 2 * W), F32).at[0:dr, 0:W].set(w_up).at[dr:dr + ir, W:].set(a_up).astype(BF16)
    blr = jnp.concatenate([_row(w0), _row(a0)], axis=1)
    gup = jnp.pad(g_up, ((0, pad), (0, 0))).astype(BF16)
    pv = _rows8([k_k, k_a, r_k.reshape(-1), gn_w, gn_b], W)
    consts = [_row(gain), w, mu_p, wlr, blr, gup, pv, _head_sum_matrix(MXU_TILE, RWKV_HEAD_DIM), _tri_ones(CHUNK)]
    scratch = ([pltpu.VMEM((ts + SUBLANES, RWKV_U_COLS), F32),
                pltpu.VMEM((W // RWKV_GROUP_LANES, RWKV_GROUP_LANES, RWKV_GROUP_LANES), F32)]
               + [pltpu.VMEM((ts, W), F32) for _ in range(8)])
    return _run_mixer(_rwkv_kernel, x, consts, W, scratch, ts)


def _hgrn_branch(x, gain, w_in, lb, gn_w, ts):
    W = HGRN_WIDTH
    lb = lb.astype(F32)
    pv = _rows8([jnp.log(lb), jnp.log1p(-lb), gn_w, 1.0 - lb], W)
    t_i = np.arange(SUB)[:, None]
    pair = np.arange(SUB * SUB)[None, :]
    sel = jnp.asarray((pair // SUB == t_i) & (pair % SUB <= t_i), dtype=BF16)
    bsel = jnp.asarray(np.arange(ts // SUB)[:, None] == (np.arange(ts) // SUB)[None, :], dtype=BF16)
    consts = [_row(gain), w_in.astype(BF16), pv, _tri_ones(CHUNK), jnp.ones((LANES, LANES), BF16), sel, bsel]
    scratch = ([pltpu.VMEM((HGRN_HEADS, HGRN_HEAD_DIM, HGRN_HEAD_DIM), F32), pltpu.VMEM((ts, D_MODEL), BF16)]
               + [pltpu.VMEM((ts, W), F32) for _ in range(5)])
    return _run_mixer(_hgrn_kernel, x, consts, W, scratch, ts)


def _ssm_branch(x, gain, w_in, conv_w, conv_b, dt_bias, a_log, d_skip, gn_w, ts):
    W = SSM_WIDTH
    w = _pad_cols(w_in, SSM_U_COLS).astype(BF16)
    cw = jnp.pad(conv_w.astype(F32).T, ((0, SUBLANES - SSM_CONV_WIDTH), (0, 0)))
    hv = _rows8([dt_bias, -jnp.exp(a_log.astype(F32))], LANES)
    wv = _rows8([jnp.repeat(d_skip.astype(F32), SSM_HEAD_DIM), gn_w], W)
    head_of_lane = np.arange(W) // SSM_HEAD_DIM
    expand = jnp.asarray(np.arange(LANES)[:, None] == head_of_lane[None, :], dtype=BF16)
    consts = [_row(gain), w, cw, _row(conv_b), hv, wv, _tri_ones(CHUNK), expand]
    scratch = [pltpu.VMEM((ts + SUBLANES, SSM_CONV_DIM), F32),
               pltpu.VMEM((SSM_GROUPS, SSM_STATE, SSM_GROUP_WIDTH), F32),
               pltpu.VMEM((ts, D_MODEL), BF16), pltpu.VMEM((ts, W), F32),
               pltpu.VMEM((ts, W), F32), pltpu.VMEM((ts, W), F32),
               pltpu.VMEM((ts, SSM_BC), F32), pltpu.VMEM((ts, SSM_BC), F32),
               pltpu.VMEM((ts, W), F32)]
    return _run_mixer(_ssm_kernel, x, consts, W, scratch, ts)


def _merge_ffn(x, ya, yb, yc, gain_mix, w_gate, w_branch, w_out, gain_ffn, w_ffn_in, w_ffn_out, gain_final,
               final_norm, ts):
    bsz, seq, _ = x.shape
    wb = w_branch.astype(BF16)
    consts = [_row(gain_mix), w_gate.astype(BF16), wb[:RWKV_WIDTH], wb[RWKV_WIDTH:RWKV_WIDTH + HGRN_WIDTH],
              wb[RWKV_WIDTH + HGRN_WIDTH:], w_out.astype(BF16), _row(gain_ffn), w_ffn_in.astype(BF16),
              w_ffn_out.astype(BF16), _row(gain_final)]
    seqs = [x, ya, yb, yc]
    return pl.pallas_call(
        functools.partial(_merge_kernel, final_norm=final_norm),
        grid=(bsz, seq // ts),
        in_specs=[_seq_spec(ts, a.shape[-1]) for a in seqs] + [_const_spec(c) for c in consts],
        out_specs=_seq_spec(ts, D_MODEL),
        out_shape=jax.ShapeDtypeStruct((bsz, seq, D_MODEL), F32),
        compiler_params=_params(),
    )(*seqs, *consts)


def kernel(x, norm_mix, w_in, rwkv_mu, rwkv_w0, rwkv_w_up, rwkv_a0, rwkv_a_up, rwkv_g_up, rwkv_k_k, rwkv_k_a,
           rwkv_r_k, rwkv_gn_w, rwkv_gn_b, hgrn_lb_logits, hgrn_gn_w, ssm_conv_w, ssm_conv_b, ssm_dt_bias,
           ssm_a_log, ssm_d, ssm_gn_w, w_branch, w_out, norm_ffn, w_ffn_in, w_ffn_out, norm_final):
    bsz, seq, d = x.shape
    assert d == D_MODEL
    depth = w_in.shape[0]
    ts = min(SEQ_BLOCK, seq)
    tm = min(MERGE_BLOCK, seq)
    assert seq % ts == 0 and seq % tm == 0
    assert ts % (CHUNK * LOOP_CHUNKS) == 0 and ts % (CHUNK * RWKV_LOOP_CHUNKS) == 0

    rwkv_cols = 3 * RWKV_WIDTH + RWKV_DECAY_RANK + RWKV_ICL_RANK + RWKV_GATE_RANK
    off_hgrn = rwkv_cols
    off_ssm = off_hgrn + 4 * HGRN_WIDTH
    off_gate = off_ssm + SSM_WIDTH + SSM_CONV_DIM + SSM_HEADS

    cs = jnp.cumsum(jax.nn.softmax(hgrn_lb_logits.astype(F32), axis=0), axis=0)
    lbs = cs - cs[:1]

    x = x.astype(F32)
    w_in_bf = w_in.astype(BF16)
    for l in range(depth):
        wi = w_in_bf[l]
        ya = _rwkv_branch(x, norm_mix[l], wi[:, :off_hgrn], rwkv_mu[l], rwkv_w0[l], rwkv_w_up[l], rwkv_a0[l],
                          rwkv_a_up[l], rwkv_g_up[l], rwkv_k_k[l], rwkv_k_a[l], rwkv_r_k[l], rwkv_gn_w[l],
                          rwkv_gn_b[l], ts)
        yb = _hgrn_branch(x, norm_mix[l], wi[:, off_hgrn:off_ssm], lbs[l], hgrn_gn_w[l], ts)
        yc = _ssm_branch(x, norm_mix[l], wi[:, off_ssm:off_gate], ssm_conv_w[l], ssm_conv_b[l], ssm_dt_bias[l],
                         ssm_a_log[l], ssm_d[l], ssm_gn_w[l], ts)
        x = _merge_ffn(x, ya, yb, yc, norm_mix[l], wi[:, off_gate:], w_branch[l], w_out[l], norm_ffn[l],
                       w_ffn_in[l], w_ffn_out[l], norm_final, l == depth - 1, tm)
    return x
```

```python
import functools

import numpy as np
import jax
import jax.numpy as jnp
from jax import lax
from jax.experimental import pallas as pl
from jax.experimental.pallas import tpu as pltpu

F32 = jnp.float32
BF16 = jnp.bfloat16

D_MODEL = 1024
CHUNK = 64
LOOP_CHUNKS = 2
SUB = 16
HGRN_FAST_LOG_RANGE = 60.0
NORM_EPS = 1e-5
LANES = 128
SUBLANES = 8
MXU_TILE = 256
VMEM_LIMIT_BYTES = 56 * 1024 * 1024

RWKV_HEAD_DIM = 64
RWKV_WIDTH = D_MODEL
RWKV_DECAY_RANK = 64
RWKV_ICL_RANK = 64
RWKV_GATE_RANK = 160
RWKV_GATE_PAD = 256
RWKV_GN_EPS = 64e-5
RWKV_LOOP_CHUNKS = 8
RWKV_GROUP_LANES = MXU_TILE
RWKV_U_COLS = 3 * RWKV_WIDTH + LANES + RWKV_GATE_PAD

HGRN_HEAD_DIM = 128
HGRN_WIDTH = D_MODEL
HGRN_HEADS = HGRN_WIDTH // HGRN_HEAD_DIM

SSM_WIDTH = 2 * D_MODEL
SSM_HEAD_DIM = 64
SSM_HEADS = SSM_WIDTH // SSM_HEAD_DIM
SSM_GROUPS = 4
SSM_STATE = 128
SSM_CONV_WIDTH = 4
SSM_BC = SSM_GROUPS * SSM_STATE
SSM_CONV_DIM = SSM_WIDTH + 2 * SSM_BC
SSM_GROUP_WIDTH = SSM_WIDTH // SSM_GROUPS
SSM_U_COLS = SSM_WIDTH + SSM_CONV_DIM + LANES

FFN_HIDDEN = ((8 * D_MODEL + 3 * 256 - 1) // (3 * 256)) * 256

SEQ_BLOCK = 512
MERGE_BLOCK = 256


def _bf(x):
    return x if x.dtype == BF16 else x.astype(BF16)


def _dot(a, b):
    return jnp.dot(_bf(a), _bf(b), preferred_element_type=F32)


def _dot_nt(a, b):
    return lax.dot_general(_bf(a), _bf(b), (((1,), (1,)), ((), ())), preferred_element_type=F32)


def _dot_tn(a, b):
    return lax.dot_general(_bf(a), _bf(b), (((0,), (0,)), ((), ())), preferred_element_type=F32)


def _split(x, n):
    parts = []
    rest = x
    for i in range(n):
        p = rest.astype(BF16)
        parts.append(p)
        if i + 1 < n:
            rest = rest - p.astype(F32)
    return parts


def _dot_sel_left(sel, x, passes=3):
    out = None
    for p in _split(x, passes):
        t = jnp.dot(sel, p, preferred_element_type=F32)
        out = t if out is None else out + t
    return out


def _dot_sel_right(x, sel, passes=3):
    out = None
    for p in _split(x, passes):
        t = jnp.dot(p, sel, preferred_element_type=F32)
        out = t if out is None else out + t
    return out


def _rmsnorm(x, gain):
    ms = jnp.mean(x * x, axis=-1, keepdims=True)
    return x * lax.rsqrt(ms + NORM_EPS) * gain


def _sigmoid(x):
    return 1.0 / (1.0 + jnp.exp(-x))


def _silu(x):
    return x * _sigmoid(x)


def _softplus(x):
    return jnp.maximum(x, 0.0) + jnp.log(1.0 + jnp.exp(-jnp.abs(x)))


def _iota(shape, axis):
    return lax.broadcasted_iota(jnp.int32, shape, axis)


def _interleave(*programs):
    live = list(programs)
    while live:
        for gen in list(live):
            try:
                next(gen)
            except StopIteration:
                live.remove(gen)


def _rwkv_kernel(x_ref, gain_ref, w_ref, mu_ref, wlr_ref, blr_ref, gup_ref, pv_ref, hsum_ref, tri_ref,
                 o_ref, ush_ref, st_ref, r_s, lw_s, k_s, v_s, ka_s, kb_s, o_s, g_s, *, ts):
    W = RWKV_WIDTH

    @pl.when(pl.program_id(1) == 0)
    def _():
        ush_ref[0:SUBLANES, :] = jnp.zeros((SUBLANES, RWKV_U_COLS), F32)
        st_ref[...] = jnp.zeros(st_ref.shape, F32)

    h = _bf(_rmsnorm(x_ref[...], gain_ref[...]))

    def project(lo, hi):
        ush_ref[SUBLANES:SUBLANES + ts, lo:hi] = _dot(h, w_ref[:, lo:hi])

    def token_shift(lo, hi):
        full = ush_ref[0:ts + SUBLANES, lo:hi]
        u = full[SUBLANES:, :]
        prev = pltpu.roll(full, 1, 0)[SUBLANES:, :]
        ush_ref[0:SUBLANES, lo:hi] = u[ts - SUBLANES:ts, :]
        return u + (prev - u) * mu_ref[:, lo:hi]

    project(3 * W, RWKV_U_COLS)
    project(W, 2 * W)
    low_rank = token_shift(3 * W, RWKV_U_COLS)
    xwa = low_rank[:, 0:LANES]
    xg = low_rank[:, LANES:]
    lane = _iota(xwa.shape, 1)
    lr_in = jnp.where(lane < RWKV_DECAY_RANK, jnp.tanh(xwa), xwa)
    lr = _dot(lr_in, wlr_ref[...]) + blr_ref[...]
    log_w = -jnp.exp(-_softplus(-lr[:, 0:W]) - 0.5)
    a = _sigmoid(lr[:, W:])
    g_s[...] = _dot(_sigmoid(xg), gup_ref[...])
    project(0, W)
    k = token_shift(W, 2 * W)

    k_k = pv_ref[0:1, :]
    k_a = pv_ref[1:2, :]
    r_k = pv_ref[2:3, :]
    gn_w = pv_ref[3:4, :]
    gn_b = pv_ref[4:5, :]
    hsum = hsum_ref[...]

    def head_sum(t):
        return jnp.concatenate([_dot(t[:, MXU_TILE * j:MXU_TILE * (j + 1)], hsum)
                                for j in range(W // MXU_TILE)], axis=1)

    kk = k * k_k
    ss = head_sum(kk * kk)
    kk = kk * lax.rsqrt(jnp.maximum(ss, 1e-24))
    kmod = k * (1.0 + (a - 1.0) * k_a)

    lw_s[...] = log_w
    k_s[...] = kmod
    ka_s[...] = -kk
    kb_s[...] = kk * a
    project(2 * W, 3 * W)
    r_s[...] = token_shift(0, W)
    v_s[...] = token_shift(2 * W, 3 * W)

    tri = tri_ref[...]
    gl = RWKV_GROUP_LANES
    n_groups = W // gl
    heads_per_group = gl // RWKV_HEAD_DIM
    row_w = _iota((CHUNK, gl), 0)
    col_w = _iota((CHUNK, gl), 1) & (RWKV_HEAD_DIM - 1)
    strict = row_w > col_w
    incl = row_w >= col_w
    eye_w = jnp.where(row_w == col_w, 1.0, 0.0)
    first_head = _iota((CHUNK, LANES), 1) < RWKV_HEAD_DIM
    head_shift = int(np.log2(RWKV_HEAD_DIM))
    same_head = (_iota((gl, gl), 0) >> head_shift) == (_iota((gl, gl), 1) >> head_shift)
    zeros_tile = jnp.zeros((CHUNK, LANES), F32)

    def blockdiag(t):
        blocks = []
        for hd in range(heads_per_group):
            tile = t[:, LANES * (hd // 2):LANES * (hd // 2 + 1)]
            kept = jnp.where(first_head, tile, 0.0) if hd % 2 == 0 else jnp.where(first_head, 0.0, tile)
            tiles = [kept if j == hd // 2 else zeros_tile for j in range(gl // LANES)]
            blocks.append(jnp.concatenate(tiles, axis=1))
        return jnp.concatenate(blocks, axis=0)

    lanes = [slice(gl * g, gl * (g + 1)) for g in range(n_groups)]

    def state_free(c, ctx):
        c0 = pl.multiple_of(c * CHUNK, CHUNK)
        sl = pl.ds(c0, CHUNK)
        lw = lw_s[sl, :]
        cum = _dot_sel_left(tri, lw, 2)
        e_last = jnp.exp(cum[CHUNK - 1:CHUNK, :])
        e_neg = jnp.exp(-cum)
        rt = r_s[sl, :] * jnp.exp(cum)
        at = ka_s[sl, :] * jnp.exp(cum - lw)
        bt = kb_s[sl, :] * e_neg
        kt = k_s[sl, :] * e_neg
        vv = v_s[sl, :]
        ar = [jnp.concatenate([at[:, ls], rt[:, ls]], axis=0) for ls in lanes]
        bk_bd = [jnp.concatenate([blockdiag(bt[:, ls]), blockdiag(kt[:, ls])], axis=0) for ls in lanes]
        v_bd = [blockdiag(vv[:, ls]) for ls in lanes]
        bk_end = [jnp.concatenate([bt[:, ls], kt[:, ls]], axis=0) * e_last[:, ls] for ls in lanes]
        yield
        gram = [_dot_nt(a, b) for a, b in zip(ar, bk_bd)]
        n_ab = [jnp.where(strict, g[0:CHUNK, 0:gl], 0.0) for g in gram]
        n_ak = [jnp.where(strict, g[0:CHUNK, gl:], 0.0) for g in gram]
        n_rbk = [jnp.concatenate([jnp.where(incl, g[CHUNK:, 0:gl], 0.0), jnp.where(incl, g[CHUNK:, gl:], 0.0)], axis=1)
                 for g in gram]
        yield
        ak_v = [_dot(n, v) for n, v in zip(n_ak, v_bd)]
        prod = [eye_w + n for n in n_ab]
        npow = [_dot(n, blockdiag(n)) for n in n_ab]
        for _ in range(int(np.log2(CHUNK)) - 2):
            yield
            both = [_dot(jnp.concatenate([nk, pr], axis=0), blockdiag(nk)) for nk, pr in zip(npow, prod)]
            prod = [pr + b[CHUNK:] for pr, b in zip(prod, both)]
            npow = [b[0:CHUNK] for b in both]
        yield
        inv = [pr + _dot(pr, blockdiag(nk)) for pr, nk in zip(prod, npow)]
        ctx.update(sl=sl, e_last=e_last, ar=ar, vv=vv, v_bd=v_bd, bk_end=bk_end, ak_v=ak_v, n_rbk=n_rbk, inv=inv)

    def state_bound(ctx, states):
        sl, e_last = ctx["sl"], ctx["e_last"]
        from_state = [_dot_nt(a, s) for a, s in zip(ctx["ar"], states)]
        yield
        u = [_dot(t, blockdiag(f[0:CHUNK] + r)) for t, f, r in zip(ctx["inv"], from_state, ctx["ak_v"])]
        yield
        for g, ls in enumerate(lanes):
            uv = jnp.concatenate([u[g], ctx["vv"][:, ls]], axis=0)
            states[g] = states[g] * e_last[:, ls] + jnp.where(same_head, _dot_tn(uv, ctx["bk_end"][g]), 0.0)
        yield
        for g, ls in enumerate(lanes):
            o_s[sl, ls] = from_state[g][CHUNK:] + _dot(ctx["n_rbk"][g],
                                                       jnp.concatenate([blockdiag(u[g]), ctx["v_bd"][g]], axis=0))

    def chained(*programs):
        for program in programs:
            yield from program

    def chunk_body(i, carry):
        n = RWKV_LOOP_CHUNKS
        states = [st_ref[g] for g in range(n_groups)]
        ctxs = [dict() for _ in range(n)]
        pending = []
        for j in range(0, n, 2):
            frees = [state_free(n * i + j + d, ctxs[j + d]) for d in range(2)]
            _interleave(*frees, *([chained(*pending)] if pending else []))
            pending = [state_bound(ctxs[j + d], states) for d in range(2)]
        _interleave(chained(*pending))
        for g in range(n_groups):
            st_ref[g] = states[g]
        return carry

    lax.fori_loop(0, ts // (CHUNK * RWKV_LOOP_CHUNKS), chunk_body, 0)

    o = o_s[...]
    inv_n = 1.0 / RWKV_HEAD_DIM
    mean = head_sum(o) * inv_n
    cen = o - mean
    var = head_sum(cen * cen) * inv_n
    o = cen * lax.rsqrt(var + RWKV_GN_EPS) * gn_w + gn_b
    bonus = head_sum(r_s[...] * k_s[...] * r_k)
    o = o + bonus * v_s[...]
    o_ref[...] = (o * g_s[...]).astype(o_ref.dtype)


def _hgrn_kernel(x_ref, gain_ref, w_ref, pv_ref, tri_ref, ones_ref, sel_ref, bsel_ref,
                 o_ref, st_ref, h_s, q_s, k_s, v_s, lf_s, o_s, *, ts):
    W = HGRN_WIDTH
    nsub = CHUNK // SUB

    @pl.when(pl.program_id(1) == 0)
    def _():
        st_ref[...] = jnp.zeros(st_ref.shape, F32)

    h = _bf(_rmsnorm(x_ref[...], gain_ref[...]))
    h_s[...] = h
    log_lb = pv_ref[0:1, :]
    log1m_lb = pv_ref[1:2, :]
    gn_w = pv_ref[2:3, :]
    f_pre = _dot(h, w_ref[:, W:2 * W])
    q_pre = _dot(h, w_ref[:, 0:W])
    e = jnp.exp(-jnp.abs(f_pre))
    b = log1m_lb - (jnp.maximum(-f_pre, 0.0) + jnp.log(1.0 + e))
    mx = jnp.maximum(log_lb, b)
    log_f = mx + jnp.log(1.0 + jnp.exp(-jnp.abs(log_lb - b)))
    lf_s[...] = log_f
    k_s[...] = pv_ref[3:4, :] * jnp.where(f_pre > 0.0, e, 1.0) / (1.0 + e)
    v_s[...] = _dot(h, w_ref[:, 2 * W:3 * W])
    q_s[...] = _silu(q_pre)

    tri = tri_ref[...]
    ones_sq = ones_ref[...]
    sel = sel_ref[...]
    sub_shift = int(np.log2(SUB))
    rblk = _iota((CHUNK, CHUNK), 0) >> sub_shift
    cblk = _iota((CHUNK, CHUNK), 1) >> sub_shift
    off_mask = cblk < rblk
    rowblk = _iota((CHUNK, LANES), 0) >> sub_shift
    s_idx = _iota((SUB, LANES), 0)

    heads = range(HGRN_HEADS)
    lanes = [slice(LANES * hd, LANES * (hd + 1)) for hd in heads]
    causal = _iota((CHUNK, CHUNK), 1) <= _iota((CHUNK, CHUNK), 0)

    def fast_chunk(c, states):
        c0 = pl.multiple_of(c * CHUNK, CHUNK)
        sl = pl.ds(c0, CHUNK)
        cum = _dot_sel_left(tri, lf_s[sl, :], 3)
        q = q_s[sl, :]
        k = k_s[sl, :]
        v = v_s[sl, :]
        yield
        starts = [None] + [cum[SUB * i - 1:SUB * i, :] for i in range(1, nsub)]
        cs = jnp.zeros_like(cum)
        rowblk_w = _iota(cum.shape, 0) >> sub_shift
        for i in range(1, nsub):
            cs = jnp.where(rowblk_w == i, starts[i], cs)
        q_sub = q * jnp.exp(cum - cs)
        k_own = k * jnp.exp(cs - cum)
        k_prev = [None] + [k[0:SUB * i, :] * jnp.exp(jnp.minimum(starts[i] - cum[0:SUB * i, :], 0.0))
                           for i in range(1, nsub)]
        yield
        last = cum[CHUNK - 1:CHUNK, :]
        q_in = q * jnp.exp(cum)
        k_end = k * jnp.exp(last - cum)
        e_last = jnp.exp(last)
        before = list(states)
        for hd, ls in enumerate(lanes):
            states[hd] = before[hd] * e_last[:, ls] + _dot_tn(v[:, ls], k_end[:, ls])
        yield
        scores = []
        for n, ls in enumerate(lanes):
            rows = []
            for i in range(nsub):
                parts = [k_own[SUB * i:SUB * (i + 1), ls]]
                if i > 0:
                    parts = [k_prev[i][:, ls]] + parts
                if i + 1 < nsub:
                    parts.append(jnp.zeros((CHUNK - SUB * (i + 1), LANES), F32))
                rows.append(_dot_nt(q_sub[SUB * i:SUB * (i + 1), ls], jnp.concatenate(parts, axis=0)))
            scores.append(jnp.where(causal, jnp.concatenate(rows, axis=0), 0.0))
            if n % 4 == 3:
                yield
        outs = [_dot(a, v[:, ls]) + _dot_nt(q_in[:, ls], s) for a, ls, s in zip(scores, lanes, before)]
        yield
        for hd, ls in enumerate(lanes):
            o_s[sl, ls] = outs[hd]

    def fast_chunk_body(i, carry):
        states = [st_ref[hd] for hd in heads]
        _interleave(*[fast_chunk(LOOP_CHUNKS * i + j, states) for j in range(LOOP_CHUNKS)])
        for hd in heads:
            st_ref[hd] = states[hd]
        return carry

    def chunk_body(c, carry):
        c0 = pl.multiple_of(c * CHUNK, CHUNK)
        sl = pl.ds(c0, CHUNK)
        cum_all = _dot_sel_left(tri, lf_s[sl, :], 3)
        q_all = q_s[sl, :]
        k_all = k_s[sl, :]
        v_all = v_s[sl, :]
        for hd in range(HGRN_HEADS):
            ls = slice(LANES * hd, LANES * (hd + 1))
            cum = cum_all[:, ls]
            q = q_all[:, ls]
            k = k_all[:, ls]
            v = v_all[:, ls]
            vb = _bf(v)
            last = cum[CHUNK - 1:CHUNK, :]
            starts = [None] + [cum[SUB * i - 1:SUB * i, :] for i in range(1, nsub)]
            cs = jnp.zeros_like(cum)
            for i in range(1, nsub):
                cs = jnp.where(rowblk == i, starts[i], cs)
            q_sub = q * jnp.exp(cum - cs)
            rows = [jnp.zeros((SUB, CHUNK), F32)]
            for i in range(1, nsub):
                k_i = k * jnp.exp(jnp.minimum(starts[i] - cum, 0.0))
                rows.append(_dot_nt(q_sub[SUB * i:SUB * (i + 1), :], k_i))
            a_off = jnp.where(off_mask, jnp.concatenate(rows, axis=0), 0.0)
            state = st_ref[hd]
            o = _dot(a_off, vb) + _dot_nt(q * jnp.exp(cum), state)
            diag = []
            for j in range(nsub):
                rs = slice(SUB * j, SUB * (j + 1))
                cum_b = cum[rs, :]
                q_b = q[rs, :]
                k_b = k[rs, :]
                pieces = []
                for t in range(SUB):
                    expo = jnp.where(s_idx <= t, cum_b[t:t + 1, :] - cum_b, -1e30)
                    pieces.append(jnp.exp(expo) * k_b * q_b[t:t + 1, :])
                pmat = jnp.concatenate(pieces, axis=0)
                score = _dot(pmat, ones_sq)
                wv = score * jnp.concatenate([v[rs, :]] * SUB, axis=0)
                diag.append(_dot(sel, wv))
            o = o + jnp.concatenate(diag, axis=0)
            o_s[sl, ls] = o
            k_end = k * jnp.exp(last - cum)
            st_ref[hd] = state * jnp.exp(last) + _dot_tn(vb, k_end)
        return carry

    sub_sums = _dot_sel_left(bsel_ref[...], log_f, 3)
    fast_ok = jnp.min(sub_sums) >= -HGRN_FAST_LOG_RANGE

    @pl.when(fast_ok)
    def _():
        lax.fori_loop(0, ts // (CHUNK * LOOP_CHUNKS), fast_chunk_body, 0)

    @pl.when(jnp.logical_not(fast_ok))
    def _():
        lax.fori_loop(0, ts // CHUNK, chunk_body, 0)

    gate = _sigmoid(_dot(h_s[...], w_ref[:, 3 * W:]))
    for hd in range(HGRN_HEADS):
        ls = slice(LANES * hd, LANES * (hd + 1))
        o = o_s[:, ls]
        o = o * lax.rsqrt(jnp.mean(o * o, axis=-1, keepdims=True) + NORM_EPS) * gn_w[:, ls]
        o_ref[:, ls] = (o * gate[:, ls]).astype(o_ref.dtype)


def _ssm_kernel(x_ref, gain_ref, w_ref, cw_ref, cb_ref, hv_ref, wv_ref, tri_ref, exp_ref,
                o_ref, xb_ref, st_ref, h_s, xs_s, ec_s, xd_s, b_s, c_s, y_s, *, ts):
    W = SSM_WIDTH
    gw = SSM_GROUP_WIDTH
    pairs = W // LANES

    @pl.when(pl.program_id(1) == 0)
    def _():
        xb_ref[0:SUBLANES, :] = jnp.zeros((SUBLANES, SSM_CONV_DIM), F32)
        st_ref[...] = jnp.zeros(st_ref.shape, F32)

    h = _bf(_rmsnorm(x_ref[...], gain_ref[...]))
    h_s[...] = h

    dt_bias = hv_ref[0:1, :]
    neg_a = hv_ref[1:2, :]
    dt = _softplus(_dot(h, w_ref[:, W + SSM_CONV_DIM:]) + dt_bias)
    log_a = dt * neg_a
    tri = tri_ref[...]
    cum = jnp.concatenate([_dot_sel_left(tri, log_a[CHUNK * c:CHUNK * (c + 1), :], 3)
                           for c in range(ts // CHUNK)], axis=0)
    cum_parts = _split(cum, 3)
    dt_parts = _split(dt, 2)

    def project(lo, hi):
        xb_ref[SUBLANES:SUBLANES + ts, lo:hi] = _dot(h, w_ref[:, W + lo:W + hi])

    def conv_silu(lo, hi):
        full = xb_ref[0:ts + SUBLANES, lo:hi]
        conv = cb_ref[:, lo:hi] + full[SUBLANES:, :] * cw_ref[SSM_CONV_WIDTH - 1:SSM_CONV_WIDTH, lo:hi]
        for k in range(1, SSM_CONV_WIDTH):
            j = SSM_CONV_WIDTH - 1 - k
            conv = conv + pltpu.roll(full, k, 0)[SUBLANES:, :] * cw_ref[j:j + 1, lo:hi]
        xb_ref[0:SUBLANES, lo:hi] = xb_ref[ts:ts + SUBLANES, lo:hi]
        return _silu(conv)

    def finish(idx):
        lo, hi = slabs[idx]
        act = conv_silu(lo, hi)
        if idx < SSM_GROUPS:
            expand = exp_ref[:, lo:hi]
            xs_s[:, lo:hi] = act
            xd_s[:, lo:hi] = act * sum(jnp.dot(p, expand, preferred_element_type=F32) for p in dt_parts)
            ec_s[:, lo:hi] = sum(jnp.dot(p, expand, preferred_element_type=F32) for p in cum_parts)
        elif idx == SSM_GROUPS:
            b_s[...] = act
        else:
            c_s[...] = act

    slabs = [(gw * g, gw * (g + 1)) for g in range(SSM_GROUPS)] + [(W, W + SSM_BC), (W + SSM_BC, W + 2 * SSM_BC)]
    project(*slabs[0])
    for idx in range(len(slabs)):
        if idx + 1 < len(slabs):
            project(*slabs[idx + 1])
        finish(idx)

    t_idx = _iota((CHUNK, W), 0)
    s_idx = _iota((CHUNK, W), 1) & (CHUNK - 1)
    causal = s_idx <= t_idx
    on_diag = s_idx == t_idx
    first_head = (_iota((CHUNK, LANES), 1) < SSM_HEAD_DIM)

    groups = range(SSM_GROUPS)
    glanes = [slice(gw * g, gw * (g + 1)) for g in groups]
    per_group = gw // LANES

    def chunk(c, states):
        c0 = pl.multiple_of(c * CHUNK, CHUNK)
        sl = pl.ds(c0, CHUNK)
        ec = ec_s[sl, :]
        xd = xd_s[sl, :]
        bm = b_s[sl, :]
        cm = c_s[sl, :]
        last = ec[CHUNK - 1:CHUNK, :]
        by_src = jnp.sum(jnp.where(on_diag, ec, 0.0), axis=0, keepdims=True)
        decay = jnp.exp(jnp.where(causal, ec - by_src, -1e30))
        yield
        xw = xd * jnp.exp(last - ec)
        e_in = jnp.exp(ec)
        e_last = jnp.exp(last)
        c_g = [_bf(cm[:, SSM_STATE * g:SSM_STATE * (g + 1)]) for g in groups]
        b_g = [_bf(bm[:, SSM_STATE * g:SSM_STATE * (g + 1)]) for g in groups]
        cb2 = [_dot_nt(c, jnp.concatenate([b, b], axis=0)) for c, b in zip(c_g, b_g)]
        yield
        before = list(states)
        for g, gl in enumerate(glanes):
            states[g] = before[g] * e_last[:, gl] + _dot_tn(b_g[g], xw[:, gl])
        y_in = [_dot(c, s) for c, s in zip(c_g, before)]
        yield
        ys = []
        for p in range(pairs):
            ls = slice(LANES * p, LANES * (p + 1))
            m = decay[:, ls] * cb2[(LANES * p) // gw]
            xp = xd[:, ls]
            x2 = jnp.concatenate([jnp.where(first_head, xp, 0.0), jnp.where(first_head, 0.0, xp)], axis=0)
            ys.append(_dot(m, x2))
            if p % 4 == 3:
                yield
        for g, gl in enumerate(glanes):
            y_s[sl, gl] = (y_in[g] * e_in[:, gl]
                           + jnp.concatenate(ys[per_group * g:per_group * (g + 1)], axis=1))

    def chunk_body(i, carry):
        states = [st_ref[g] for g in groups]
        _interleave(*[chunk(LOOP_CHUNKS * i + j, states) for j in range(LOOP_CHUNKS)])
        for g in groups:
            st_ref[g] = states[g]
        return carry

    lax.fori_loop(0, ts // (CHUNK * LOOP_CHUNKS), chunk_body, 0)

    d_skip = wv_ref[0:1, :]
    gn_w = wv_ref[1:2, :]
    h = h_s[...]
    for g in range(SSM_GROUPS):
        gl = slice(gw * g, gw * (g + 1))
        z = _dot(h, w_ref[:, gl])
        yg = (y_s[:, gl] + d_skip[:, gl] * xs_s[:, gl]) * _silu(z)
        yg = yg * lax.rsqrt(jnp.mean(yg * yg, axis=-1, keepdims=True) + NORM_EPS)
        o_ref[:, gl] = (yg * gn_w[:, gl]).astype(o_ref.dtype)


def _merge_kernel(x_ref, ya_ref, yb_ref, yc_ref, gmix_ref, wg_ref, wb_ref, wout_ref,
                  gffn_ref, wfi_ref, wfo_ref, gfin_ref, o_ref, *, final_norm):
    x = x_ref[...]
    h = _rmsnorm(x, gmix_ref[...])
    gates = _sigmoid(_dot(h, wg_ref[...]))
    rows_b = RWKV_WIDTH + HGRN_WIDTH
    merged = (gates[:, 0:D_MODEL] * jnp.dot(ya_ref[...], wb_ref[0:RWKV_WIDTH, :], preferred_element_type=F32)
              + gates[:, D_MODEL:2 * D_MODEL] * jnp.dot(yb_ref[...], wb_ref[RWKV_WIDTH:rows_b, :],
                                                        preferred_element_type=F32)
              + gates[:, 2 * D_MODEL:] * jnp.dot(yc_ref[...], wb_ref[rows_b:, :], preferred_element_type=F32))
    x = x + _dot(merged, wout_ref[...])
    h = _rmsnorm(x, gffn_ref[...])
    gu = _dot(h, wfi_ref[...])
    act = _silu(gu[:, 0:FFN_HIDDEN]) * gu[:, FFN_HIDDEN:]
    x = x + _dot(act, wfo_ref[...])
    if final_norm:
        x = _rmsnorm(x, gfin_ref[...])
    o_ref[...] = x


class _PerLayer:
    def __init__(self, value):
        self.value = value


def _const_spec(const, layer):
    if isinstance(const, _PerLayer):
        shape = const.value.shape[1:]
        return pl.BlockSpec((None,) + shape, lambda b, s, _l=layer, _nd=len(shape): (_l,) + (0,) * _nd,
                            pipeline_mode=pl.Buffered(1))
    return pl.BlockSpec(const.shape, lambda b, s, _nd=const.ndim: (0,) * _nd, pipeline_mode=pl.Buffered(1))


def _const_value(const):
    return const.value if isinstance(const, _PerLayer) else const


def _per_layer(fn, *stacked_args):
    return [_PerLayer(v) for v in jax.vmap(fn)(*stacked_args)]


def _seq_spec(ts, width):
    return pl.BlockSpec((None, ts, width), lambda b, s: (b, s, 0))


def _params():
    return pltpu.CompilerParams(dimension_semantics=("arbitrary", "arbitrary"),
                                vmem_limit_bytes=VMEM_LIMIT_BYTES)


def _run_mixer(kernel_fn, x, layer, consts, out_width, scratch, ts):
    bsz, seq, _ = x.shape
    return pl.pallas_call(
        functools.partial(kernel_fn, ts=ts),
        grid=(bsz, seq // ts),
        in_specs=[_seq_spec(ts, D_MODEL)] + [_const_spec(c, layer) for c in consts],
        out_specs=_seq_spec(ts, out_width),
        out_shape=jax.ShapeDtypeStruct((bsz, seq, out_width), BF16),
        scratch_shapes=scratch,
        compiler_params=_params(),
    )(x, *[_const_value(c) for c in consts])


def _row(v):
    return v.astype(F32).reshape(1, -1)


def _pad_cols(m, width):
    return jnp.pad(m, ((0, 0), (0, width - m.shape[1])))


def _rows8(rows, width):
    m = jnp.concatenate([_row(r) for r in rows], axis=0)
    return jnp.pad(m, ((0, SUBLANES - m.shape[0]), (0, width - m.shape[1])))


def _head_sum_matrix(width, head_dim):
    idx = np.arange(width) // head_dim
    return jnp.asarray(idx[:, None] == idx[None, :], dtype=BF16)


def _tri_ones(n):
    return jnp.asarray(np.tril(np.ones((n, n), np.float32)), dtype=BF16)


def _rwkv_layer_consts(gain, w_in, mu, w0, w_up, a0, a_up, g_up, k_k, k_a, r_k, gn_w, gn_b):
    W = RWKV_WIDTH
    dr, ir, gr = RWKV_DECAY_RANK, RWKV_ICL_RANK, RWKV_GATE_RANK
    w = _pad_cols(w_in, RWKV_U_COLS).astype(BF16)
    mu_p = _pad_cols(_row(mu), RWKV_U_COLS)
    wlr = jnp.zeros((LANES, 2 * W), F32).at[0:dr, 0:W].set(w_up).at[dr:dr + ir, W:].set(a_up).astype(BF16)
    blr = jnp.concatenate([_row(w0), _row(a0)], axis=1)
    gup = jnp.pad(g_up, ((0, RWKV_GATE_PAD - gr), (0, 0))).astype(BF16)
    pv = _rows8([k_k, k_a, r_k.reshape(-1), gn_w, gn_b], W)
    return _row(gain), w, mu_p, wlr, blr, gup, pv


def _rwkv_branch(x, layer, layer_consts, ts):
    W = RWKV_WIDTH
    consts = layer_consts + [_head_sum_matrix(MXU_TILE, RWKV_HEAD_DIM), _tri_ones(CHUNK)]
    scratch = ([pltpu.VMEM((ts + SUBLANES, RWKV_U_COLS), F32),
                pltpu.VMEM((W // RWKV_GROUP_LANES, RWKV_GROUP_LANES, RWKV_GROUP_LANES), F32)]
               + [pltpu.VMEM((ts, W), F32) for _ in range(8)])
    return _run_mixer(_rwkv_kernel, x, layer, consts, W, scratch, ts)


def _hgrn_layer_consts(gain, w_in, lb, gn_w):
    lb = lb.astype(F32)
    pv = _rows8([jnp.log(lb), jnp.log1p(-lb), gn_w, 1.0 - lb], HGRN_WIDTH)
    return _row(gain), w_in.astype(BF16), pv


def _hgrn_branch(x, layer, layer_consts, ts):
    W = HGRN_WIDTH
    t_i = np.arange(SUB)[:, None]
    pair = np.arange(SUB * SUB)[None, :]
    sel = jnp.asarray((pair // SUB == t_i) & (pair % SUB <= t_i), dtype=BF16)
    bsel = jnp.asarray(np.arange(ts // SUB)[:, None] == (np.arange(ts) // SUB)[None, :], dtype=BF16)
    consts = layer_consts + [_tri_ones(CHUNK), jnp.ones((LANES, LANES), BF16), sel, bsel]
    scratch = ([pltpu.VMEM((HGRN_HEADS, HGRN_HEAD_DIM, HGRN_HEAD_DIM), F32), pltpu.VMEM((ts, D_MODEL), BF16)]
               + [pltpu.VMEM((ts, W), F32) for _ in range(5)])
    return _run_mixer(_hgrn_kernel, x, layer, consts, W, scratch, ts)


def _ssm_layer_consts(gain, w_in, conv_w, conv_b, dt_bias, a_log, d_skip, gn_w):
    w = _pad_cols(w_in, SSM_U_COLS).astype(BF16)
    cw = jnp.pad(conv_w.astype(F32).T, ((0, SUBLANES - SSM_CONV_WIDTH), (0, 0)))
    hv = _rows8([dt_bias, -jnp.exp(a_log.astype(F32))], LANES)
    wv = _rows8([jnp.repeat(d_skip.astype(F32), SSM_HEAD_DIM), gn_w], SSM_WIDTH)
    return _row(gain), w, cw, _row(conv_b), hv, wv


def _ssm_branch(x, layer, layer_consts, ts):
    W = SSM_WIDTH
    head_of_lane = np.arange(W) // SSM_HEAD_DIM
    expand = jnp.asarray(np.arange(LANES)[:, None] == head_of_lane[None, :], dtype=BF16)
    consts = layer_consts + [_tri_ones(CHUNK), expand]
    scratch = [pltpu.VMEM((ts + SUBLANES, SSM_CONV_DIM), F32),
               pltpu.VMEM((SSM_GROUPS, SSM_STATE, SSM_GROUP_WIDTH), F32),
               pltpu.VMEM((ts, D_MODEL), BF16), pltpu.VMEM((ts, W), F32),
               pltpu.VMEM((ts, W), F32), pltpu.VMEM((ts, W), F32),
               pltpu.VMEM((ts, SSM_BC), F32), pltpu.VMEM((ts, SSM_BC), F32),
               pltpu.VMEM((ts, W), F32)]
    return _run_mixer(_ssm_kernel, x, layer, consts, W, scratch, ts)


def _merge_layer_consts(gain_mix, w_gate, w_branch, w_out, gain_ffn, w_ffn_in, w_ffn_out):
    return (_row(gain_mix), w_gate.astype(BF16), w_branch.astype(BF16), w_out.astype(BF16), _row(gain_ffn),
            w_ffn_in.astype(BF16), w_ffn_out.astype(BF16))


def _merge_ffn(x, ya, yb, yc, layer, layer_consts, gain_final, final_norm, ts):
    bsz, seq, _ = x.shape
    consts = layer_consts + [_row(gain_final)]
    seqs = [x, ya, yb, yc]
    return pl.pallas_call(
        functools.partial(_merge_kernel, final_norm=final_norm),
        grid=(bsz, seq // ts),
        in_specs=[_seq_spec(ts, a.shape[-1]) for a in seqs] + [_const_spec(c, layer) for c in consts],
        out_specs=_seq_spec(ts, D_MODEL),
        out_shape=jax.ShapeDtypeStruct((bsz, seq, D_MODEL), F32),
        compiler_params=_params(),
    )(*seqs, *[_const_value(c) for c in consts])


def kernel(x, norm_mix, w_in, rwkv_mu, rwkv_w0, rwkv_w_up, rwkv_a0, rwkv_a_up, rwkv_g_up, rwkv_k_k, rwkv_k_a,
           rwkv_r_k, rwkv_gn_w, rwkv_gn_b, hgrn_lb_logits, hgrn_gn_w, ssm_conv_w, ssm_conv_b, ssm_dt_bias,
           ssm_a_log, ssm_d, ssm_gn_w, w_branch, w_out, norm_ffn, w_ffn_in, w_ffn_out, norm_final):
    bsz, seq, d = x.shape
    assert d == D_MODEL
    depth = w_in.shape[0]
    ts = min(SEQ_BLOCK, seq)
    tm = min(MERGE_BLOCK, seq)
    assert seq % ts == 0 and seq % tm == 0
    assert ts % (CHUNK * LOOP_CHUNKS) == 0 and ts % (CHUNK * RWKV_LOOP_CHUNKS) == 0

    rwkv_cols = 3 * RWKV_WIDTH + RWKV_DECAY_RANK + RWKV_ICL_RANK + RWKV_GATE_RANK
    off_hgrn = rwkv_cols
    off_ssm = off_hgrn + 4 * HGRN_WIDTH
    off_gate = off_ssm + SSM_WIDTH + SSM_CONV_DIM + SSM_HEADS

    cs = jnp.cumsum(jax.nn.softmax(hgrn_lb_logits.astype(F32), axis=0), axis=0)
    lbs = cs - cs[:1]

    w_in_bf = w_in.astype(BF16)
    rwkv_consts = _per_layer(_rwkv_layer_consts, norm_mix, w_in_bf[:, :, :off_hgrn], rwkv_mu, rwkv_w0, rwkv_w_up,
                             rwkv_a0, rwkv_a_up, rwkv_g_up, rwkv_k_k, rwkv_k_a, rwkv_r_k, rwkv_gn_w, rwkv_gn_b)
    hgrn_consts = _per_layer(_hgrn_layer_consts, norm_mix, w_in_bf[:, :, off_hgrn:off_ssm], lbs, hgrn_gn_w)
    ssm_consts = _per_layer(_ssm_layer_consts, norm_mix, w_in_bf[:, :, off_ssm:off_gate], ssm_conv_w, ssm_conv_b,
                            ssm_dt_bias, ssm_a_log, ssm_d, ssm_gn_w)
    merge_consts = _per_layer(_merge_layer_consts, norm_mix, w_in_bf[:, :, off_gate:], w_branch, w_out, norm_ffn,
                              w_ffn_in, w_ffn_out)

    x = x.astype(F32)
    for l in range(depth):
        ya = _rwkv_branch(x, l, rwkv_consts, ts)
        yb = _hgrn_branch(x, l, hgrn_consts, ts)
        yc = _ssm_branch(x, l, ssm_consts, ts)
        x = _merge_ffn(x, ya, yb, yc, l, merge_consts, norm_final, l == depth - 1, tm)
    return x
```

```python
import functools

import numpy as np
import jax
import jax.numpy as jnp
from jax import lax
from jax.experimental import pallas as pl
from jax.experimental.pallas import tpu as pltpu

F32 = jnp.float32
BF16 = jnp.bfloat16

D_MODEL = 1024
CHUNK = 64
LOOP_CHUNKS = 2
HGRN_LOOP_CHUNKS = 8
SUB = 16
HGRN_FAST_LOG_RANGE = 60.0
NORM_EPS = 1e-5
LANES = 128
SUBLANES = 8
MXU_TILE = 256
VMEM_LIMIT_BYTES = 56 * 1024 * 1024

RWKV_HEAD_DIM = 64
RWKV_WIDTH = D_MODEL
RWKV_DECAY_RANK = 64
RWKV_ICL_RANK = 64
RWKV_GATE_RANK = 160
RWKV_GATE_PAD = 256
RWKV_GN_EPS = 64e-5
RWKV_LOOP_CHUNKS = 8
RWKV_GROUP_LANES = MXU_TILE
RWKV_U_COLS = 3 * RWKV_WIDTH + LANES + RWKV_GATE_PAD

HGRN_HEAD_DIM = 128
HGRN_WIDTH = D_MODEL
HGRN_HEADS = HGRN_WIDTH // HGRN_HEAD_DIM

SSM_WIDTH = 2 * D_MODEL
SSM_HEAD_DIM = 64
SSM_HEADS = SSM_WIDTH // SSM_HEAD_DIM
SSM_GROUPS = 4
SSM_STATE = 128
SSM_CONV_WIDTH = 4
SSM_BC = SSM_GROUPS * SSM_STATE
SSM_CONV_DIM = SSM_WIDTH + 2 * SSM_BC
SSM_GROUP_WIDTH = SSM_WIDTH // SSM_GROUPS
SSM_U_COLS = SSM_WIDTH + SSM_CONV_DIM + LANES

FFN_HIDDEN = ((8 * D_MODEL + 3 * 256 - 1) // (3 * 256)) * 256

SEQ_BLOCK = 512
MERGE_BLOCK = 256


def _bf(x):
    return x if x.dtype == BF16 else x.astype(BF16)


def _dot(a, b):
    return jnp.dot(_bf(a), _bf(b), preferred_element_type=F32)


def _dot_nt(a, b):
    return lax.dot_general(_bf(a), _bf(b), (((1,), (1,)), ((), ())), preferred_element_type=F32)


def _dot_tn(a, b):
    return lax.dot_general(_bf(a), _bf(b), (((0,), (0,)), ((), ())), preferred_element_type=F32)


def _split(x, n):
    parts = []
    rest = x
    for i in range(n):
        p = rest.astype(BF16)
        parts.append(p)
        if i + 1 < n:
            rest = rest - p.astype(F32)
    return parts


def _dot_sel_left(sel, x, passes=3):
    out = None
    for p in _split(x, passes):
        t = jnp.dot(sel, p, preferred_element_type=F32)
        out = t if out is None else out + t
    return out


def _rmsnorm(x, gain):
    ms = jnp.mean(x * x, axis=-1, keepdims=True)
    return x * lax.rsqrt(ms + NORM_EPS) * gain


def _sigmoid(x):
    return 1.0 / (1.0 + jnp.exp(-x))


def _silu(x):
    return x * _sigmoid(x)


def _softplus(x):
    return jnp.maximum(x, 0.0) + jnp.log(1.0 + jnp.exp(-jnp.abs(x)))


def _iota(shape, axis):
    return lax.broadcasted_iota(jnp.int32, shape, axis)


def _interleave(*programs):
    live = list(programs)
    while live:
        for gen in list(live):
            try:
                next(gen)
            except StopIteration:
                live.remove(gen)


def _rwkv_kernel(x_ref, gain_ref, w_ref, mu_ref, wlr_ref, blr_ref, gup_ref, pv_ref, hsum_ref, tri_ref,
                 o_ref, ush_ref, st_ref, r_s, lw_s, k_s, v_s, ka_s, kb_s, o_s, g_s, *, ts):
    W = RWKV_WIDTH

    @pl.when(pl.program_id(1) == 0)
    def _():
        ush_ref[0:SUBLANES, :] = jnp.zeros((SUBLANES, RWKV_U_COLS), F32)
        st_ref[...] = jnp.zeros(st_ref.shape, F32)

    h = _bf(_rmsnorm(x_ref[...], gain_ref[...]))

    def project(lo, hi):
        ush_ref[SUBLANES:SUBLANES + ts, lo:hi] = _dot(h, w_ref[:, lo:hi])

    def token_shift(lo, hi):
        full = ush_ref[0:ts + SUBLANES, lo:hi]
        u = full[SUBLANES:, :]
        prev = pltpu.roll(full, 1, 0)[SUBLANES:, :]
        ush_ref[0:SUBLANES, lo:hi] = u[ts - SUBLANES:ts, :]
        return u + (prev - u) * mu_ref[:, lo:hi]

    project(3 * W, RWKV_U_COLS)
    project(W, 2 * W)
    low_rank = token_shift(3 * W, RWKV_U_COLS)
    xwa = low_rank[:, 0:LANES]
    xg = low_rank[:, LANES:]
    lane = _iota(xwa.shape, 1)
    lr_in = jnp.where(lane < RWKV_DECAY_RANK, jnp.tanh(xwa), xwa)
    lr = _dot(lr_in, wlr_ref[...]) + blr_ref[...]
    log_w = -jnp.exp(-_softplus(-lr[:, 0:W]) - 0.5)
    a = _sigmoid(lr[:, W:])
    g_s[...] = _dot(_sigmoid(xg), gup_ref[...])
    project(0, W)
    k = token_shift(W, 2 * W)

    k_k = pv_ref[0:1, :]
    k_a = pv_ref[1:2, :]
    r_k = pv_ref[2:3, :]
    gn_w = pv_ref[3:4, :]
    gn_b = pv_ref[4:5, :]
    hsum = hsum_ref[...]

    def head_sum(t):
        return jnp.concatenate([_dot(t[:, MXU_TILE * j:MXU_TILE * (j + 1)], hsum)
                                for j in range(W // MXU_TILE)], axis=1)

    kk = k * k_k
    ss = head_sum(kk * kk)
    kk = kk * lax.rsqrt(jnp.maximum(ss, 1e-24))
    kmod = k * (1.0 + (a - 1.0) * k_a)

    lw_s[...] = log_w
    k_s[...] = kmod
    ka_s[...] = -kk
    kb_s[...] = kk * a
    project(2 * W, 3 * W)
    r_s[...] = token_shift(0, W)
    v_s[...] = token_shift(2 * W, 3 * W)

    tri = tri_ref[...]
    gl = RWKV_GROUP_LANES
    n_groups = W // gl
    heads_per_group = gl // RWKV_HEAD_DIM
    row_w = _iota((CHUNK, gl), 0)
    col_w = _iota((CHUNK, gl), 1) & (RWKV_HEAD_DIM - 1)
    strict = row_w > col_w
    incl = row_w >= col_w
    eye_w = jnp.where(row_w == col_w, 1.0, 0.0)
    first_head = _iota((CHUNK, LANES), 1) < RWKV_HEAD_DIM
    head_shift = int(np.log2(RWKV_HEAD_DIM))
    same_head = (_iota((gl, gl), 0) >> head_shift) == (_iota((gl, gl), 1) >> head_shift)
    zeros_tile = jnp.zeros((CHUNK, LANES), F32)

    def blockdiag(t):
        blocks = []
        for hd in range(heads_per_group):
            tile = t[:, LANES * (hd // 2):LANES * (hd // 2 + 1)]
            kept = jnp.where(first_head, tile, 0.0) if hd % 2 == 0 else jnp.where(first_head, 0.0, tile)
            tiles = [kept if j == hd // 2 else zeros_tile for j in range(gl // LANES)]
            blocks.append(jnp.concatenate(tiles, axis=1))
        return jnp.concatenate(blocks, axis=0)

    lanes = [slice(gl * g, gl * (g + 1)) for g in range(n_groups)]

    def state_free(c, ctx):
        c0 = pl.multiple_of(c * CHUNK, CHUNK)
        sl = pl.ds(c0, CHUNK)
        lw = lw_s[sl, :]
        cum = _dot_sel_left(tri, lw, 2)
        e_last = jnp.exp(cum[CHUNK - 1:CHUNK, :])
        e_neg = jnp.exp(-cum)
        rt = r_s[sl, :] * jnp.exp(cum)
        at = ka_s[sl, :] * jnp.exp(cum - lw)
        bt = kb_s[sl, :] * e_neg
        kt = k_s[sl, :] * e_neg
        vv = v_s[sl, :]
        ar = [jnp.concatenate([at[:, ls], rt[:, ls]], axis=0) for ls in lanes]
        bk_bd = [jnp.concatenate([blockdiag(bt[:, ls]), blockdiag(kt[:, ls])], axis=0) for ls in lanes]
        v_bd = [blockdiag(vv[:, ls]) for ls in lanes]
        bk_end = [jnp.concatenate([bt[:, ls], kt[:, ls]], axis=0) * e_last[:, ls] for ls in lanes]
        yield
        gram = [_dot_nt(a, b) for a, b in zip(ar, bk_bd)]
        n_ab = [jnp.where(strict, g[0:CHUNK, 0:gl], 0.0) for g in gram]
        n_ak = [jnp.where(strict, g[0:CHUNK, gl:], 0.0) for g in gram]
        n_rbk = [jnp.concatenate([jnp.where(incl, g[CHUNK:, 0:gl], 0.0), jnp.where(incl, g[CHUNK:, gl:], 0.0)], axis=1)
                 for g in gram]
        yield
        ak_v = [_dot(n, v) for n, v in zip(n_ak, v_bd)]
        prod = [eye_w + n for n in n_ab]
        npow = [_dot(n, blockdiag(n)) for n in n_ab]
        for _ in range(int(np.log2(CHUNK)) - 2):
            yield
            both = [_dot(jnp.concatenate([nk, pr], axis=0), blockdiag(nk)) for nk, pr in zip(npow, prod)]
            prod = [pr + b[CHUNK:] for pr, b in zip(prod, both)]
            npow = [b[0:CHUNK] for b in both]
        yield
        inv = [pr + _dot(pr, blockdiag(nk)) for pr, nk in zip(prod, npow)]
        ctx.update(sl=sl, e_last=e_last, ar=ar, vv=vv, v_bd=v_bd, bk_end=bk_end, ak_v=ak_v, n_rbk=n_rbk, inv=inv)

    def state_bound(ctx, states):
        sl, e_last = ctx["sl"], ctx["e_last"]
        from_state = [_dot_nt(a, s) for a, s in zip(ctx["ar"], states)]
        yield
        u = [_dot(t, blockdiag(f[0:CHUNK] + r)) for t, f, r in zip(ctx["inv"], from_state, ctx["ak_v"])]
        yield
        for g, ls in enumerate(lanes):
            uv = jnp.concatenate([u[g], ctx["vv"][:, ls]], axis=0)
            states[g] = states[g] * e_last[:, ls] + jnp.where(same_head, _dot_tn(uv, ctx["bk_end"][g]), 0.0)
        yield
        for g, ls in enumerate(lanes):
            o_s[sl, ls] = from_state[g][CHUNK:] + _dot(ctx["n_rbk"][g],
                                                       jnp.concatenate([blockdiag(u[g]), ctx["v_bd"][g]], axis=0))

    def chained(*programs):
        for program in programs:
            yield from program

    def chunk_body(i, carry):
        n = RWKV_LOOP_CHUNKS
        states = [st_ref[g] for g in range(n_groups)]
        ctxs = [dict() for _ in range(n)]
        pending = []
        for j in range(0, n, 2):
            frees = [state_free(n * i + j + d, ctxs[j + d]) for d in range(2)]
            _interleave(*frees, *([chained(*pending)] if pending else []))
            pending = [state_bound(ctxs[j + d], states) for d in range(2)]
        _interleave(chained(*pending))
        for g in range(n_groups):
            st_ref[g] = states[g]
        return carry

    lax.fori_loop(0, ts // (CHUNK * RWKV_LOOP_CHUNKS), chunk_body, 0)

    o = o_s[...]
    inv_n = 1.0 / RWKV_HEAD_DIM
    mean = head_sum(o) * inv_n
    cen = o - mean
    var = head_sum(cen * cen) * inv_n
    o = cen * lax.rsqrt(var + RWKV_GN_EPS) * gn_w + gn_b
    bonus = head_sum(r_s[...] * k_s[...] * r_k)
    o = o + bonus * v_s[...]
    o_ref[...] = (o * g_s[...]).astype(o_ref.dtype)


def _hgrn_kernel(x_ref, gain_ref, w_ref, pv_ref, tri_ref, ones_ref, sel_ref, bsel_ref,
                 o_ref, st_ref, h_s, q_s, k_s, v_s, lf_s, o_s, *, ts):
    W = HGRN_WIDTH
    nsub = CHUNK // SUB

    @pl.when(pl.program_id(1) == 0)
    def _():
        st_ref[...] = jnp.zeros(st_ref.shape, F32)

    h = _bf(_rmsnorm(x_ref[...], gain_ref[...]))
    h_s[...] = h
    log_lb = pv_ref[0:1, :]
    log1m_lb = pv_ref[1:2, :]
    gn_w = pv_ref[2:3, :]
    f_pre = _dot(h, w_ref[:, W:2 * W])
    q_pre = _dot(h, w_ref[:, 0:W])
    e = jnp.exp(-jnp.abs(f_pre))
    b = log1m_lb - (jnp.maximum(-f_pre, 0.0) + jnp.log(1.0 + e))
    mx = jnp.maximum(log_lb, b)
    log_f = mx + jnp.log(1.0 + jnp.exp(-jnp.abs(log_lb - b)))
    lf_s[...] = log_f
    k_s[...] = pv_ref[3:4, :] * jnp.where(f_pre > 0.0, e, 1.0) / (1.0 + e)
    v_s[...] = _dot(h, w_ref[:, 2 * W:3 * W])
    q_s[...] = _silu(q_pre)

    tri = tri_ref[...]
    ones_sq = ones_ref[...]
    sel = sel_ref[...]
    sub_shift = int(np.log2(SUB))
    rblk = _iota((CHUNK, CHUNK), 0) >> sub_shift
    cblk = _iota((CHUNK, CHUNK), 1) >> sub_shift
    off_mask = cblk < rblk
    rowblk = _iota((CHUNK, LANES), 0) >> sub_shift
    s_idx = _iota((SUB, LANES), 0)

    heads = range(HGRN_HEADS)
    lanes = [slice(LANES * hd, LANES * (hd + 1)) for hd in heads]
    causal = _iota((CHUNK, CHUNK), 1) <= _iota((CHUNK, CHUNK), 0)

    def fast_chunk(c, states):
        c0 = pl.multiple_of(c * CHUNK, CHUNK)
        sl = pl.ds(c0, CHUNK)
        cum = _dot_sel_left(tri, lf_s[sl, :], 3)
        q = q_s[sl, :]
        k = k_s[sl, :]
        v = v_s[sl, :]
        yield
        starts = [None] + [cum[SUB * i - 1:SUB * i, :] for i in range(1, nsub)]
        cs = jnp.zeros_like(cum)
        rowblk_w = _iota(cum.shape, 0) >> sub_shift
        for i in range(1, nsub):
            cs = jnp.where(rowblk_w == i, starts[i], cs)
        q_sub = q * jnp.exp(cum - cs)
        k_own = k * jnp.exp(cs - cum)
        k_prev = [None] + [k[0:SUB * i, :] * jnp.exp(jnp.minimum(starts[i] - cum[0:SUB * i, :], 0.0))
                           for i in range(1, nsub)]
        yield
        last = cum[CHUNK - 1:CHUNK, :]
        q_in = q * jnp.exp(cum)
        k_end = k * jnp.exp(last - cum)
        e_last = jnp.exp(last)
        before = list(states)
        for hd, ls in enumerate(lanes):
            states[hd] = before[hd] * e_last[:, ls] + _dot_tn(v[:, ls], k_end[:, ls])
        yield
        scores = []
        for n, ls in enumerate(lanes):
            rows = []
            for i in range(nsub):
                parts = [k_own[SUB * i:SUB * (i + 1), ls]]
                if i > 0:
                    parts = [k_prev[i][:, ls]] + parts
                if i + 1 < nsub:
                    parts.append(jnp.zeros((CHUNK - SUB * (i + 1), LANES), F32))
                rows.append(_dot_nt(q_sub[SUB * i:SUB * (i + 1), ls], jnp.concatenate(parts, axis=0)))
            scores.append(jnp.where(causal, jnp.concatenate(rows, axis=0), 0.0))
            if n % 4 == 3:
                yield
        outs = [_dot(a, v[:, ls]) + _dot_nt(q_in[:, ls], s) for a, ls, s in zip(scores, lanes, before)]
        yield
        for hd, ls in enumerate(lanes):
            o_s[sl, ls] = outs[hd]

    def fast_chunk_body(i, carry):
        states = [st_ref[hd] for hd in heads]
        _interleave(*[fast_chunk(HGRN_LOOP_CHUNKS * i + j, states) for j in range(HGRN_LOOP_CHUNKS)])
        for hd in heads:
            st_ref[hd] = states[hd]
        return carry

    def chunk_body(c, carry):
        c0 = pl.multiple_of(c * CHUNK, CHUNK)
        sl = pl.ds(c0, CHUNK)
        cum_all = _dot_sel_left(tri, lf_s[sl, :], 3)
        q_all = q_s[sl, :]
        k_all = k_s[sl, :]
        v_all = v_s[sl, :]
        for hd in range(HGRN_HEADS):
            ls = slice(LANES * hd, LANES * (hd + 1))
            cum = cum_all[:, ls]
            q = q_all[:, ls]
            k = k_all[:, ls]
            v = v_all[:, ls]
            vb = _bf(v)
            last = cum[CHUNK - 1:CHUNK, :]
            starts = [None] + [cum[SUB * i - 1:SUB * i, :] for i in range(1, nsub)]
            cs = jnp.zeros_like(cum)
            for i in range(1, nsub):
                cs = jnp.where(rowblk == i, starts[i], cs)
            q_sub = q * jnp.exp(cum - cs)
            rows = [jnp.zeros((SUB, CHUNK), F32)]
            for i in range(1, nsub):
                k_i = k * jnp.exp(jnp.minimum(starts[i] - cum, 0.0))
                rows.append(_dot_nt(q_sub[SUB * i:SUB * (i + 1), :], k_i))
            a_off = jnp.where(off_mask, jnp.concatenate(rows, axis=0), 0.0)
            state = st_ref[hd]
            o = _dot(a_off, vb) + _dot_nt(q * jnp.exp(cum), state)
            diag = []
            for j in range(nsub):
                rs = slice(SUB * j, SUB * (j + 1))
                cum_b = cum[rs, :]
                q_b = q[rs, :]
                k_b = k[rs, :]
                pieces = []
                for t in range(SUB):
                    expo = jnp.where(s_idx <= t, cum_b[t:t + 1, :] - cum_b, -1e30)
                    pieces.append(jnp.exp(expo) * k_b * q_b[t:t + 1, :])
                pmat = jnp.concatenate(pieces, axis=0)
                score = _dot(pmat, ones_sq)
                wv = score * jnp.concatenate([v[rs, :]] * SUB, axis=0)
                diag.append(_dot(sel, wv))
            o = o + jnp.concatenate(diag, axis=0)
            o_s[sl, ls] = o
            k_end = k * jnp.exp(last - cum)
            st_ref[hd] = state * jnp.exp(last) + _dot_tn(vb, k_end)
        return carry

    sub_sums = _dot_sel_left(bsel_ref[...], log_f, 3)
    fast_ok = jnp.min(sub_sums) >= -HGRN_FAST_LOG_RANGE

    @pl.when(fast_ok)
    def _():
        lax.fori_loop(0, ts // (CHUNK * HGRN_LOOP_CHUNKS), fast_chunk_body, 0)

    @pl.when(jnp.logical_not(fast_ok))
    def _():
        lax.fori_loop(0, ts // CHUNK, chunk_body, 0)

    gate = _sigmoid(_dot(h_s[...], w_ref[:, 3 * W:]))
    for hd in range(HGRN_HEADS):
        ls = slice(LANES * hd, LANES * (hd + 1))
        o = o_s[:, ls]
        o = o * lax.rsqrt(jnp.mean(o * o, axis=-1, keepdims=True) + NORM_EPS) * gn_w[:, ls]
        o_ref[:, ls] = (o * gate[:, ls]).astype(o_ref.dtype)


def _ssm_kernel(x_ref, gain_ref, w_ref, cw_ref, cb_ref, hv_ref, wv_ref, tri_ref,
                o_ref, xb_ref, st_ref, h_s, xs_s, ec_s, xd_s, b_s, c_s, y_s, *, ts):
    W = SSM_WIDTH
    gw = SSM_GROUP_WIDTH
    pairs = W // LANES

    @pl.when(pl.program_id(1) == 0)
    def _():
        xb_ref[0:SUBLANES, :] = jnp.zeros((SUBLANES, SSM_CONV_DIM), F32)
        st_ref[...] = jnp.zeros(st_ref.shape, F32)

    h = _bf(_rmsnorm(x_ref[...], gain_ref[...]))
    h_s[...] = h

    dt_bias = hv_ref[0:1, :]
    neg_a = hv_ref[1:2, :]
    dt = _softplus(_dot(h, w_ref[:, W + SSM_CONV_DIM:]) + dt_bias)
    log_a = dt * neg_a
    tri = tri_ref[...]
    cum = jnp.concatenate([_dot_sel_left(tri, log_a[CHUNK * c:CHUNK * (c + 1), :], 3)
                           for c in range(ts // CHUNK)], axis=0)

    first_head_rows = _iota((ts, LANES), 1) < SSM_HEAD_DIM
    heads_per_group = gw // SSM_HEAD_DIM

    def expand_heads(v, g):
        tiles = []
        for j in range(gw // LANES):
            h0 = heads_per_group * g + 2 * j
            a = jnp.broadcast_to(v[:, h0:h0 + 1], (ts, LANES))
            b = jnp.broadcast_to(v[:, h0 + 1:h0 + 2], (ts, LANES))
            tiles.append(jnp.where(first_head_rows, a, b))
        return jnp.concatenate(tiles, axis=1)

    def project(lo, hi):
        xb_ref[SUBLANES:SUBLANES + ts, lo:hi] = _dot(h, w_ref[:, W + lo:W + hi])

    def conv_silu(lo, hi):
        full = xb_ref[0:ts + SUBLANES, lo:hi]
        conv = cb_ref[:, lo:hi] + full[SUBLANES:, :] * cw_ref[SSM_CONV_WIDTH - 1:SSM_CONV_WIDTH, lo:hi]
        for k in range(1, SSM_CONV_WIDTH):
            j = SSM_CONV_WIDTH - 1 - k
            conv = conv + pltpu.roll(full, k, 0)[SUBLANES:, :] * cw_ref[j:j + 1, lo:hi]
        xb_ref[0:SUBLANES, lo:hi] = xb_ref[ts:ts + SUBLANES, lo:hi]
        return _silu(conv)

    def finish(idx):
        lo, hi = slabs[idx]
        act = conv_silu(lo, hi)
        if idx < SSM_GROUPS:
            xs_s[:, lo:hi] = act
            xd_s[:, lo:hi] = act * expand_heads(dt, idx)
            ec_s[:, lo:hi] = expand_heads(cum, idx)
        elif idx == SSM_GROUPS:
            b_s[...] = act
        else:
            c_s[...] = act

    slabs = [(gw * g, gw * (g + 1)) for g in range(SSM_GROUPS)] + [(W, W + SSM_BC), (W + SSM_BC, W + 2 * SSM_BC)]
    project(*slabs[0])
    for idx in range(len(slabs)):
        if idx + 1 < len(slabs):
            project(*slabs[idx + 1])
        finish(idx)

    t_idx = _iota((CHUNK, W), 0)
    s_idx = _iota((CHUNK, W), 1) & (CHUNK - 1)
    causal = s_idx <= t_idx
    on_diag = s_idx == t_idx
    first_head = (_iota((CHUNK, LANES), 1) < SSM_HEAD_DIM)

    groups = range(SSM_GROUPS)
    glanes = [slice(gw * g, gw * (g + 1)) for g in groups]
    per_group = gw // LANES

    def chunk(c, states):
        c0 = pl.multiple_of(c * CHUNK, CHUNK)
        sl = pl.ds(c0, CHUNK)
        ec = ec_s[sl, :]
        xd = xd_s[sl, :]
        bm = b_s[sl, :]
        cm = c_s[sl, :]
        last = ec[CHUNK - 1:CHUNK, :]
        by_src = jnp.sum(jnp.where(on_diag, ec, 0.0), axis=0, keepdims=True)
        decay = jnp.exp(jnp.where(causal, ec - by_src, -1e30))
        yield
        xw = xd * jnp.exp(last - ec)
        e_in = jnp.exp(ec)
        e_last = jnp.exp(last)
        c_g = [_bf(cm[:, SSM_STATE * g:SSM_STATE * (g + 1)]) for g in groups]
        b_g = [_bf(bm[:, SSM_STATE * g:SSM_STATE * (g + 1)]) for g in groups]
        cb2 = [_dot_nt(c, jnp.concatenate([b, b], axis=0)) for c, b in zip(c_g, b_g)]
        yield
        before = list(states)
        for g, gl in enumerate(glanes):
            states[g] = before[g] * e_last[:, gl] + _dot_tn(b_g[g], xw[:, gl])
        y_in = [_dot(c, s) for c, s in zip(c_g, before)]
        yield
        ys = []
        for p in range(pairs):
            ls = slice(LANES * p, LANES * (p + 1))
            m = decay[:, ls] * cb2[(LANES * p) // gw]
            xp = xd[:, ls]
            x2 = jnp.concatenate([jnp.where(first_head, xp, 0.0), jnp.where(first_head, 0.0, xp)], axis=0)
            ys.append(_dot(m, x2))
            if p % 4 == 3:
                yield
        for g, gl in enumerate(glanes):
            y_s[sl, gl] = (y_in[g] * e_in[:, gl]
                           + jnp.concatenate(ys[per_group * g:per_group * (g + 1)], axis=1))

    def chunk_body(i, carry):
        states = [st_ref[g] for g in groups]
        _interleave(*[chunk(LOOP_CHUNKS * i + j, states) for j in range(LOOP_CHUNKS)])
        for g in groups:
            st_ref[g] = states[g]
        return carry

    lax.fori_loop(0, ts // (CHUNK * LOOP_CHUNKS), chunk_body, 0)

    d_skip = wv_ref[0:1, :]
    gn_w = wv_ref[1:2, :]
    h = h_s[...]
    for g in range(SSM_GROUPS):
        gl = slice(gw * g, gw * (g + 1))
        z = _dot(h, w_ref[:, gl])
        yg = (y_s[:, gl] + d_skip[:, gl] * xs_s[:, gl]) * _silu(z)
        yg = yg * lax.rsqrt(jnp.mean(yg * yg, axis=-1, keepdims=True) + NORM_EPS)
        o_ref[:, gl] = (yg * gn_w[:, gl]).astype(o_ref.dtype)


def _merge_kernel(x_ref, ya_ref, yb_ref, yc_ref, gmix_ref, wg_ref, wb_ref, wout_ref,
                  gffn_ref, wfi_ref, wfo_ref, gfin_ref, o_ref, *, final_norm):
    x = x_ref[...]
    h = _rmsnorm(x, gmix_ref[...])
    gates = _sigmoid(_dot(h, wg_ref[...]))
    rows_b = RWKV_WIDTH + HGRN_WIDTH
    merged = (gates[:, 0:D_MODEL] * jnp.dot(ya_ref[...], wb_ref[0:RWKV_WIDTH, :], preferred_element_type=F32)
              + gates[:, D_MODEL:2 * D_MODEL] * jnp.dot(yb_ref[...], wb_ref[RWKV_WIDTH:rows_b, :],
                                                        preferred_element_type=F32)
              + gates[:, 2 * D_MODEL:] * jnp.dot(yc_ref[...], wb_ref[rows_b:, :], preferred_element_type=F32))
    x = x + _dot(merged, wout_ref[...])
    h = _rmsnorm(x, gffn_ref[...])
    gu = _dot(h, wfi_ref[...])
    act = _silu(gu[:, 0:FFN_HIDDEN]) * gu[:, FFN_HIDDEN:]
    x = x + _dot(act, wfo_ref[...])
    if final_norm:
        x = _rmsnorm(x, gfin_ref[...])
    o_ref[...] = x


class _PerLayer:
    def __init__(self, value):
        self.value = value


def _const_spec(const, layer):
    if isinstance(const, _PerLayer):
        shape = const.value.shape[1:]
        return pl.BlockSpec((None,) + shape, lambda b, s, _l=layer, _nd=len(shape): (_l,) + (0,) * _nd,
                            pipeline_mode=pl.Buffered(1))
    return pl.BlockSpec(const.shape, lambda b, s, _nd=const.ndim: (0,) * _nd, pipeline_mode=pl.Buffered(1))


def _const_value(const):
    return const.value if isinstance(const, _PerLayer) else const


def _per_layer(fn, *stacked_args):
    return [_PerLayer(v) for v in jax.vmap(fn)(*stacked_args)]


def _seq_spec(ts, width):
    return pl.BlockSpec((None, ts, width), lambda b, s: (b, s, 0))


def _params():
    return pltpu.CompilerParams(dimension_semantics=("arbitrary", "arbitrary"),
                                vmem_limit_bytes=VMEM_LIMIT_BYTES)


def _run_mixer(kernel_fn, x, layer, consts, out_width, scratch, ts):
    bsz, seq, _ = x.shape
    return pl.pallas_call(
        functools.partial(kernel_fn, ts=ts),
        grid=(bsz, seq // ts),
        in_specs=[_seq_spec(ts, D_MODEL)] + [_const_spec(c, layer) for c in consts],
        out_specs=_seq_spec(ts, out_width),
        out_shape=jax.ShapeDtypeStruct((bsz, seq, out_width), BF16),
        scratch_shapes=scratch,
        compiler_params=_params(),
    )(x, *[_const_value(c) for c in consts])


def _row(v):
    return v.astype(F32).reshape(1, -1)


def _pad_cols(m, width):
    return jnp.pad(m, ((0, 0), (0, width - m.shape[1])))


def _rows8(rows, width):
    m = jnp.concatenate([_row(r) for r in rows], axis=0)
    return jnp.pad(m, ((0, SUBLANES - m.shape[0]), (0, width - m.shape[1])))


def _head_sum_matrix(width, head_dim):
    idx = np.arange(width) // head_dim
    return jnp.asarray(idx[:, None] == idx[None, :], dtype=BF16)


def _tri_ones(n):
    return jnp.asarray(np.tril(np.ones((n, n), np.float32)), dtype=BF16)


def _rwkv_layer_consts(gain, w_in, mu, w0, w_up, a0, a_up, g_up, k_k, k_a, r_k, gn_w, gn_b):
    W = RWKV_WIDTH
    dr, ir, gr = RWKV_DECAY_RANK, RWKV_ICL_RANK, RWKV_GATE_RANK
    w = _pad_cols(w_in, RWKV_U_COLS).astype(BF16)
    mu_p = _pad_cols(_row(mu), RWKV_U_COLS)
    wlr = jnp.zeros((LANES, 2 * W), F32).at[0:dr, 0:W].set(w_up).at[dr:dr + ir, W:].set(a_up).astype(BF16)
    blr = jnp.concatenate([_row(w0), _row(a0)], axis=1)
    gup = jnp.pad(g_up, ((0, RWKV_GATE_PAD - gr), (0, 0))).astype(BF16)
    pv = _rows8([k_k, k_a, r_k.reshape(-1), gn_w, gn_b], W)
    return _row(gain), w, mu_p, wlr, blr, gup, pv


def _rwkv_branch(x, layer, layer_consts, ts):
    W = RWKV_WIDTH
    consts = layer_consts + [_head_sum_matrix(MXU_TILE, RWKV_HEAD_DIM), _tri_ones(CHUNK)]
    scratch = ([pltpu.VMEM((ts + SUBLANES, RWKV_U_COLS), F32),
                pltpu.VMEM((W // RWKV_GROUP_LANES, RWKV_GROUP_LANES, RWKV_GROUP_LANES), F32)]
               + [pltpu.VMEM((ts, W), F32) for _ in range(8)])
    return _run_mixer(_rwkv_kernel, x, layer, consts, W, scratch, ts)


def _hgrn_layer_consts(gain, w_in, lb, gn_w):
    lb = lb.astype(F32)
    pv = _rows8([jnp.log(lb), jnp.log1p(-lb), gn_w, 1.0 - lb], HGRN_WIDTH)
    return _row(gain), w_in.astype(BF16), pv


def _hgrn_branch(x, layer, layer_consts, ts):
    W = HGRN_WIDTH
    t_i = np.arange(SUB)[:, None]
    pair = np.arange(SUB * SUB)[None, :]
    sel = jnp.asarray((pair // SUB == t_i) & (pair % SUB <= t_i), dtype=BF16)
    bsel = jnp.asarray(np.arange(ts // SUB)[:, None] == (np.arange(ts) // SUB)[None, :], dtype=BF16)
    consts = layer_consts + [_tri_ones(CHUNK), jnp.ones((LANES, LANES), BF16), sel, bsel]
    scratch = ([pltpu.VMEM((HGRN_HEADS, HGRN_HEAD_DIM, HGRN_HEAD_DIM), F32), pltpu.VMEM((ts, D_MODEL), BF16)]
               + [pltpu.VMEM((ts, W), F32) for _ in range(5)])
    return _run_mixer(_hgrn_kernel, x, layer, consts, W, scratch, ts)


def _ssm_layer_consts(gain, w_in, conv_w, conv_b, dt_bias, a_log, d_skip_lanes, gn_w):
    w = _pad_cols(w_in, SSM_U_COLS).astype(BF16)
    cw = jnp.pad(conv_w.astype(F32).T, ((0, SUBLANES - SSM_CONV_WIDTH), (0, 0)))
    hv = _rows8([dt_bias, -jnp.exp(a_log.astype(F32))], LANES)
    wv = _rows8([d_skip_lanes, gn_w], SSM_WIDTH)
    return _row(gain), w, cw, _row(conv_b), hv, wv


def _ssm_branch(x, layer, layer_consts, ts):
    W = SSM_WIDTH
    consts = layer_consts + [_tri_ones(CHUNK)]
    scratch = [pltpu.VMEM((ts + SUBLANES, SSM_CONV_DIM), F32),
               pltpu.VMEM((SSM_GROUPS, SSM_STATE, SSM_GROUP_WIDTH), F32),
               pltpu.VMEM((ts, D_MODEL), BF16), pltpu.VMEM((ts, W), F32),
               pltpu.VMEM((ts, W), F32), pltpu.VMEM((ts, W), F32),
               pltpu.VMEM((ts, SSM_BC), F32), pltpu.VMEM((ts, SSM_BC), F32),
               pltpu.VMEM((ts, W), F32)]
    return _run_mixer(_ssm_kernel, x, layer, consts, W, scratch, ts)


def _merge_layer_consts(gain_mix, w_gate, w_branch, w_out, gain_ffn, w_ffn_in, w_ffn_out):
    return (_row(gain_mix), w_gate.astype(BF16), w_branch.astype(BF16), w_out.astype(BF16), _row(gain_ffn),
            w_ffn_in.astype(BF16), w_ffn_out.astype(BF16))


def _merge_ffn(x, ya, yb, yc, layer, layer_consts, gain_final, final_norm, ts):
    bsz, seq, _ = x.shape
    consts = layer_consts + [_row(gain_final)]
    seqs = [x, ya, yb, yc]
    return pl.pallas_call(
        functools.partial(_merge_kernel, final_norm=final_norm),
        grid=(bsz, seq // ts),
        in_specs=[_seq_spec(ts, a.shape[-1]) for a in seqs] + [_const_spec(c, layer) for c in consts],
        out_specs=_seq_spec(ts, D_MODEL),
        out_shape=jax.ShapeDtypeStruct((bsz, seq, D_MODEL), F32),
        compiler_params=_params(),
    )(*seqs, *[_const_value(c) for c in consts])


def kernel(x, norm_mix, w_in, rwkv_mu, rwkv_w0, rwkv_w_up, rwkv_a0, rwkv_a_up, rwkv_g_up, rwkv_k_k, rwkv_k_a,
           rwkv_r_k, rwkv_gn_w, rwkv_gn_b, hgrn_lb_logits, hgrn_gn_w, ssm_conv_w, ssm_conv_b, ssm_dt_bias,
           ssm_a_log, ssm_d, ssm_gn_w, w_branch, w_out, norm_ffn, w_ffn_in, w_ffn_out, norm_final):
    bsz, seq, d = x.shape
    assert d == D_MODEL
    depth = w_in.shape[0]
    ts = min(SEQ_BLOCK, seq)
    tm = min(MERGE_BLOCK, seq)
    assert seq % ts == 0 and seq % tm == 0
    assert all(ts % (CHUNK * n) == 0 for n in (LOOP_CHUNKS, HGRN_LOOP_CHUNKS, RWKV_LOOP_CHUNKS))

    rwkv_cols = 3 * RWKV_WIDTH + RWKV_DECAY_RANK + RWKV_ICL_RANK + RWKV_GATE_RANK
    off_hgrn = rwkv_cols
    off_ssm = off_hgrn + 4 * HGRN_WIDTH
    off_gate = off_ssm + SSM_WIDTH + SSM_CONV_DIM + SSM_HEADS

    cs = jnp.cumsum(jax.nn.softmax(hgrn_lb_logits.astype(F32), axis=0), axis=0)
    lbs = cs - cs[:1]

    w_in_bf = w_in.astype(BF16)
    rwkv_consts = _per_layer(_rwkv_layer_consts, norm_mix, w_in_bf[:, :, :off_hgrn], rwkv_mu, rwkv_w0, rwkv_w_up,
                             rwkv_a0, rwkv_a_up, rwkv_g_up, rwkv_k_k, rwkv_k_a, rwkv_r_k, rwkv_gn_w, rwkv_gn_b)
    hgrn_consts = _per_layer(_hgrn_layer_consts, norm_mix, w_in_bf[:, :, off_hgrn:off_ssm], lbs, hgrn_gn_w)
    ssm_consts = _per_layer(_ssm_layer_consts, norm_mix, w_in_bf[:, :, off_ssm:off_gate], ssm_conv_w, ssm_conv_b,
                            ssm_dt_bias, ssm_a_log, jnp.repeat(ssm_d.astype(F32), SSM_HEAD_DIM, axis=1), ssm_gn_w)
    merge_consts = _per_layer(_merge_layer_consts, norm_mix, w_in_bf[:, :, off_gate:], w_branch, w_out, norm_ffn,
                              w_ffn_in, w_ffn_out)

    x = x.astype(F32)
    for l in range(depth):
        ya = _rwkv_branch(x, l, rwkv_consts, ts)
        yb = _hgrn_branch(x, l, hgrn_consts, ts)
        yc = _ssm_branch(x, l, ssm_consts, ts)
        x = _merge_ffn(x, ya, yb, yc, l, merge_consts, norm_final, l == depth - 1, tm)
    return x
```

```python
import functools

import numpy as np
import jax
import jax.numpy as jnp
from jax import lax
from jax.experimental import pallas as pl
from jax.experimental.pallas import tpu as pltpu

F32 = jnp.float32
BF16 = jnp.bfloat16

D_MODEL = 1024
CHUNK = 64
LOOP_CHUNKS = 2
HGRN_LOOP_CHUNKS = 8
SUB = 16
HGRN_FAST_LOG_RANGE = 60.0
NORM_EPS = 1e-5
LANES = 128
SUBLANES = 8
MXU_TILE = 256
VMEM_LIMIT_BYTES = 56 * 1024 * 1024

RWKV_HEAD_DIM = 64
RWKV_WIDTH = D_MODEL
RWKV_DECAY_RANK = 64
RWKV_ICL_RANK = 64
RWKV_GATE_RANK = 160
RWKV_GATE_PAD = 256
RWKV_GN_EPS = 64e-5
RWKV_LOOP_CHUNKS = 8
RWKV_GROUP_LANES = MXU_TILE
RWKV_U_COLS = 3 * RWKV_WIDTH + LANES + RWKV_GATE_PAD

HGRN_HEAD_DIM = 128
HGRN_WIDTH = D_MODEL
HGRN_HEADS = HGRN_WIDTH // HGRN_HEAD_DIM

SSM_WIDTH = 2 * D_MODEL
SSM_HEAD_DIM = 64
SSM_HEADS = SSM_WIDTH // SSM_HEAD_DIM
SSM_GROUPS = 4
SSM_STATE = 128
SSM_CONV_WIDTH = 4
SSM_BC = SSM_GROUPS * SSM_STATE
SSM_CONV_DIM = SSM_WIDTH + 2 * SSM_BC
SSM_GROUP_WIDTH = SSM_WIDTH // SSM_GROUPS
SSM_U_COLS = SSM_WIDTH + SSM_CONV_DIM + LANES

FFN_HIDDEN = ((8 * D_MODEL + 3 * 256 - 1) // (3 * 256)) * 256

SEQ_BLOCK = 512
MERGE_BLOCK = 512
FFN_SPLIT = 6 * MXU_TILE


def _bf(x):
    return x if x.dtype == BF16 else x.astype(BF16)


def _dot(a, b):
    return jnp.dot(_bf(a), _bf(b), preferred_element_type=F32)


def _dot_nt(a, b):
    return lax.dot_general(_bf(a), _bf(b), (((1,), (1,)), ((), ())), preferred_element_type=F32)


def _dot_tn(a, b):
    return lax.dot_general(_bf(a), _bf(b), (((0,), (0,)), ((), ())), preferred_element_type=F32)


def _split(x, n):
    parts = []
    rest = x
    for i in range(n):
        p = rest.astype(BF16)
        parts.append(p)
        if i + 1 < n:
            rest = rest - p.astype(F32)
    return parts


def _dot_sel_left(sel, x, passes=3):
    out = None
    for p in _split(x, passes):
        t = jnp.dot(sel, p, preferred_element_type=F32)
        out = t if out is None else out + t
    return out


def _rmsnorm(x, gain):
    ms = jnp.mean(x * x, axis=-1, keepdims=True)
    return x * lax.rsqrt(ms + NORM_EPS) * gain


def _sigmoid(x):
    return 1.0 / (1.0 + jnp.exp(-x))


def _silu(x):
    return x * _sigmoid(x)


def _softplus(x):
    return jnp.maximum(x, 0.0) + jnp.log(1.0 + jnp.exp(-jnp.abs(x)))


def _iota(shape, axis):
    return lax.broadcasted_iota(jnp.int32, shape, axis)


def _interleave(*programs):
    live = list(programs)
    while live:
        for gen in list(live):
            try:
                next(gen)
            except StopIteration:
                live.remove(gen)


def _rwkv_kernel(x_ref, gain_ref, w_ref, mu_ref, wlr_ref, blr_ref, gup_ref, pv_ref, hsum_ref, tri_ref,
                 o_ref, ush_ref, st_ref, r_s, lw_s, k_s, v_s, ka_s, kb_s, o_s, g_s, *, ts):
    W = RWKV_WIDTH

    @pl.when(pl.program_id(1) == 0)
    def _():
        ush_ref[0:SUBLANES, :] = jnp.zeros((SUBLANES, RWKV_U_COLS), F32)
        st_ref[...] = jnp.zeros(st_ref.shape, F32)

    h = _bf(_rmsnorm(x_ref[...], gain_ref[...]))

    def project(lo, hi):
        ush_ref[SUBLANES:SUBLANES + ts, lo:hi] = _dot(h, w_ref[:, lo:hi])

    def token_shift(lo, hi):
        full = ush_ref[0:ts + SUBLANES, lo:hi]
        u = full[SUBLANES:, :]
        prev = pltpu.roll(full, 1, 0)[SUBLANES:, :]
        ush_ref[0:SUBLANES, lo:hi] = u[ts - SUBLANES:ts, :]
        return u + (prev - u) * mu_ref[:, lo:hi]

    project(3 * W, RWKV_U_COLS)
    project(W, 2 * W)
    low_rank = token_shift(3 * W, RWKV_U_COLS)
    xwa = low_rank[:, 0:LANES]
    xg = low_rank[:, LANES:]
    lane = _iota(xwa.shape, 1)
    lr_in = jnp.where(lane < RWKV_DECAY_RANK, jnp.tanh(xwa), xwa)
    lr = _dot(lr_in, wlr_ref[...]) + blr_ref[...]
    log_w = -jnp.exp(-_softplus(-lr[:, 0:W]) - 0.5)
    a = _sigmoid(lr[:, W:])
    g_s[...] = _dot(_sigmoid(xg), gup_ref[...])
    project(0, W)
    k = token_shift(W, 2 * W)

    k_k = pv_ref[0:1, :]
    k_a = pv_ref[1:2, :]
    r_k = pv_ref[2:3, :]
    gn_w = pv_ref[3:4, :]
    gn_b = pv_ref[4:5, :]
    hsum = hsum_ref[...]

    def head_sum(t):
        return jnp.concatenate([_dot(t[:, MXU_TILE * j:MXU_TILE * (j + 1)], hsum)
                                for j in range(W // MXU_TILE)], axis=1)

    kk = k * k_k
    ss = head_sum(kk * kk)
    kk = kk * lax.rsqrt(jnp.maximum(ss, 1e-24))
    kmod = k * (1.0 + (a - 1.0) * k_a)

    lw_s[...] = log_w
    k_s[...] = kmod
    ka_s[...] = -kk
    kb_s[...] = kk * a
    project(2 * W, 3 * W)
    r_s[...] = token_shift(0, W)
    v_s[...] = token_shift(2 * W, 3 * W)

    tri = tri_ref[...]
    gl = RWKV_GROUP_LANES
    n_groups = W // gl
    heads_per_group = gl // RWKV_HEAD_DIM
    row_w = _iota((CHUNK, gl), 0)
    col_w = _iota((CHUNK, gl), 1) & (RWKV_HEAD_DIM - 1)
    strict = row_w > col_w
    incl = row_w >= col_w
    eye_w = jnp.where(row_w == col_w, 1.0, 0.0)
    first_head = _iota((CHUNK, LANES), 1) < RWKV_HEAD_DIM
    head_shift = int(np.log2(RWKV_HEAD_DIM))
    same_head = (_iota((gl, gl), 0) >> head_shift) == (_iota((gl, gl), 1) >> head_shift)
    zeros_tile = jnp.zeros((CHUNK, LANES), F32)

    def blockdiag(t):
        blocks = []
        for hd in range(heads_per_group):
            tile = t[:, LANES * (hd // 2):LANES * (hd // 2 + 1)]
            kept = jnp.where(first_head, tile, 0.0) if hd % 2 == 0 else jnp.where(first_head, 0.0, tile)
            tiles = [kept if j == hd // 2 else zeros_tile for j in range(gl // LANES)]
            blocks.append(jnp.concatenate(tiles, axis=1))
        return jnp.concatenate(blocks, axis=0)

    lanes = [slice(gl * g, gl * (g + 1)) for g in range(n_groups)]

    def state_free(c, ctx):
        c0 = pl.multiple_of(c * CHUNK, CHUNK)
        sl = pl.ds(c0, CHUNK)
        lw = lw_s[sl, :]
        cum = _dot_sel_left(tri, lw, 2)
        e_last = jnp.exp(cum[CHUNK - 1:CHUNK, :])
        e_neg = jnp.exp(-cum)
        rt = r_s[sl, :] * jnp.exp(cum)
        at = ka_s[sl, :] * jnp.exp(cum - lw)
        bt = kb_s[sl, :] * e_neg
        kt = k_s[sl, :] * e_neg
        vv = v_s[sl, :]
        ar = [jnp.concatenate([at[:, ls], rt[:, ls]], axis=0) for ls in lanes]
        bk_bd = [jnp.concatenate([blockdiag(bt[:, ls]), blockdiag(kt[:, ls])], axis=0) for ls in lanes]
        v_bd = [blockdiag(vv[:, ls]) for ls in lanes]
        bk_end = [jnp.concatenate([bt[:, ls], kt[:, ls]], axis=0) * e_last[:, ls] for ls in lanes]
        yield
        gram = [_dot_nt(a, b) for a, b in zip(ar, bk_bd)]
        n_ab = [jnp.where(strict, g[0:CHUNK, 0:gl], 0.0) for g in gram]
        n_ak = [jnp.where(strict, g[0:CHUNK, gl:], 0.0) for g in gram]
        n_rbk = [jnp.concatenate([jnp.where(incl, g[CHUNK:, 0:gl], 0.0), jnp.where(incl, g[CHUNK:, gl:], 0.0)], axis=1)
                 for g in gram]
        yield
        ak_v = [_dot(n, v) for n, v in zip(n_ak, v_bd)]
        prod = [eye_w + n for n in n_ab]
        npow = [_dot(n, blockdiag(n)) for n in n_ab]
        for _ in range(int(np.log2(CHUNK)) - 2):
            yield
            both = [_dot(jnp.concatenate([nk, pr], axis=0), blockdiag(nk)) for nk, pr in zip(npow, prod)]
            prod = [pr + b[CHUNK:] for pr, b in zip(prod, both)]
            npow = [b[0:CHUNK] for b in both]
        yield
        inv = [pr + _dot(pr, blockdiag(nk)) for pr, nk in zip(prod, npow)]
        ctx.update(sl=sl, e_last=e_last, ar=ar, vv=vv, v_bd=v_bd, bk_end=bk_end, ak_v=ak_v, n_rbk=n_rbk, inv=inv)

    def state_bound(ctx, states):
        sl, e_last = ctx["sl"], ctx["e_last"]
        from_state = [_dot_nt(a, s) for a, s in zip(ctx["ar"], states)]
        yield
        u = [_dot(t, blockdiag(f[0:CHUNK] + r)) for t, f, r in zip(ctx["inv"], from_state, ctx["ak_v"])]
        yield
        for g, ls in enumerate(lanes):
            uv = jnp.concatenate([u[g], ctx["vv"][:, ls]], axis=0)
            states[g] = states[g] * e_last[:, ls] + jnp.where(same_head, _dot_tn(uv, ctx["bk_end"][g]), 0.0)
        yield
        for g, ls in enumerate(lanes):
            o_s[sl, ls] = from_state[g][CHUNK:] + _dot(ctx["n_rbk"][g],
                                                       jnp.concatenate([blockdiag(u[g]), ctx["v_bd"][g]], axis=0))

    def chained(*programs):
        for program in programs:
            yield from program

    def chunk_body(i, carry):
        n = RWKV_LOOP_CHUNKS
        states = [st_ref[g] for g in range(n_groups)]
        ctxs = [dict() for _ in range(n)]
        pending = []
        for j in range(0, n, 2):
            frees = [state_free(n * i + j + d, ctxs[j + d]) for d in range(2)]
            _interleave(*frees, *([chained(*pending)] if pending else []))
            pending = [state_bound(ctxs[j + d], states) for d in range(2)]
        _interleave(chained(*pending))
        for g in range(n_groups):
            st_ref[g] = states[g]
        return carry

    lax.fori_loop(0, ts // (CHUNK * RWKV_LOOP_CHUNKS), chunk_body, 0)

    o = o_s[...]
    inv_n = 1.0 / RWKV_HEAD_DIM
    mean = head_sum(o) * inv_n
    cen = o - mean
    var = head_sum(cen * cen) * inv_n
    o = cen * lax.rsqrt(var + RWKV_GN_EPS) * gn_w + gn_b
    bonus = head_sum(r_s[...] * k_s[...] * r_k)
    o = o + bonus * v_s[...]
    o_ref[...] = (o * g_s[...]).astype(o_ref.dtype)


def _hgrn_kernel(x_ref, gain_ref, w_ref, pv_ref, tri_ref, ones_ref, sel_ref, bsel_ref,
                 o_ref, st_ref, h_s, q_s, k_s, v_s, lf_s, o_s, *, ts):
    W = HGRN_WIDTH
    nsub = CHUNK // SUB

    @pl.when(pl.program_id(1) == 0)
    def _():
        st_ref[...] = jnp.zeros(st_ref.shape, F32)

    h = _bf(_rmsnorm(x_ref[...], gain_ref[...]))
    h_s[...] = h
    log_lb = pv_ref[0:1, :]
    log1m_lb = pv_ref[1:2, :]
    gn_w = pv_ref[2:3, :]
    f_pre = _dot(h, w_ref[:, W:2 * W])
    q_pre = _dot(h, w_ref[:, 0:W])
    e = jnp.exp(-jnp.abs(f_pre))
    b = log1m_lb - (jnp.maximum(-f_pre, 0.0) + jnp.log(1.0 + e))
    mx = jnp.maximum(log_lb, b)
    log_f = mx + jnp.log(1.0 + jnp.exp(-jnp.abs(log_lb - b)))
    lf_s[...] = log_f
    k_s[...] = pv_ref[3:4, :] * jnp.where(f_pre > 0.0, e, 1.0) / (1.0 + e)
    v_s[...] = _dot(h, w_ref[:, 2 * W:3 * W])
    q_s[...] = _silu(q_pre)

    tri = tri_ref[...]
    ones_sq = ones_ref[...]
    sel = sel_ref[...]
    sub_shift = int(np.log2(SUB))
    rblk = _iota((CHUNK, CHUNK), 0) >> sub_shift
    cblk = _iota((CHUNK, CHUNK), 1) >> sub_shift
    off_mask = cblk < rblk
    rowblk = _iota((CHUNK, LANES), 0) >> sub_shift
    s_idx = _iota((SUB, LANES), 0)

    heads = range(HGRN_HEADS)
    lanes = [slice(LANES * hd, LANES * (hd + 1)) for hd in heads]
    causal = _iota((CHUNK, CHUNK), 1) <= _iota((CHUNK, CHUNK), 0)

    def fast_chunk(c, states):
        c0 = pl.multiple_of(c * CHUNK, CHUNK)
        sl = pl.ds(c0, CHUNK)
        cum = _dot_sel_left(tri, lf_s[sl, :], 3)
        q = q_s[sl, :]
        k = k_s[sl, :]
        v = v_s[sl, :]
        yield
        starts = [None] + [cum[SUB * i - 1:SUB * i, :] for i in range(1, nsub)]
        cs = jnp.zeros_like(cum)
        rowblk_w = _iota(cum.shape, 0) >> sub_shift
        for i in range(1, nsub):
            cs = jnp.where(rowblk_w == i, starts[i], cs)
        q_sub = q * jnp.exp(cum - cs)
        k_own = k * jnp.exp(cs - cum)
        k_prev = [None] + [k[0:SUB * i, :] * jnp.exp(jnp.minimum(starts[i] - cum[0:SUB * i, :], 0.0))
                           for i in range(1, nsub)]
        yield
        last = cum[CHUNK - 1:CHUNK, :]
        q_in = q * jnp.exp(cum)
        k_end = k * jnp.exp(last - cum)
        e_last = jnp.exp(last)
        before = list(states)
        for hd, ls in enumerate(lanes):
            states[hd] = before[hd] * e_last[:, ls] + _dot_tn(v[:, ls], k_end[:, ls])
        yield
        scores = []
        for n, ls in enumerate(lanes):
            rows = []
            for i in range(nsub):
                parts = [k_own[SUB * i:SUB * (i + 1), ls]]
                if i > 0:
                    parts = [k_prev[i][:, ls]] + parts
                if i + 1 < nsub:
                    parts.append(jnp.zeros((CHUNK - SUB * (i + 1), LANES), F32))
                rows.append(_dot_nt(q_sub[SUB * i:SUB * (i + 1), ls], jnp.concatenate(parts, axis=0)))
            scores.append(jnp.where(causal, jnp.concatenate(rows, axis=0), 0.0))
            if n % 4 == 3:
                yield
        outs = [_dot(a, v[:, ls]) + _dot_nt(q_in[:, ls], s) for a, ls, s in zip(scores, lanes, before)]
        yield
        for hd, ls in enumerate(lanes):
            o_s[sl, ls] = outs[hd]

    def fast_chunk_body(i, carry):
        states = [st_ref[hd] for hd in heads]
        _interleave(*[fast_chunk(HGRN_LOOP_CHUNKS * i + j, states) for j in range(HGRN_LOOP_CHUNKS)])
        for hd in heads:
            st_ref[hd] = states[hd]
        return carry

    def chunk_body(c, carry):
        c0 = pl.multiple_of(c * CHUNK, CHUNK)
        sl = pl.ds(c0, CHUNK)
        cum_all = _dot_sel_left(tri, lf_s[sl, :], 3)
        q_all = q_s[sl, :]
        k_all = k_s[sl, :]
        v_all = v_s[sl, :]
        for hd in range(HGRN_HEADS):
            ls = slice(LANES * hd, LANES * (hd + 1))
            cum = cum_all[:, ls]
            q = q_all[:, ls]
            k = k_all[:, ls]
            v = v_all[:, ls]
            vb = _bf(v)
            last = cum[CHUNK - 1:CHUNK, :]
            starts = [None] + [cum[SUB * i - 1:SUB * i, :] for i in range(1, nsub)]
            cs = jnp.zeros_like(cum)
            for i in range(1, nsub):
                cs = jnp.where(rowblk == i, starts[i], cs)
            q_sub = q * jnp.exp(cum - cs)
            rows = [jnp.zeros((SUB, CHUNK), F32)]
            for i in range(1, nsub):
                k_i = k * jnp.exp(jnp.minimum(starts[i] - cum, 0.0))
                rows.append(_dot_nt(q_sub[SUB * i:SUB * (i + 1), :], k_i))
            a_off = jnp.where(off_mask, jnp.concatenate(rows, axis=0), 0.0)
            state = st_ref[hd]
            o = _dot(a_off, vb) + _dot_nt(q * jnp.exp(cum), state)
            diag = []
            for j in range(nsub):
                rs = slice(SUB * j, SUB * (j + 1))
                cum_b = cum[rs, :]
                q_b = q[rs, :]
                k_b = k[rs, :]
                pieces = []
                for t in range(SUB):
                    expo = jnp.where(s_idx <= t, cum_b[t:t + 1, :] - cum_b, -1e30)
                    pieces.append(jnp.exp(expo) * k_b * q_b[t:t + 1, :])
                pmat = jnp.concatenate(pieces, axis=0)
                score = _dot(pmat, ones_sq)
                wv = score * jnp.concatenate([v[rs, :]] * SUB, axis=0)
                diag.append(_dot(sel, wv))
            o = o + jnp.concatenate(diag, axis=0)
            o_s[sl, ls] = o
            k_end = k * jnp.exp(last - cum)
            st_ref[hd] = state * jnp.exp(last) + _dot_tn(vb, k_end)
        return carry

    sub_sums = _dot_sel_left(bsel_ref[...], log_f, 3)
    fast_ok = jnp.min(sub_sums) >= -HGRN_FAST_LOG_RANGE

    @pl.when(fast_ok)
    def _():
        lax.fori_loop(0, ts // (CHUNK * HGRN_LOOP_CHUNKS), fast_chunk_body, 0)

    @pl.when(jnp.logical_not(fast_ok))
    def _():
        lax.fori_loop(0, ts // CHUNK, chunk_body, 0)

    gate = _sigmoid(_dot(h_s[...], w_ref[:, 3 * W:]))
    for hd in range(HGRN_HEADS):
        ls = slice(LANES * hd, LANES * (hd + 1))
        o = o_s[:, ls]
        o = o * lax.rsqrt(jnp.mean(o * o, axis=-1, keepdims=True) + NORM_EPS) * gn_w[:, ls]
        o_ref[:, ls] = (o * gate[:, ls]).astype(o_ref.dtype)


def _ssm_kernel(x_ref, gain_ref, w_ref, cw_ref, cb_ref, hv_ref, wv_ref, tri_ref,
                o_ref, xb_ref, st_ref, h_s, xs_s, ec_s, xd_s, b_s, c_s, y_s, *, ts):
    W = SSM_WIDTH
    gw = SSM_GROUP_WIDTH
    pairs = W // LANES

    @pl.when(pl.program_id(1) == 0)
    def _():
        xb_ref[0:SUBLANES, :] = jnp.zeros((SUBLANES, SSM_CONV_DIM), F32)
        st_ref[...] = jnp.zeros(st_ref.shape, F32)

    h = _bf(_rmsnorm(x_ref[...], gain_ref[...]))
    h_s[...] = h

    dt_bias = hv_ref[0:1, :]
    neg_a = hv_ref[1:2, :]
    dt = _softplus(_dot(h, w_ref[:, W + SSM_CONV_DIM:]) + dt_bias)
    log_a = dt * neg_a
    tri = tri_ref[...]
    cum = jnp.concatenate([_dot_sel_left(tri, log_a[CHUNK * c:CHUNK * (c + 1), :], 3)
                           for c in range(ts // CHUNK)], axis=0)

    first_head_rows = _iota((ts, LANES), 1) < SSM_HEAD_DIM
    heads_per_group = gw // SSM_HEAD_DIM

    def expand_heads(v, g):
        tiles = []
        for j in range(gw // LANES):
            h0 = heads_per_group * g + 2 * j
            a = jnp.broadcast_to(v[:, h0:h0 + 1], (ts, LANES))
            b = jnp.broadcast_to(v[:, h0 + 1:h0 + 2], (ts, LANES))
            tiles.append(jnp.where(first_head_rows, a, b))
        return jnp.concatenate(tiles, axis=1)

    def project(lo, hi):
        xb_ref[SUBLANES:SUBLANES + ts, lo:hi] = _dot(h, w_ref[:, W + lo:W + hi])

    def conv_silu(lo, hi):
        full = xb_ref[0:ts + SUBLANES, lo:hi]
        conv = cb_ref[:, lo:hi] + full[SUBLANES:, :] * cw_ref[SSM_CONV_WIDTH - 1:SSM_CONV_WIDTH, lo:hi]
        for k in range(1, SSM_CONV_WIDTH):
            j = SSM_CONV_WIDTH - 1 - k
            conv = conv + pltpu.roll(full, k, 0)[SUBLANES:, :] * cw_ref[j:j + 1, lo:hi]
        xb_ref[0:SUBLANES, lo:hi] = xb_ref[ts:ts + SUBLANES, lo:hi]
        return _silu(conv)

    def finish(idx):
        lo, hi = slabs[idx]
        act = conv_silu(lo, hi)
        if idx < SSM_GROUPS:
            xs_s[:, lo:hi] = act
            xd_s[:, lo:hi] = act * expand_heads(dt, idx)
            ec_s[:, lo:hi] = expand_heads(cum, idx)
        elif idx == SSM_GROUPS:
            b_s[...] = act
        else:
            c_s[...] = act

    slabs = [(gw * g, gw * (g + 1)) for g in range(SSM_GROUPS)] + [(W, W + SSM_BC), (W + SSM_BC, W + 2 * SSM_BC)]
    project(*slabs[0])
    for idx in range(len(slabs)):
        if idx + 1 < len(slabs):
            project(*slabs[idx + 1])
        finish(idx)

    t_idx = _iota((CHUNK, W), 0)
    s_idx = _iota((CHUNK, W), 1) & (CHUNK - 1)
    causal = s_idx <= t_idx
    on_diag = s_idx == t_idx
    first_head = (_iota((CHUNK, LANES), 1) < SSM_HEAD_DIM)

    groups = range(SSM_GROUPS)
    glanes = [slice(gw * g, gw * (g + 1)) for g in groups]
    per_group = gw // LANES

    def chunk(c, states):
        c0 = pl.multiple_of(c * CHUNK, CHUNK)
        sl = pl.ds(c0, CHUNK)
        ec = ec_s[sl, :]
        xd = xd_s[sl, :]
        bm = b_s[sl, :]
        cm = c_s[sl, :]
        last = ec[CHUNK - 1:CHUNK, :]
        by_src = jnp.sum(jnp.where(on_diag, ec, 0.0), axis=0, keepdims=True)
        decay = jnp.exp(jnp.where(causal, ec - by_src, -1e30))
        yield
        xw = xd * jnp.exp(last - ec)
        e_in = jnp.exp(ec)
        e_last = jnp.exp(last)
        c_g = [_bf(cm[:, SSM_STATE * g:SSM_STATE * (g + 1)]) for g in groups]
        b_g = [_bf(bm[:, SSM_STATE * g:SSM_STATE * (g + 1)]) for g in groups]
        cb2 = [_dot_nt(c, jnp.concatenate([b, b], axis=0)) for c, b in zip(c_g, b_g)]
        yield
        before = list(states)
        for g, gl in enumerate(glanes):
            states[g] = before[g] * e_last[:, gl] + _dot_tn(b_g[g], xw[:, gl])
        y_in = [_dot(c, s) for c, s in zip(c_g, before)]
        yield
        ys = []
        for p in range(pairs):
            ls = slice(LANES * p, LANES * (p + 1))
            m = decay[:, ls] * cb2[(LANES * p) // gw]
            xp = xd[:, ls]
            x2 = jnp.concatenate([jnp.where(first_head, xp, 0.0), jnp.where(first_head, 0.0, xp)], axis=0)
            ys.append(_dot(m, x2))
            if p % 4 == 3:
                yield
        for g, gl in enumerate(glanes):
            y_s[sl, gl] = (y_in[g] * e_in[:, gl]
                           + jnp.concatenate(ys[per_group * g:per_group * (g + 1)], axis=1))

    def chunk_body(i, carry):
        states = [st_ref[g] for g in groups]
        _interleave(*[chunk(LOOP_CHUNKS * i + j, states) for j in range(LOOP_CHUNKS)])
        for g in groups:
            st_ref[g] = states[g]
        return carry

    lax.fori_loop(0, ts // (CHUNK * LOOP_CHUNKS), chunk_body, 0)

    d_skip = wv_ref[0:1, :]
    gn_w = wv_ref[1:2, :]
    h = h_s[...]
    for g in range(SSM_GROUPS):
        gl = slice(gw * g, gw * (g + 1))
        z = _dot(h, w_ref[:, gl])
        yg = (y_s[:, gl] + d_skip[:, gl] * xs_s[:, gl]) * _silu(z)
        yg = yg * lax.rsqrt(jnp.mean(yg * yg, axis=-1, keepdims=True) + NORM_EPS)
        o_ref[:, gl] = (yg * gn_w[:, gl]).astype(o_ref.dtype)


def _merge_kernel(x_ref, ya_ref, yb_ref, yc_ref, gmix_ref, wg_ref, wb_ref, wout_ref,
                  gffn_ref, wfi_ref, wfo_ref, gfin_ref, o_ref, *, final_norm):
    x = x_ref[...]
    h = _bf(_rmsnorm(x, gmix_ref[...]))
    rows_b = RWKV_WIDTH + HGRN_WIDTH
    merged = (_sigmoid(_dot(h, wg_ref[:, 0:D_MODEL]))
              * jnp.dot(ya_ref[...], wb_ref[0:RWKV_WIDTH, :], preferred_element_type=F32))
    merged = merged + (_sigmoid(_dot(h, wg_ref[:, D_MODEL:2 * D_MODEL]))
                       * jnp.dot(yb_ref[...], wb_ref[RWKV_WIDTH:rows_b, :], preferred_element_type=F32))
    merged = merged + (_sigmoid(_dot(h, wg_ref[:, 2 * D_MODEL:]))
                       * jnp.dot(yc_ref[...], wb_ref[rows_b:, :], preferred_element_type=F32))
    x = x + _dot(merged, wout_ref[...])
    h = _bf(_rmsnorm(x, gffn_ref[...]))
    for lo, hi in ((0, FFN_SPLIT), (FFN_SPLIT, FFN_HIDDEN)):
        gate = _dot(h, wfi_ref[:, lo:hi])
        up = _dot(h, wfi_ref[:, FFN_HIDDEN + lo:FFN_HIDDEN + hi])
        x = x + _dot(_silu(gate) * up, wfo_ref[lo:hi, :])
    if final_norm:
        x = _rmsnorm(x, gfin_ref[...])
    o_ref[...] = x


class _PerLayer:
    def __init__(self, value):
        self.value = value


def _const_spec(const, layer):
    if isinstance(const, _PerLayer):
        shape = const.value.shape[1:]
        return pl.BlockSpec((None,) + shape, lambda b, s, _l=layer, _nd=len(shape): (_l,) + (0,) * _nd,
                            pipeline_mode=pl.Buffered(1))
    return pl.BlockSpec(const.shape, lambda b, s, _nd=const.ndim: (0,) * _nd, pipeline_mode=pl.Buffered(1))


def _const_value(const):
    return const.value if isinstance(const, _PerLayer) else const


def _per_layer(fn, *stacked_args):
    return [_PerLayer(v) for v in jax.vmap(fn)(*stacked_args)]


def _seq_spec(ts, width):
    return pl.BlockSpec((None, ts, width), lambda b, s: (b, s, 0))


def _params():
    return pltpu.CompilerParams(dimension_semantics=("arbitrary", "arbitrary"),
                                vmem_limit_bytes=VMEM_LIMIT_BYTES)


def _run_mixer(kernel_fn, x, layer, consts, out_width, scratch, ts):
    bsz, seq, _ = x.shape
    return pl.pallas_call(
        functools.partial(kernel_fn, ts=ts),
        grid=(bsz, seq // ts),
        in_specs=[_seq_spec(ts, D_MODEL)] + [_const_spec(c, layer) for c in consts],
        out_specs=_seq_spec(ts, out_width),
        out_shape=jax.ShapeDtypeStruct((bsz, seq, out_width), BF16),
        scratch_shapes=scratch,
        compiler_params=_params(),
    )(x, *[_const_value(c) for c in consts])


def _row(v):
    return v.astype(F32).reshape(1, -1)


def _pad_cols(m, width):
    return jnp.pad(m, ((0, 0), (0, width - m.shape[1])))


def _rows8(rows, width):
    m = jnp.concatenate([_row(r) for r in rows], axis=0)
    return jnp.pad(m, ((0, SUBLANES - m.shape[0]), (0, width - m.shape[1])))


def _head_sum_matrix(width, head_dim):
    idx = np.arange(width) // head_dim
    return jnp.asarray(idx[:, None] == idx[None, :], dtype=BF16)


def _tri_ones(n):
    return jnp.asarray(np.tril(np.ones((n, n), np.float32)), dtype=BF16)


def _rwkv_layer_consts(gain, w_in, mu, w0, w_up, a0, a_up, g_up, k_k, k_a, r_k, gn_w, gn_b):
    W = RWKV_WIDTH
    dr, ir, gr = RWKV_DECAY_RANK, RWKV_ICL_RANK, RWKV_GATE_RANK
    w = _pad_cols(w_in, RWKV_U_COLS).astype(BF16)
    mu_p = _pad_cols(_row(mu), RWKV_U_COLS)
    wlr = jnp.zeros((LANES, 2 * W), F32).at[0:dr, 0:W].set(w_up).at[dr:dr + ir, W:].set(a_up).astype(BF16)
    blr = jnp.concatenate([_row(w0), _row(a0)], axis=1)
    gup = jnp.pad(g_up, ((0, RWKV_GATE_PAD - gr), (0, 0))).astype(BF16)
    pv = _rows8([k_k, k_a, r_k.reshape(-1), gn_w, gn_b], W)
    return _row(gain), w, mu_p, wlr, blr, gup, pv


def _rwkv_branch(x, layer, layer_consts, ts):
    W = RWKV_WIDTH
    consts = layer_consts + [_head_sum_matrix(MXU_TILE, RWKV_HEAD_DIM), _tri_ones(CHUNK)]
    scratch = ([pltpu.VMEM((ts + SUBLANES, RWKV_U_COLS), F32),
                pltpu.VMEM((W // RWKV_GROUP_LANES, RWKV_GROUP_LANES, RWKV_GROUP_LANES), F32)]
               + [pltpu.VMEM((ts, W), F32) for _ in range(8)])
    return _run_mixer(_rwkv_kernel, x, layer, consts, W, scratch, ts)


def _hgrn_layer_consts(gain, w_in, lb, gn_w):
    lb = lb.astype(F32)
    pv = _rows8([jnp.log(lb), jnp.log1p(-lb), gn_w, 1.0 - lb], HGRN_WIDTH)
    return _row(gain), w_in.astype(BF16), pv


def _hgrn_branch(x, layer, layer_consts, ts):
    W = HGRN_WIDTH
    t_i = np.arange(SUB)[:, None]
    pair = np.arange(SUB * SUB)[None, :]
    sel = jnp.asarray((pair // SUB == t_i) & (pair % SUB <= t_i), dtype=BF16)
    bsel = jnp.asarray(np.arange(ts // SUB)[:, None] == (np.arange(ts) // SUB)[None, :], dtype=BF16)
    consts = layer_consts + [_tri_ones(CHUNK), jnp.ones((LANES, LANES), BF16), sel, bsel]
    scratch = ([pltpu.VMEM((HGRN_HEADS, HGRN_HEAD_DIM, HGRN_HEAD_DIM), F32), pltpu.VMEM((ts, D_MODEL), BF16)]
               + [pltpu.VMEM((ts, W), F32) for _ in range(5)])
    return _run_mixer(_hgrn_kernel, x, layer, consts, W, scratch, ts)


def _ssm_layer_consts(gain, w_in, conv_w, conv_b, dt_bias, a_log, d_skip_lanes, gn_w):
    w = _pad_cols(w_in, SSM_U_COLS).astype(BF16)
    cw = jnp.pad(conv_w.astype(F32).T, ((0, SUBLANES - SSM_CONV_WIDTH), (0, 0)))
    hv = _rows8([dt_bias, -jnp.exp(a_log.astype(F32))], LANES)
    wv = _rows8([d_skip_lanes, gn_w], SSM_WIDTH)
    return _row(gain), w, cw, _row(conv_b), hv, wv


def _ssm_branch(x, layer, layer_consts, ts):
    W = SSM_WIDTH
    consts = layer_consts + [_tri_ones(CHUNK)]
    scratch = [pltpu.VMEM((ts + SUBLANES, SSM_CONV_DIM), F32),
               pltpu.VMEM((SSM_GROUPS, SSM_STATE, SSM_GROUP_WIDTH), F32),
               pltpu.VMEM((ts, D_MODEL), BF16), pltpu.VMEM((ts, W), F32),
               pltpu.VMEM((ts, W), F32), pltpu.VMEM((ts, W), F32),
               pltpu.VMEM((ts, SSM_BC), F32), pltpu.VMEM((ts, SSM_BC), F32),
               pltpu.VMEM((ts, W), F32)]
    return _run_mixer(_ssm_kernel, x, layer, consts, W, scratch, ts)


def _merge_layer_consts(gain_mix, w_gate, w_branch, w_out, gain_ffn, w_ffn_in, w_ffn_out):
    return (_row(gain_mix), w_gate.astype(BF16), w_branch.astype(BF16), w_out.astype(BF16), _row(gain_ffn),
            w_ffn_in.astype(BF16), w_ffn_out.astype(BF16))


def _merge_ffn(x, ya, yb, yc, layer, layer_consts, gain_final, final_norm, ts):
    bsz, seq, _ = x.shape
    consts = layer_consts + [_row(gain_final)]
    seqs = [x, ya, yb, yc]
    return pl.pallas_call(
        functools.partial(_merge_kernel, final_norm=final_norm),
        grid=(bsz, seq // ts),
        in_specs=[_seq_spec(ts, a.shape[-1]) for a in seqs] + [_const_spec(c, layer) for c in consts],
        out_specs=_seq_spec(ts, D_MODEL),
        out_shape=jax.ShapeDtypeStruct((bsz, seq, D_MODEL), F32),
        compiler_params=_params(),
    )(*seqs, *[_const_value(c) for c in consts])


def kernel(x, norm_mix, w_in, rwkv_mu, rwkv_w0, rwkv_w_up, rwkv_a0, rwkv_a_up, rwkv_g_up, rwkv_k_k, rwkv_k_a,
           rwkv_r_k, rwkv_gn_w, rwkv_gn_b, hgrn_lb_logits, hgrn_gn_w, ssm_conv_w, ssm_conv_b, ssm_dt_bias,
           ssm_a_log, ssm_d, ssm_gn_w, w_branch, w_out, norm_ffn, w_ffn_in, w_ffn_out, norm_final):
    bsz, seq, d = x.shape
    assert d == D_MODEL
    depth = w_in.shape[0]
    ts = min(SEQ_BLOCK, seq)
    tm = min(MERGE_BLOCK, seq)
    assert seq % ts == 0 and seq % tm == 0
    assert all(ts % (CHUNK * n) == 0 for n in (LOOP_CHUNKS, HGRN_LOOP_CHUNKS, RWKV_LOOP_CHUNKS))

    rwkv_cols = 3 * RWKV_WIDTH + RWKV_DECAY_RANK + RWKV_ICL_RANK + RWKV_GATE_RANK
    off_hgrn = rwkv_cols
    off_ssm = off_hgrn + 4 * HGRN_WIDTH
    off_gate = off_ssm + SSM_WIDTH + SSM_CONV_DIM + SSM_HEADS

    cs = jnp.cumsum(jax.nn.softmax(hgrn_lb_logits.astype(F32), axis=0), axis=0)
    lbs = cs - cs[:1]

    rwkv_consts = _per_layer(_rwkv_layer_consts, norm_mix, w_in[:, :, :off_hgrn], rwkv_mu, rwkv_w0, rwkv_w_up,
                             rwkv_a0, rwkv_a_up, rwkv_g_up, rwkv_k_k, rwkv_k_a, rwkv_r_k, rwkv_gn_w, rwkv_gn_b)
    hgrn_consts = _per_layer(_hgrn_layer_consts, norm_mix, w_in[:, :, off_hgrn:off_ssm], lbs, hgrn_gn_w)
    ssm_consts = _per_layer(_ssm_layer_consts, norm_mix, w_in[:, :, off_ssm:off_gate], ssm_conv_w, ssm_conv_b,
                            ssm_dt_bias, ssm_a_log, jnp.repeat(ssm_d.astype(F32), SSM_HEAD_DIM, axis=1), ssm_gn_w)
    merge_consts = _per_layer(_merge_layer_consts, norm_mix, w_in[:, :, off_gate:], w_branch, w_out, norm_ffn,
                              w_ffn_in, w_ffn_out)

    x = x.astype(F32)
    for l in range(depth):
        ya = _rwkv_branch(x, l, rwkv_consts, ts)
        yb = _hgrn_branch(x, l, hgrn_consts, ts)
        yc = _ssm_branch(x, l, ssm_consts, ts)
        x = _merge_ffn(x, ya, yb, yc, l, merge_consts, norm_final, l == depth - 1, tm)
    return x
```

```python
import functools

import numpy as np
import jax
import jax.numpy as jnp
from jax import lax
from jax.experimental import pallas as pl
from jax.experimental.pallas import tpu as pltpu

F32 = jnp.float32
BF16 = jnp.bfloat16

D_MODEL = 1024
CHUNK = 64
LOOP_CHUNKS = 2
HGRN_LOOP_CHUNKS = 8
SUB = 16
HGRN_FAST_LOG_RANGE = 60.0
NORM_EPS = 1e-5
LANES = 128
SUBLANES = 8
MXU_TILE = 256
VMEM_LIMIT_BYTES = 56 * 1024 * 1024

RWKV_HEAD_DIM = 64
RWKV_WIDTH = D_MODEL
RWKV_DECAY_RANK = 64
RWKV_ICL_RANK = 64
RWKV_GATE_RANK = 160
RWKV_GATE_PAD = 256
RWKV_GN_EPS = 64e-5
RWKV_LOOP_CHUNKS = 8
RWKV_GROUP_LANES = MXU_TILE
RWKV_U_COLS = 3 * RWKV_WIDTH + LANES + RWKV_GATE_PAD

HGRN_HEAD_DIM = 128
HGRN_WIDTH = D_MODEL
HGRN_HEADS = HGRN_WIDTH // HGRN_HEAD_DIM

SSM_WIDTH = 2 * D_MODEL
SSM_HEAD_DIM = 64
SSM_HEADS = SSM_WIDTH // SSM_HEAD_DIM
SSM_GROUPS = 4
SSM_STATE = 128
SSM_CONV_WIDTH = 4
SSM_BC = SSM_GROUPS * SSM_STATE
SSM_CONV_DIM = SSM_WIDTH + 2 * SSM_BC
SSM_GROUP_WIDTH = SSM_WIDTH // SSM_GROUPS
SSM_U_COLS = SSM_WIDTH + SSM_CONV_DIM + LANES

FFN_HIDDEN = ((8 * D_MODEL + 3 * 256 - 1) // (3 * 256)) * 256

SEQ_BLOCK = 512
MERGE_BLOCK = 512
FFN_SPLIT = 6 * MXU_TILE


def _bf(x):
    return x if x.dtype == BF16 else x.astype(BF16)


def _dot(a, b):
    return jnp.dot(_bf(a), _bf(b), preferred_element_type=F32)


def _dot_nt(a, b):
    return lax.dot_general(_bf(a), _bf(b), (((1,), (1,)), ((), ())), preferred_element_type=F32)


def _dot_tn(a, b):
    return lax.dot_general(_bf(a), _bf(b), (((0,), (0,)), ((), ())), preferred_element_type=F32)


def _split(x, n):
    parts = []
    rest = x
    for i in range(n):
        p = rest.astype(BF16)
        parts.append(p)
        if i + 1 < n:
            rest = rest - p.astype(F32)
    return parts


def _dot_sel_left(sel, x, passes=3):
    out = None
    for p in _split(x, passes):
        t = jnp.dot(sel, p, preferred_element_type=F32)
        out = t if out is None else out + t
    return out


def _rmsnorm(x, gain):
    ms = jnp.mean(x * x, axis=-1, keepdims=True)
    return x * lax.rsqrt(ms + NORM_EPS) * gain


def _sigmoid(x):
    return 1.0 / (1.0 + jnp.exp(-x))


def _silu(x):
    return x * _sigmoid(x)


def _softplus(x):
    return jnp.maximum(x, 0.0) + jnp.log(1.0 + jnp.exp(-jnp.abs(x)))


def _iota(shape, axis):
    return lax.broadcasted_iota(jnp.int32, shape, axis)


def _interleave(*programs):
    live = list(programs)
    while live:
        for gen in list(live):
            try:
                next(gen)
            except StopIteration:
                live.remove(gen)


def _rwkv_kernel(x_ref, gain_ref, w_ref, mu_ref, wlr_ref, blr_ref, gup_ref, pv_ref, hsum_ref, tri_ref,
                 o_ref, ush_ref, st_ref, r_s, lw_s, k_s, v_s, ka_s, kb_s, o_s, g_s, *, ts):
    W = RWKV_WIDTH

    @pl.when(pl.program_id(1) == 0)
    def _():
        ush_ref[0:SUBLANES, :] = jnp.zeros((SUBLANES, RWKV_U_COLS), F32)
        st_ref[...] = jnp.zeros(st_ref.shape, F32)

    h = _bf(_rmsnorm(x_ref[...], gain_ref[...]))

    def project(lo, hi):
        ush_ref[SUBLANES:SUBLANES + ts, lo:hi] = _dot(h, w_ref[:, lo:hi])

    def token_shift(lo, hi):
        full = ush_ref[0:ts + SUBLANES, lo:hi]
        u = full[SUBLANES:, :]
        prev = pltpu.roll(full, 1, 0)[SUBLANES:, :]
        ush_ref[0:SUBLANES, lo:hi] = u[ts - SUBLANES:ts, :]
        return u + (prev - u) * mu_ref[:, lo:hi]

    project(3 * W, RWKV_U_COLS)
    project(W, 2 * W)
    low_rank = token_shift(3 * W, RWKV_U_COLS)
    xwa = low_rank[:, 0:LANES]
    xg = low_rank[:, LANES:]
    lane = _iota(xwa.shape, 1)
    lr_in = jnp.where(lane < RWKV_DECAY_RANK, jnp.tanh(xwa), xwa)
    lr = _dot(lr_in, wlr_ref[...]) + blr_ref[...]
    log_w = -jnp.exp(-_softplus(-lr[:, 0:W]) - 0.5)
    a = _sigmoid(lr[:, W:])
    g_s[...] = _dot(_sigmoid(xg), gup_ref[...])
    project(0, W)
    k = token_shift(W, 2 * W)

    k_k = pv_ref[0:1, :]
    k_a = pv_ref[1:2, :]
    r_k = pv_ref[2:3, :]
    gn_w = pv_ref[3:4, :]
    gn_b = pv_ref[4:5, :]
    hsum = hsum_ref[...]

    def head_sum(t):
        return jnp.concatenate([_dot(t[:, MXU_TILE * j:MXU_TILE * (j + 1)], hsum)
                                for j in range(W // MXU_TILE)], axis=1)

    kk = k * k_k
    ss = head_sum(kk * kk)
    kk = kk * lax.rsqrt(jnp.maximum(ss, 1e-24))
    kmod = k * (1.0 + (a - 1.0) * k_a)

    lw_s[...] = log_w
    k_s[...] = kmod
    ka_s[...] = -kk
    kb_s[...] = kk * a
    project(2 * W, 3 * W)
    r_s[...] = token_shift(0, W)
    v_s[...] = token_shift(2 * W, 3 * W)

    tri = tri_ref[...]
    gl = RWKV_GROUP_LANES
    n_groups = W // gl
    heads_per_group = gl // RWKV_HEAD_DIM
    row_w = _iota((CHUNK, gl), 0)
    col_w = _iota((CHUNK, gl), 1) & (RWKV_HEAD_DIM - 1)
    strict = row_w > col_w
    incl = row_w >= col_w
    eye_w = jnp.where(row_w == col_w, 1.0, 0.0)
    first_head = _iota((CHUNK, LANES), 1) < RWKV_HEAD_DIM
    head_shift = int(np.log2(RWKV_HEAD_DIM))
    same_head = (_iota((gl, gl), 0) >> head_shift) == (_iota((gl, gl), 1) >> head_shift)
    zeros_tile = jnp.zeros((CHUNK, LANES), F32)

    def blockdiag(t):
        blocks = []
        for hd in range(heads_per_group):
            tile = t[:, LANES * (hd // 2):LANES * (hd // 2 + 1)]
            kept = jnp.where(first_head, tile, 0.0) if hd % 2 == 0 else jnp.where(first_head, 0.0, tile)
            tiles = [kept if j == hd // 2 else zeros_tile for j in range(gl // LANES)]
            blocks.append(jnp.concatenate(tiles, axis=1))
        return jnp.concatenate(blocks, axis=0)

    lanes = [slice(gl * g, gl * (g + 1)) for g in range(n_groups)]

    def state_free(c, ctx):
        c0 = pl.multiple_of(c * CHUNK, CHUNK)
        sl = pl.ds(c0, CHUNK)
        lw = lw_s[sl, :]
        cum = _dot_sel_left(tri, lw, 2)
        e_last = jnp.exp(cum[CHUNK - 1:CHUNK, :])
        e_neg = jnp.exp(-cum)
        rt = r_s[sl, :] * jnp.exp(cum)
        at = ka_s[sl, :] * jnp.exp(cum - lw)
        bt = kb_s[sl, :] * e_neg
        kt = k_s[sl, :] * e_neg
        vv = v_s[sl, :]
        ar = [jnp.concatenate([at[:, ls], rt[:, ls]], axis=0) for ls in lanes]
        bk_bd = [jnp.concatenate([blockdiag(bt[:, ls]), blockdiag(kt[:, ls])], axis=0) for ls in lanes]
        v_bd = [blockdiag(vv[:, ls]) for ls in lanes]
        bk_end = [jnp.concatenate([bt[:, ls], kt[:, ls]], axis=0) * e_last[:, ls] for ls in lanes]
        yield
        gram = [_dot_nt(a, b) for a, b in zip(ar, bk_bd)]
        n_ab = [jnp.where(strict, g[0:CHUNK, 0:gl], 0.0) for g in gram]
        n_ak = [jnp.where(strict, g[0:CHUNK, gl:], 0.0) for g in gram]
        n_rbk = [jnp.concatenate([jnp.where(incl, g[CHUNK:, 0:gl], 0.0), jnp.where(incl, g[CHUNK:, gl:], 0.0)], axis=1)
                 for g in gram]
        yield
        ak_v = [_dot(n, v) for n, v in zip(n_ak, v_bd)]
        prod = [eye_w + n for n in n_ab]
        npow = [_dot(n, blockdiag(n)) for n in n_ab]
        for _ in range(int(np.log2(CHUNK)) - 2):
            yield
            both = [_dot(jnp.concatenate([nk, pr], axis=0), blockdiag(nk)) for nk, pr in zip(npow, prod)]
            prod = [pr + b[CHUNK:] for pr, b in zip(prod, both)]
            npow = [b[0:CHUNK] for b in both]
        yield
        inv = [pr + _dot(pr, blockdiag(nk)) for pr, nk in zip(prod, npow)]
        ctx.update(sl=sl, e_last=e_last, ar=ar, vv=vv, v_bd=v_bd, bk_end=bk_end, ak_v=ak_v, n_rbk=n_rbk, inv=inv)

    def state_bound(ctx, states):
        sl, e_last = ctx["sl"], ctx["e_last"]
        from_state = [_dot_nt(a, s) for a, s in zip(ctx["ar"], states)]
        yield
        u = [_dot(t, blockdiag(f[0:CHUNK] + r)) for t, f, r in zip(ctx["inv"], from_state, ctx["ak_v"])]
        yield
        for g, ls in enumerate(lanes):
            uv = jnp.concatenate([u[g], ctx["vv"][:, ls]], axis=0)
            states[g] = states[g] * e_last[:, ls] + jnp.where(same_head, _dot_tn(uv, ctx["bk_end"][g]), 0.0)
        yield
        for g, ls in enumerate(lanes):
            o_s[sl, ls] = from_state[g][CHUNK:] + _dot(ctx["n_rbk"][g],
                                                       jnp.concatenate([blockdiag(u[g]), ctx["v_bd"][g]], axis=0))

    def chained(*programs):
        for program in programs:
            yield from program

    def chunk_body(i, carry):
        n = RWKV_LOOP_CHUNKS
        states = [st_ref[g] for g in range(n_groups)]
        ctxs = [dict() for _ in range(n)]
        pending = []
        for j in range(0, n, 2):
            frees = [state_free(n * i + j + d, ctxs[j + d]) for d in range(2)]
            _interleave(*frees, *([chained(*pending)] if pending else []))
            pending = [state_bound(ctxs[j + d], states) for d in range(2)]
        _interleave(chained(*pending))
        for g in range(n_groups):
            st_ref[g] = states[g]
        return carry

    lax.fori_loop(0, ts // (CHUNK * RWKV_LOOP_CHUNKS), chunk_body, 0)

    o = o_s[...]
    inv_n = 1.0 / RWKV_HEAD_DIM
    mean = head_sum(o) * inv_n
    cen = o - mean
    var = head_sum(cen * cen) * inv_n
    o = cen * lax.rsqrt(var + RWKV_GN_EPS) * gn_w + gn_b
    bonus = head_sum(r_s[...] * k_s[...] * r_k)
    o = o + bonus * v_s[...]
    o_ref[...] = (o * g_s[...]).astype(o_ref.dtype)


def _hgrn_kernel(x_ref, gain_ref, w_ref, pv_ref, tri_ref, ones_ref, sel_ref, bsel_ref,
                 o_ref, st_ref, h_s, q_s, k_s, v_s, lf_s, o_s, *, ts):
    W = HGRN_WIDTH
    nsub = CHUNK // SUB

    @pl.when(pl.program_id(1) == 0)
    def _():
        st_ref[...] = jnp.zeros(st_ref.shape, F32)

    h = _bf(_rmsnorm(x_ref[...], gain_ref[...]))
    h_s[...] = h
    log_lb = pv_ref[0:1, :]
    log1m_lb = pv_ref[1:2, :]
    gn_w = pv_ref[2:3, :]
    f_pre = _dot(h, w_ref[:, W:2 * W])
    q_pre = _dot(h, w_ref[:, 0:W])
    e = jnp.exp(-jnp.abs(f_pre))
    b = log1m_lb - (jnp.maximum(-f_pre, 0.0) + jnp.log(1.0 + e))
    mx = jnp.maximum(log_lb, b)
    log_f = mx + jnp.log(1.0 + jnp.exp(-jnp.abs(log_lb - b)))
    lf_s[...] = log_f
    k_s[...] = pv_ref[3:4, :] * jnp.where(f_pre > 0.0, e, 1.0) / (1.0 + e)
    v_s[...] = _dot(h, w_ref[:, 2 * W:3 * W])
    q_s[...] = _silu(q_pre)

    tri = tri_ref[...]
    ones_sq = ones_ref[...]
    sel = sel_ref[...]
    sub_shift = int(np.log2(SUB))
    rblk = _iota((CHUNK, CHUNK), 0) >> sub_shift
    cblk = _iota((CHUNK, CHUNK), 1) >> sub_shift
    off_mask = cblk < rblk
    rowblk = _iota((CHUNK, LANES), 0) >> sub_shift
    s_idx = _iota((SUB, LANES), 0)

    heads = range(HGRN_HEADS)
    lanes = [slice(LANES * hd, LANES * (hd + 1)) for hd in heads]
    causal = _iota((CHUNK, CHUNK), 1) <= _iota((CHUNK, CHUNK), 0)

    def fast_chunk(c, states):
        c0 = pl.multiple_of(c * CHUNK, CHUNK)
        sl = pl.ds(c0, CHUNK)
        cum = _dot_sel_left(tri, lf_s[sl, :], 3)
        q = q_s[sl, :]
        k = k_s[sl, :]
        v = v_s[sl, :]
        yield
        starts = [None] + [cum[SUB * i - 1:SUB * i, :] for i in range(1, nsub)]
        cs = jnp.zeros_like(cum)
        rowblk_w = _iota(cum.shape, 0) >> sub_shift
        for i in range(1, nsub):
            cs = jnp.where(rowblk_w == i, starts[i], cs)
        q_sub = q * jnp.exp(cum - cs)
        k_own = k * jnp.exp(cs - cum)
        k_prev = [None] + [k[0:SUB * i, :] * jnp.exp(jnp.minimum(starts[i] - cum[0:SUB * i, :], 0.0))
                           for i in range(1, nsub)]
        yield
        last = cum[CHUNK - 1:CHUNK, :]
        q_in = q * jnp.exp(cum)
        k_end = k * jnp.exp(last - cum)
        e_last = jnp.exp(last)
        before = list(states)
        for hd, ls in enumerate(lanes):
            states[hd] = before[hd] * e_last[:, ls] + _dot_tn(v[:, ls], k_end[:, ls])
        yield
        scores = []
        for n, ls in enumerate(lanes):
            rows = []
            for i in range(nsub):
                parts = [k_own[SUB * i:SUB * (i + 1), ls]]
                if i > 0:
                    parts = [k_prev[i][:, ls]] + parts
                if i + 1 < nsub:
                    parts.append(jnp.zeros((CHUNK - SUB * (i + 1), LANES), F32))
                rows.append(_dot_nt(q_sub[SUB * i:SUB * (i + 1), ls], jnp.concatenate(parts, axis=0)))
            scores.append(jnp.where(causal, jnp.concatenate(rows, axis=0), 0.0))
            if n % 4 == 3:
                yield
        outs = [_dot(a, v[:, ls]) + _dot_nt(q_in[:, ls], s) for a, ls, s in zip(scores, lanes, before)]
        yield
        for hd, ls in enumerate(lanes):
            o_s[sl, ls] = outs[hd]

    def fast_chunk_body(i, carry):
        states = [st_ref[hd] for hd in heads]
        _interleave(*[fast_chunk(HGRN_LOOP_CHUNKS * i + j, states) for j in range(HGRN_LOOP_CHUNKS)])
        for hd in heads:
            st_ref[hd] = states[hd]
        return carry

    def chunk_body(c, carry):
        c0 = pl.multiple_of(c * CHUNK, CHUNK)
        sl = pl.ds(c0, CHUNK)
        cum_all = _dot_sel_left(tri, lf_s[sl, :], 3)
        q_all = q_s[sl, :]
        k_all = k_s[sl, :]
        v_all = v_s[sl, :]
        for hd in range(HGRN_HEADS):
            ls = slice(LANES * hd, LANES * (hd + 1))
            cum = cum_all[:, ls]
            q = q_all[:, ls]
            k = k_all[:, ls]
            v = v_all[:, ls]
            vb = _bf(v)
            last = cum[CHUNK - 1:CHUNK, :]
            starts = [None] + [cum[SUB * i - 1:SUB * i, :] for i in range(1, nsub)]
            cs = jnp.zeros_like(cum)
            for i in range(1, nsub):
                cs = jnp.where(rowblk == i, starts[i], cs)
            q_sub = q * jnp.exp(cum - cs)
            rows = [jnp.zeros((SUB, CHUNK), F32)]
            for i in range(1, nsub):
                k_i = k * jnp.exp(jnp.minimum(starts[i] - cum, 0.0))
                rows.append(_dot_nt(q_sub[SUB * i:SUB * (i + 1), :], k_i))
            a_off = jnp.where(off_mask, jnp.concatenate(rows, axis=0), 0.0)
            state = st_ref[hd]
            o = _dot(a_off, vb) + _dot_nt(q * jnp.exp(cum), state)
            diag = []
            for j in range(nsub):
                rs = slice(SUB * j, SUB * (j + 1))
                cum_b = cum[rs, :]
                q_b = q[rs, :]
                k_b = k[rs, :]
                pieces = []
                for t in range(SUB):
                    expo = jnp.where(s_idx <= t, cum_b[t:t + 1, :] - cum_b, -1e30)
                    pieces.append(jnp.exp(expo) * k_b * q_b[t:t + 1, :])
                pmat = jnp.concatenate(pieces, axis=0)
                score = _dot(pmat, ones_sq)
                wv = score * jnp.concatenate([v[rs, :]] * SUB, axis=0)
                diag.append(_dot(sel, wv))
            o = o + jnp.concatenate(diag, axis=0)
            o_s[sl, ls] = o
            k_end = k * jnp.exp(last - cum)
            st_ref[hd] = state * jnp.exp(last) + _dot_tn(vb, k_end)
        return carry

    sub_sums = _dot_sel_left(bsel_ref[...], log_f, 3)
    fast_ok = jnp.min(sub_sums) >= -HGRN_FAST_LOG_RANGE

    @pl.when(fast_ok)
    def _():
        lax.fori_loop(0, ts // (CHUNK * HGRN_LOOP_CHUNKS), fast_chunk_body, 0)

    @pl.when(jnp.logical_not(fast_ok))
    def _():
        lax.fori_loop(0, ts // CHUNK, chunk_body, 0)

    gate = _sigmoid(_dot(h_s[...], w_ref[:, 3 * W:]))
    for hd in range(HGRN_HEADS):
        ls = slice(LANES * hd, LANES * (hd + 1))
        o = o_s[:, ls]
        o = o * lax.rsqrt(jnp.mean(o * o, axis=-1, keepdims=True) + NORM_EPS) * gn_w[:, ls]
        o_ref[:, ls] = (o * gate[:, ls]).astype(o_ref.dtype)


def _ssm_kernel(x_ref, gain_ref, w_ref, cw_ref, cb_ref, hv_ref, wv_ref, tri_ref,
                o_ref, xb_ref, st_ref, h_s, xs_s, ec_s, xd_s, b_s, c_s, y_s, *, ts):
    W = SSM_WIDTH
    gw = SSM_GROUP_WIDTH
    pairs = W // LANES

    @pl.when(pl.program_id(1) == 0)
    def _():
        xb_ref[0:SUBLANES, :] = jnp.zeros((SUBLANES, SSM_CONV_DIM), F32)
        st_ref[...] = jnp.zeros(st_ref.shape, F32)

    h = _bf(_rmsnorm(x_ref[...], gain_ref[...]))
    h_s[...] = h

    dt_bias = hv_ref[0:1, :]
    neg_a = hv_ref[1:2, :]
    dt = _softplus(_dot(h, w_ref[:, W + SSM_CONV_DIM:]) + dt_bias)
    log_a = dt * neg_a
    tri = tri_ref[...]
    cum = jnp.concatenate([_dot_sel_left(tri, log_a[CHUNK * c:CHUNK * (c + 1), :], 3)
                           for c in range(ts // CHUNK)], axis=0)

    first_head_rows = _iota((ts, LANES), 1) < SSM_HEAD_DIM
    heads_per_group = gw // SSM_HEAD_DIM

    def expand_heads(v, g):
        tiles = []
        for j in range(gw // LANES):
            h0 = heads_per_group * g + 2 * j
            a = jnp.broadcast_to(v[:, h0:h0 + 1], (ts, LANES))
            b = jnp.broadcast_to(v[:, h0 + 1:h0 + 2], (ts, LANES))
            tiles.append(jnp.where(first_head_rows, a, b))
        return jnp.concatenate(tiles, axis=1)

    def project(lo, hi):
        xb_ref[SUBLANES:SUBLANES + ts, lo:hi] = _dot(h, w_ref[:, W + lo:W + hi])

    def conv_silu(lo, hi):
        full = xb_ref[0:ts + SUBLANES, lo:hi]
        conv = cb_ref[:, lo:hi] + full[SUBLANES:, :] * cw_ref[SSM_CONV_WIDTH - 1:SSM_CONV_WIDTH, lo:hi]
        for k in range(1, SSM_CONV_WIDTH):
            j = SSM_CONV_WIDTH - 1 - k
            conv = conv + pltpu.roll(full, k, 0)[SUBLANES:, :] * cw_ref[j:j + 1, lo:hi]
        xb_ref[0:SUBLANES, lo:hi] = xb_ref[ts:ts + SUBLANES, lo:hi]
        return _silu(conv)

    def finish(idx):
        lo, hi = slabs[idx]
        act = conv_silu(lo, hi)
        if idx < SSM_GROUPS:
            xs_s[:, lo:hi] = act
            xd_s[:, lo:hi] = act * expand_heads(dt, idx)
            ec_s[:, lo:hi] = expand_heads(cum, idx)
        elif idx == SSM_GROUPS:
            b_s[...] = act
        else:
            c_s[...] = act

    slabs = [(gw * g, gw * (g + 1)) for g in range(SSM_GROUPS)] + [(W, W + SSM_BC), (W + SSM_BC, W + 2 * SSM_BC)]
    project(*slabs[0])
    for idx in range(len(slabs)):
        if idx + 1 < len(slabs):
            project(*slabs[idx + 1])
        finish(idx)

    t_idx = _iota((CHUNK, W), 0)
    s_idx = _iota((CHUNK, W), 1) & (CHUNK - 1)
    causal = s_idx <= t_idx
    on_diag = s_idx == t_idx
    first_head = (_iota((CHUNK, LANES), 1) < SSM_HEAD_DIM)

    groups = range(SSM_GROUPS)
    glanes = [slice(gw * g, gw * (g + 1)) for g in groups]
    per_group = gw // LANES

    def chunk(c, states):
        c0 = pl.multiple_of(c * CHUNK, CHUNK)
        sl = pl.ds(c0, CHUNK)
        ec = ec_s[sl, :]
        xd = xd_s[sl, :]
        bm = b_s[sl, :]
        cm = c_s[sl, :]
        last = ec[CHUNK - 1:CHUNK, :]
        by_src = jnp.sum(jnp.where(on_diag, ec, 0.0), axis=0, keepdims=True)
        decay = jnp.exp(jnp.where(causal, ec - by_src, -1e30))
        yield
        xw = xd * jnp.exp(last - ec)
        e_in = jnp.exp(ec)
        e_last = jnp.exp(last)
        c_g = [_bf(cm[:, SSM_STATE * g:SSM_STATE * (g + 1)]) for g in groups]
        b_g = [_bf(bm[:, SSM_STATE * g:SSM_STATE * (g + 1)]) for g in groups]
        cb2 = [_dot_nt(c, jnp.concatenate([b, b], axis=0)) for c, b in zip(c_g, b_g)]
        yield
        before = list(states)
        for g, gl in enumerate(glanes):
            states[g] = before[g] * e_last[:, gl] + _dot_tn(b_g[g], xw[:, gl])
        y_in = [_dot(c, s) for c, s in zip(c_g, before)]
        yield
        ys = []
        for p in range(pairs):
            ls = slice(LANES * p, LANES * (p + 1))
            m = decay[:, ls] * cb2[(LANES * p) // gw]
            xp = xd[:, ls]
            x2 = jnp.concatenate([jnp.where(first_head, xp, 0.0), jnp.where(first_head, 0.0, xp)], axis=0)
            ys.append(_dot(m, x2))
            if p % 4 == 3:
                yield
        for g, gl in enumerate(glanes):
            y_s[sl, gl] = (y_in[g] * e_in[:, gl]
                           + jnp.concatenate(ys[per_group * g:per_group * (g + 1)], axis=1))

    def chunk_body(i, carry):
        states = [st_ref[g] for g in groups]
        _interleave(*[chunk(LOOP_CHUNKS * i + j, states) for j in range(LOOP_CHUNKS)])
        for g in groups:
            st_ref[g] = states[g]
        return carry

    lax.fori_loop(0, ts // (CHUNK * LOOP_CHUNKS), chunk_body, 0)

    d_skip = wv_ref[0:1, :]
    gn_w = wv_ref[1:2, :]
    h = h_s[...]
    for g in range(SSM_GROUPS):
        gl = slice(gw * g, gw * (g + 1))
        z = _dot(h, w_ref[:, gl])
        yg = (y_s[:, gl] + d_skip[:, gl] * xs_s[:, gl]) * _silu(z)
        yg = yg * lax.rsqrt(jnp.mean(yg * yg, axis=-1, keepdims=True) + NORM_EPS)
        o_ref[:, gl] = (yg * gn_w[:, gl]).astype(o_ref.dtype)


def _merge_kernel(x_ref, ya_ref, yb_ref, yc_ref, gmix_ref, wg_ref, wb_ref, wout_ref,
                  gffn_ref, wfi_ref, wfo_ref, gfin_ref, o_ref, *, final_norm):
    x = x_ref[...]
    h = _bf(_rmsnorm(x, gmix_ref[...]))
    rows_b = RWKV_WIDTH + HGRN_WIDTH
    merged = (_sigmoid(_dot(h, wg_ref[:, 0:D_MODEL]))
              * jnp.dot(ya_ref[...], wb_ref[0:RWKV_WIDTH, :], preferred_element_type=F32))
    merged = merged + (_sigmoid(_dot(h, wg_ref[:, D_MODEL:2 * D_MODEL]))
                       * jnp.dot(yb_ref[...], wb_ref[RWKV_WIDTH:rows_b, :], preferred_element_type=F32))
    merged = merged + (_sigmoid(_dot(h, wg_ref[:, 2 * D_MODEL:]))
                       * jnp.dot(yc_ref[...], wb_ref[rows_b:, :], preferred_element_type=F32))
    x = x + _dot(merged, wout_ref[...])
    h = _bf(_rmsnorm(x, gffn_ref[...]))
    for lo, hi in ((0, FFN_SPLIT), (FFN_SPLIT, FFN_HIDDEN)):
        gate = _dot(h, wfi_ref[:, lo:hi])
        up = _dot(h, wfi_ref[:, FFN_HIDDEN + lo:FFN_HIDDEN + hi])
        x = x + _dot(_silu(gate) * up, wfo_ref[lo:hi, :])
    if final_norm:
        x = _rmsnorm(x, gfin_ref[...])
    o_ref[...] = x


class _PerLayer:
    def __init__(self, value):
        self.value = value


def _const_spec(const, layer):
    if isinstance(const, _PerLayer):
        shape = const.value.shape[1:]
        return pl.BlockSpec((None,) + shape, lambda b, s, _l=layer, _nd=len(shape): (_l,) + (0,) * _nd,
                            pipeline_mode=pl.Buffered(1))
    return pl.BlockSpec(const.shape, lambda b, s, _nd=const.ndim: (0,) * _nd, pipeline_mode=pl.Buffered(1))


def _const_value(const):
    return const.value if isinstance(const, _PerLayer) else const


def _per_layer(fn, *stacked_args):
    return [_PerLayer(v) for v in jax.vmap(fn)(*stacked_args)]


def _seq_spec(ts, width):
    return pl.BlockSpec((None, ts, width), lambda b, s: (b, s, 0))


def _params():
    return pltpu.CompilerParams(dimension_semantics=("arbitrary", "arbitrary"),
                                vmem_limit_bytes=VMEM_LIMIT_BYTES)


def _run_mixer(kernel_fn, x, layer, consts, out_width, scratch, ts):
    bsz, seq, _ = x.shape
    return pl.pallas_call(
        functools.partial(kernel_fn, ts=ts),
        grid=(bsz, seq // ts),
        in_specs=[_seq_spec(ts, D_MODEL)] + [_const_spec(c, layer) for c in consts],
        out_specs=_seq_spec(ts, out_width),
        out_shape=jax.ShapeDtypeStruct((bsz, seq, out_width), BF16),
        scratch_shapes=scratch,
        compiler_params=_params(),
    )(x, *[_const_value(c) for c in consts])


def _row(v):
    return v.astype(F32).reshape(1, -1)


def _pad_rows(m, rows):
    return jnp.concatenate([m, jnp.zeros((rows - m.shape[0], m.shape[1]), m.dtype)], axis=0)


def _pad_cols(m, width):
    return jnp.concatenate([m, jnp.zeros((m.shape[0], width - m.shape[1]), m.dtype)], axis=1)


def _rows8(rows, width):
    m = jnp.concatenate([_row(r) for r in rows], axis=0)
    return _pad_rows(_pad_cols(m, width), SUBLANES)


def _head_sum_matrix(width, head_dim):
    idx = np.arange(width) // head_dim
    return jnp.asarray(idx[:, None] == idx[None, :], dtype=BF16)


def _tri_ones(n):
    return jnp.asarray(np.tril(np.ones((n, n), np.float32)), dtype=BF16)


def _rwkv_layer_consts(gain, w_in, mu, w0, w_up, a0, a_up, g_up, k_k, k_a, r_k, gn_w, gn_b):
    W = RWKV_WIDTH
    dr, ir, gr = RWKV_DECAY_RANK, RWKV_ICL_RANK, RWKV_GATE_RANK
    w = _pad_cols(w_in, RWKV_U_COLS).astype(BF16)
    mu_p = _pad_cols(_row(mu), RWKV_U_COLS)
    wlr = jnp.concatenate([jnp.concatenate([w_up, jnp.zeros((dr, W), w_up.dtype)], axis=1),
                           jnp.concatenate([jnp.zeros((ir, W), a_up.dtype), a_up], axis=1)], axis=0)
    wlr = _pad_rows(wlr, LANES).astype(BF16)
    blr = jnp.concatenate([_row(w0), _row(a0)], axis=1)
    gup = _pad_rows(g_up, RWKV_GATE_PAD).astype(BF16)
    pv = _rows8([k_k, k_a, r_k.reshape(-1), gn_w, gn_b], W)
    return _row(gain), w, mu_p, wlr, blr, gup, pv


def _rwkv_branch(x, layer, layer_consts, ts):
    W = RWKV_WIDTH
    consts = layer_consts + [_head_sum_matrix(MXU_TILE, RWKV_HEAD_DIM), _tri_ones(CHUNK)]
    scratch = ([pltpu.VMEM((ts + SUBLANES, RWKV_U_COLS), F32),
                pltpu.VMEM((W // RWKV_GROUP_LANES, RWKV_GROUP_LANES, RWKV_GROUP_LANES), F32)]
               + [pltpu.VMEM((ts, W), F32) for _ in range(8)])
    return _run_mixer(_rwkv_kernel, x, layer, consts, W, scratch, ts)


def _hgrn_layer_consts(gain, w_in, lb, gn_w):
    lb = lb.astype(F32)
    pv = _rows8([jnp.log(lb), jnp.log1p(-lb), gn_w, 1.0 - lb], HGRN_WIDTH)
    return _row(gain), w_in.astype(BF16), pv


def _hgrn_branch(x, layer, layer_consts, ts):
    W = HGRN_WIDTH
    t_i = np.arange(SUB)[:, None]
    pair = np.arange(SUB * SUB)[None, :]
    sel = jnp.asarray((pair // SUB == t_i) & (pair % SUB <= t_i), dtype=BF16)
    bsel = jnp.asarray(np.arange(ts // SUB)[:, None] == (np.arange(ts) // SUB)[None, :], dtype=BF16)
    consts = layer_consts + [_tri_ones(CHUNK), jnp.ones((LANES, LANES), BF16), sel, bsel]
    scratch = ([pltpu.VMEM((HGRN_HEADS, HGRN_HEAD_DIM, HGRN_HEAD_DIM), F32), pltpu.VMEM((ts, D_MODEL), BF16)]
               + [pltpu.VMEM((ts, W), F32) for _ in range(5)])
    return _run_mixer(_hgrn_kernel, x, layer, consts, W, scratch, ts)


def _ssm_layer_consts(gain, w_in, conv_w, conv_b, dt_bias, a_log, d_skip_lanes, gn_w):
    w = _pad_cols(w_in, SSM_U_COLS).astype(BF16)
    cw = _pad_rows(conv_w.astype(F32).T, SUBLANES)
    hv = _rows8([dt_bias, -jnp.exp(a_log.astype(F32))], LANES)
    wv = _rows8([d_skip_lanes, gn_w], SSM_WIDTH)
    return _row(gain), w, cw, _row(conv_b), hv, wv


def _ssm_branch(x, layer, layer_consts, ts):
    W = SSM_WIDTH
    consts = layer_consts + [_tri_ones(CHUNK)]
    scratch = [pltpu.VMEM((ts + SUBLANES, SSM_CONV_DIM), F32),
               pltpu.VMEM((SSM_GROUPS, SSM_STATE, SSM_GROUP_WIDTH), F32),
               pltpu.VMEM((ts, D_MODEL), BF16), pltpu.VMEM((ts, W), F32),
               pltpu.VMEM((ts, W), F32), pltpu.VMEM((ts, W), F32),
               pltpu.VMEM((ts, SSM_BC), F32), pltpu.VMEM((ts, SSM_BC), F32),
               pltpu.VMEM((ts, W), F32)]
    return _run_mixer(_ssm_kernel, x, layer, consts, W, scratch, ts)


def _merge_layer_consts(gain_mix, w_gate, w_branch, w_out, gain_ffn, w_ffn_in, w_ffn_out):
    return (_row(gain_mix), w_gate.astype(BF16), w_branch.astype(BF16), w_out.astype(BF16), _row(gain_ffn),
            w_ffn_in.astype(BF16), w_ffn_out.astype(BF16))


def _merge_ffn(x, ya, yb, yc, layer, layer_consts, gain_final, final_norm, ts):
    bsz, seq, _ = x.shape
    consts = layer_consts + [_row(gain_final)]
    seqs = [x, ya, yb, yc]
    return pl.pallas_call(
        functools.partial(_merge_kernel, final_norm=final_norm),
        grid=(bsz, seq // ts),
        in_specs=[_seq_spec(ts, a.shape[-1]) for a in seqs] + [_const_spec(c, layer) for c in consts],
        out_specs=_seq_spec(ts, D_MODEL),
        out_shape=jax.ShapeDtypeStruct((bsz, seq, D_MODEL), F32),
        compiler_params=_params(),
    )(*seqs, *[_const_value(c) for c in consts])


def kernel(x, norm_mix, w_in, rwkv_mu, rwkv_w0, rwkv_w_up, rwkv_a0, rwkv_a_up, rwkv_g_up, rwkv_k_k, rwkv_k_a,
           rwkv_r_k, rwkv_gn_w, rwkv_gn_b, hgrn_lb_logits, hgrn_gn_w, ssm_conv_w, ssm_conv_b, ssm_dt_bias,
           ssm_a_log, ssm_d, ssm_gn_w, w_branch, w_out, norm_ffn, w_ffn_in, w_ffn_out, norm_final):
    bsz, seq, d = x.shape
    assert d == D_MODEL
    depth = w_in.shape[0]
    ts = min(SEQ_BLOCK, seq)
    tm = min(MERGE_BLOCK, seq)
    assert seq % ts == 0 and seq % tm == 0
    assert all(ts % (CHUNK * n) == 0 for n in (LOOP_CHUNKS, HGRN_LOOP_CHUNKS, RWKV_LOOP_CHUNKS))

    rwkv_cols = 3 * RWKV_WIDTH + RWKV_DECAY_RANK + RWKV_ICL_RANK + RWKV_GATE_RANK
    off_hgrn = rwkv_cols
    off_ssm = off_hgrn + 4 * HGRN_WIDTH
    off_gate = off_ssm + SSM_WIDTH + SSM_CONV_DIM + SSM_HEADS

    cs = jnp.cumsum(jax.nn.softmax(hgrn_lb_logits.astype(F32), axis=0), axis=0)
    lbs = cs - cs[:1]

    rwkv_consts = _per_layer(_rwkv_layer_consts, norm_mix, w_in[:, :, :off_hgrn], rwkv_mu, rwkv_w0, rwkv_w_up,
                             rwkv_a0, rwkv_a_up, rwkv_g_up, rwkv_k_k, rwkv_k_a, rwkv_r_k, rwkv_gn_w, rwkv_gn_b)
    hgrn_consts = _per_layer(_hgrn_layer_consts, norm_mix, w_in[:, :, off_hgrn:off_ssm], lbs, hgrn_gn_w)
    ssm_consts = _per_layer(_ssm_layer_consts, norm_mix, w_in[:, :, off_ssm:off_gate], ssm_conv_w, ssm_conv_b,
                            ssm_dt_bias, ssm_a_log, jnp.repeat(ssm_d.astype(F32), SSM_HEAD_DIM, axis=1), ssm_gn_w)
    merge_consts = _per_layer(_merge_layer_consts, norm_mix, w_in[:, :, off_gate:], w_branch, w_out, norm_ffn,
                              w_ffn_in, w_ffn_out)

    x = x.astype(F32)
    for l in range(depth):
        ya = _rwkv_branch(x, l, rwkv_consts, ts)
        yb = _hgrn_branch(x, l, hgrn_consts, ts)
        yc = _ssm_branch(x, l, ssm_consts, ts)
        x = _merge_ffn(x, ya, yb, yc, l, merge_consts, norm_final, l == depth - 1, tm)
    return x
```

```python
import functools

import numpy as np
import jax
import jax.numpy as jnp
from jax import lax
from jax.experimental import pallas as pl
from jax.experimental.pallas import tpu as pltpu

F32 = jnp.float32
BF16 = jnp.bfloat16

D_MODEL = 1024
CHUNK = 64
LOOP_CHUNKS = 2
HGRN_LOOP_CHUNKS = 8
SUB = 16
HGRN_FAST_LOG_RANGE = 60.0
NORM_EPS = 1e-5
LANES = 128
SUBLANES = 8
MXU_TILE = 256
VMEM_LIMIT_BYTES = 56 * 1024 * 1024

RWKV_HEAD_DIM = 64
RWKV_WIDTH = D_MODEL
RWKV_DECAY_RANK = 64
RWKV_ICL_RANK = 64
RWKV_GATE_RANK = 160
RWKV_GATE_PAD = 256
RWKV_GN_EPS = 64e-5
RWKV_LOOP_CHUNKS = 8
RWKV_GROUP_LANES = MXU_TILE
RWKV_U_COLS = 3 * RWKV_WIDTH + LANES + RWKV_GATE_PAD

HGRN_HEAD_DIM = 128
HGRN_WIDTH = D_MODEL
HGRN_HEADS = HGRN_WIDTH // HGRN_HEAD_DIM

SSM_WIDTH = 2 * D_MODEL
SSM_HEAD_DIM = 64
SSM_HEADS = SSM_WIDTH // SSM_HEAD_DIM
SSM_GROUPS = 4
SSM_STATE = 128
SSM_CONV_WIDTH = 4
SSM_BC = SSM_GROUPS * SSM_STATE
SSM_CONV_DIM = SSM_WIDTH + 2 * SSM_BC
SSM_GROUP_WIDTH = SSM_WIDTH // SSM_GROUPS
SSM_U_COLS = SSM_WIDTH + SSM_CONV_DIM + LANES

FFN_HIDDEN = ((8 * D_MODEL + 3 * 256 - 1) // (3 * 256)) * 256

SEQ_BLOCK = 512
MERGE_BLOCK = 512
FFN_SPLIT = 6 * MXU_TILE


def _bf(x):
    return x if x.dtype == BF16 else x.astype(BF16)


def _dot(a, b):
    return jnp.dot(_bf(a), _bf(b), preferred_element_type=F32)


def _dot_nt(a, b):
    return lax.dot_general(_bf(a), _bf(b), (((1,), (1,)), ((), ())), preferred_element_type=F32)


def _dot_tn(a, b):
    return lax.dot_general(_bf(a), _bf(b), (((0,), (0,)), ((), ())), preferred_element_type=F32)


def _split(x, n):
    parts = []
    rest = x
    for i in range(n):
        p = rest.astype(BF16)
        parts.append(p)
        if i + 1 < n:
            rest = rest - p.astype(F32)
    return parts


def _dot_sel_left(sel, x, passes=3):
    out = None
    for p in _split(x, passes):
        t = jnp.dot(sel, p, preferred_element_type=F32)
        out = t if out is None else out + t
    return out


def _rmsnorm(x, gain):
    ms = jnp.mean(x * x, axis=-1, keepdims=True)
    return x * lax.rsqrt(ms + NORM_EPS) * gain


def _sigmoid(x):
    return 1.0 / (1.0 + jnp.exp(-x))


def _silu(x):
    return x * _sigmoid(x)


def _softplus(x):
    return jnp.maximum(x, 0.0) + jnp.log(1.0 + jnp.exp(-jnp.abs(x)))


def _iota(shape, axis):
    return lax.broadcasted_iota(jnp.int32, shape, axis)


def _interleave(*programs):
    live = list(programs)
    while live:
        for gen in list(live):
            try:
                next(gen)
            except StopIteration:
                live.remove(gen)


def _rwkv_kernel(x_ref, gain_ref, w_ref, mu_ref, wlr_ref, blr_ref, gup_ref, pv_ref, hsum_ref, tri_ref,
                 o_ref, ush_ref, st_ref, r_s, lw_s, k_s, v_s, ka_s, kb_s, o_s, g_s, *, ts):
    W = RWKV_WIDTH

    @pl.when(pl.program_id(1) == 0)
    def _():
        ush_ref[0:SUBLANES, :] = jnp.zeros((SUBLANES, RWKV_U_COLS), F32)
        st_ref[...] = jnp.zeros(st_ref.shape, F32)

    h = _bf(_rmsnorm(x_ref[...], gain_ref[...]))

    def project(lo, hi):
        ush_ref[SUBLANES:SUBLANES + ts, lo:hi] = _dot(h, w_ref[:, lo:hi])

    def token_shift(lo, hi):
        full = ush_ref[0:ts + SUBLANES, lo:hi]
        u = full[SUBLANES:, :]
        prev = pltpu.roll(full, 1, 0)[SUBLANES:, :]
        ush_ref[0:SUBLANES, lo:hi] = u[ts - SUBLANES:ts, :]
        return u + (prev - u) * mu_ref[:, lo:hi]

    project(3 * W, RWKV_U_COLS)
    project(W, 2 * W)
    low_rank = token_shift(3 * W, RWKV_U_COLS)
    xwa = low_rank[:, 0:LANES]
    xg = low_rank[:, LANES:]
    lane = _iota(xwa.shape, 1)
    lr_in = jnp.where(lane < RWKV_DECAY_RANK, jnp.tanh(xwa), xwa)
    lr = _dot(lr_in, wlr_ref[...]) + blr_ref[...]
    log_w = -jnp.exp(-_softplus(-lr[:, 0:W]) - 0.5)
    a = _sigmoid(lr[:, W:])
    g_s[...] = _dot(_sigmoid(xg), gup_ref[...])
    project(0, W)
    k = token_shift(W, 2 * W)

    k_k = pv_ref[0:1, :]
    k_a = pv_ref[1:2, :]
    r_k = pv_ref[2:3, :]
    gn_w = pv_ref[3:4, :]
    gn_b = pv_ref[4:5, :]
    hsum = hsum_ref[...]

    def head_sum(t):
        return jnp.concatenate([_dot(t[:, MXU_TILE * j:MXU_TILE * (j + 1)], hsum)
                                for j in range(W // MXU_TILE)], axis=1)

    kk = k * k_k
    ss = head_sum(kk * kk)
    kk = kk * lax.rsqrt(jnp.maximum(ss, 1e-24))
    kmod = k * (1.0 + (a - 1.0) * k_a)

    lw_s[...] = log_w
    k_s[...] = kmod
    ka_s[...] = -kk
    kb_s[...] = kk * a
    project(2 * W, 3 * W)
    r_s[...] = token_shift(0, W)
    v_s[...] = token_shift(2 * W, 3 * W)

    tri = tri_ref[...]
    gl = RWKV_GROUP_LANES
    n_groups = W // gl
    heads_per_group = gl // RWKV_HEAD_DIM
    row_w = _iota((CHUNK, gl), 0)
    col_w = _iota((CHUNK, gl), 1) & (RWKV_HEAD_DIM - 1)
    strict = row_w > col_w
    incl = row_w >= col_w
    eye_w = jnp.where(row_w == col_w, 1.0, 0.0)
    first_head = _iota((CHUNK, LANES), 1) < RWKV_HEAD_DIM
    head_shift = int(np.log2(RWKV_HEAD_DIM))
    same_head = (_iota((gl, gl), 0) >> head_shift) == (_iota((gl, gl), 1) >> head_shift)
    zeros_tile = jnp.zeros((CHUNK, LANES), F32)

    def blockdiag(t):
        blocks = []
        for hd in range(heads_per_group):
            tile = t[:, LANES * (hd // 2):LANES * (hd // 2 + 1)]
            kept = jnp.where(first_head, tile, 0.0) if hd % 2 == 0 else jnp.where(first_head, 0.0, tile)
            tiles = [kept if j == hd // 2 else zeros_tile for j in range(gl // LANES)]
            blocks.append(jnp.concatenate(tiles, axis=1))
        return jnp.concatenate(blocks, axis=0)

    lanes = [slice(gl * g, gl * (g + 1)) for g in range(n_groups)]

    def state_free(c, ctx):
        c0 = pl.multiple_of(c * CHUNK, CHUNK)
        sl = pl.ds(c0, CHUNK)
        lw = lw_s[sl, :]
        cum = _dot_sel_left(tri, lw, 2)
        e_last = jnp.exp(cum[CHUNK - 1:CHUNK, :])
        e_neg = jnp.exp(-cum)
        rt = r_s[sl, :] * jnp.exp(cum)
        at = ka_s[sl, :] * jnp.exp(cum - lw)
        bt = kb_s[sl, :] * e_neg
        kt = k_s[sl, :] * e_neg
        vv = v_s[sl, :]
        ar = [jnp.concatenate([at[:, ls], rt[:, ls]], axis=0) for ls in lanes]
        bk_bd = [jnp.concatenate([blockdiag(bt[:, ls]), blockdiag(kt[:, ls])], axis=0) for ls in lanes]
        v_bd = [blockdiag(vv[:, ls]) for ls in lanes]
        bk_end = [jnp.concatenate([bt[:, ls], kt[:, ls]], axis=0) * e_last[:, ls] for ls in lanes]
        yield
        gram = [_dot_nt(a, b) for a, b in zip(ar, bk_bd)]
        n_ab = [jnp.where(strict, g[0:CHUNK, 0:gl], 0.0) for g in gram]
        n_ak = [jnp.where(strict, g[0:CHUNK, gl:], 0.0) for g in gram]
        n_rbk = [jnp.concatenate([jnp.where(incl, g[CHUNK:, 0:gl], 0.0), jnp.where(incl, g[CHUNK:, gl:], 0.0)], axis=1)
                 for g in gram]
        yield
        ak_v = [_dot(n, v) for n, v in zip(n_ak, v_bd)]
        prod = [eye_w + n for n in n_ab]
        npow = [_dot(n, blockdiag(n)) for n in n_ab]
        for _ in range(int(np.log2(CHUNK)) - 2):
            yield
            both = [_dot(jnp.concatenate([nk, pr], axis=0), blockdiag(nk)) for nk, pr in zip(npow, prod)]
            prod = [pr + b[CHUNK:] for pr, b in zip(prod, both)]
            npow = [b[0:CHUNK] for b in both]
        yield
        inv = [pr + _dot(pr, blockdiag(nk)) for pr, nk in zip(prod, npow)]
        ctx.update(sl=sl, e_last=e_last, ar=ar, vv=vv, v_bd=v_bd, bk_end=bk_end, ak_v=ak_v, n_rbk=n_rbk, inv=inv)

    def state_bound(ctx, states):
        sl, e_last = ctx["sl"], ctx["e_last"]
        from_state = [_dot_nt(a, s) for a, s in zip(ctx["ar"], states)]
        yield
        u = [_dot(t, blockdiag(f[0:CHUNK] + r)) for t, f, r in zip(ctx["inv"], from_state, ctx["ak_v"])]
        yield
        for g, ls in enumerate(lanes):
            uv = jnp.concatenate([u[g], ctx["vv"][:, ls]], axis=0)
            states[g] = states[g] * e_last[:, ls] + jnp.where(same_head, _dot_tn(uv, ctx["bk_end"][g]), 0.0)
        yield
        for g, ls in enumerate(lanes):
            o_s[sl, ls] = from_state[g][CHUNK:] + _dot(ctx["n_rbk"][g],
                                                       jnp.concatenate([blockdiag(u[g]), ctx["v_bd"][g]], axis=0))

    def chained(*programs):
        for program in programs:
            yield from program

    def chunk_body(i, carry):
        n = RWKV_LOOP_CHUNKS
        states = [st_ref[g] for g in range(n_groups)]
        ctxs = [dict() for _ in range(n)]
        pending = []
        for j in range(0, n, 2):
            frees = [state_free(n * i + j + d, ctxs[j + d]) for d in range(2)]
            _interleave(*frees, *([chained(*pending)] if pending else []))
            pending = [state_bound(ctxs[j + d], states) for d in range(2)]
        _interleave(chained(*pending))
        for g in range(n_groups):
            st_ref[g] = states[g]
        return carry

    lax.fori_loop(0, ts // (CHUNK * RWKV_LOOP_CHUNKS), chunk_body, 0)

    o = o_s[...]
    inv_n = 1.0 / RWKV_HEAD_DIM
    mean = head_sum(o) * inv_n
    cen = o - mean
    var = head_sum(cen * cen) * inv_n
    o = cen * lax.rsqrt(var + RWKV_GN_EPS) * gn_w + gn_b
    bonus = head_sum(r_s[...] * k_s[...] * r_k)
    o = o + bonus * v_s[...]
    o_ref[...] = (o * g_s[...]).astype(o_ref.dtype)


def _hgrn_kernel(x_ref, gain_ref, w_ref, pv_ref, tri_ref, ones_ref, sel_ref, bsel_ref,
                 o_ref, st_ref, h_s, q_s, k_s, v_s, lf_s, o_s, g_s, *, ts):
    W = HGRN_WIDTH
    nsub = CHUNK // SUB

    @pl.when(pl.program_id(1) == 0)
    def _():
        st_ref[...] = jnp.zeros(st_ref.shape, F32)

    h = _bf(_rmsnorm(x_ref[...], gain_ref[...]))
    h_s[...] = h
    log_lb = pv_ref[0:1, :]
    log1m_lb = pv_ref[1:2, :]
    gn_w = pv_ref[2:3, :]
    f_pre = _dot(h, w_ref[:, W:2 * W])
    q_pre = _dot(h, w_ref[:, 0:W])
    e = jnp.exp(-jnp.abs(f_pre))
    b = log1m_lb - (jnp.maximum(-f_pre, 0.0) + jnp.log(1.0 + e))
    mx = jnp.maximum(log_lb, b)
    log_f = mx + jnp.log(1.0 + jnp.exp(-jnp.abs(log_lb - b)))
    lf_s[...] = log_f
    k_s[...] = pv_ref[3:4, :] * jnp.where(f_pre > 0.0, e, 1.0) / (1.0 + e)
    v_s[...] = _dot(h, w_ref[:, 2 * W:3 * W])
    q_s[...] = _silu(q_pre)

    tri = tri_ref[...]
    ones_sq = ones_ref[...]
    sel = sel_ref[...]
    sub_shift = int(np.log2(SUB))
    rblk = _iota((CHUNK, CHUNK), 0) >> sub_shift
    cblk = _iota((CHUNK, CHUNK), 1) >> sub_shift
    off_mask = cblk < rblk
    rowblk = _iota((CHUNK, LANES), 0) >> sub_shift
    s_idx = _iota((SUB, LANES), 0)

    heads = range(HGRN_HEADS)
    lanes = [slice(LANES * hd, LANES * (hd + 1)) for hd in heads]
    causal = _iota((CHUNK, CHUNK), 1) <= _iota((CHUNK, CHUNK), 0)

    def fast_chunk(c, states):
        c0 = pl.multiple_of(c * CHUNK, CHUNK)
        sl = pl.ds(c0, CHUNK)
        cum = _dot_sel_left(tri, lf_s[sl, :], 3)
        q = q_s[sl, :]
        k = k_s[sl, :]
        v = v_s[sl, :]
        yield
        starts = [None] + [cum[SUB * i - 1:SUB * i, :] for i in range(1, nsub)]
        cs = jnp.zeros_like(cum)
        rowblk_w = _iota(cum.shape, 0) >> sub_shift
        for i in range(1, nsub):
            cs = jnp.where(rowblk_w == i, starts[i], cs)
        q_sub = q * jnp.exp(cum - cs)
        k_own = k * jnp.exp(cs - cum)
        k_prev = [None] + [k[0:SUB * i, :] * jnp.exp(jnp.minimum(starts[i] - cum[0:SUB * i, :], 0.0))
                           for i in range(1, nsub)]
        yield
        last = cum[CHUNK - 1:CHUNK, :]
        q_in = q * jnp.exp(cum)
        k_end = k * jnp.exp(last - cum)
        e_last = jnp.exp(last)
        before = list(states)
        for hd, ls in enumerate(lanes):
            states[hd] = before[hd] * e_last[:, ls] + _dot_tn(v[:, ls], k_end[:, ls])
        yield
        scores = []
        for n, ls in enumerate(lanes):
            rows = []
            for i in range(nsub):
                parts = [k_own[SUB * i:SUB * (i + 1), ls]]
                if i > 0:
                    parts = [k_prev[i][:, ls]] + parts
                if i + 1 < nsub:
                    parts.append(jnp.zeros((CHUNK - SUB * (i + 1), LANES), F32))
                rows.append(_dot_nt(q_sub[SUB * i:SUB * (i + 1), ls], jnp.concatenate(parts, axis=0)))
            scores.append(jnp.where(causal, jnp.concatenate(rows, axis=0), 0.0))
            if n % 4 == 3:
                yield
        outs = [_dot(a, v[:, ls]) + _dot_nt(q_in[:, ls], s) for a, ls, s in zip(scores, lanes, before)]
        yield
        for hd, ls in enumerate(lanes):
            o_s[sl, ls] = outs[hd]

    def gate_rows(r0, rows):
        hr = h_s[pl.ds(r0, rows), :]
        for j in range(W // MXU_TILE):
            cols = slice(MXU_TILE * j, MXU_TILE * (j + 1))
            g_s[pl.ds(r0, rows), cols] = _sigmoid(_dot(hr, w_ref[:, 3 * W + MXU_TILE * j:3 * W + MXU_TILE * (j + 1)]))
            yield

    def fast_chunk_body(i, carry):
        states = [st_ref[hd] for hd in heads]
        rows = CHUNK * HGRN_LOOP_CHUNKS
        _interleave(*[fast_chunk(HGRN_LOOP_CHUNKS * i + j, states) for j in range(HGRN_LOOP_CHUNKS)],
                    gate_rows(pl.multiple_of(i * rows, rows), rows))
        for hd in heads:
            st_ref[hd] = states[hd]
        return carry

    def chunk_body(c, carry):
        c0 = pl.multiple_of(c * CHUNK, CHUNK)
        sl = pl.ds(c0, CHUNK)
        cum_all = _dot_sel_left(tri, lf_s[sl, :], 3)
        q_all = q_s[sl, :]
        k_all = k_s[sl, :]
        v_all = v_s[sl, :]
        for hd in range(HGRN_HEADS):
            ls = slice(LANES * hd, LANES * (hd + 1))
            cum = cum_all[:, ls]
            q = q_all[:, ls]
            k = k_all[:, ls]
            v = v_all[:, ls]
            vb = _bf(v)
            last = cum[CHUNK - 1:CHUNK, :]
            starts = [None] + [cum[SUB * i - 1:SUB * i, :] for i in range(1, nsub)]
            cs = jnp.zeros_like(cum)
            for i in range(1, nsub):
                cs = jnp.where(rowblk == i, starts[i], cs)
            q_sub = q * jnp.exp(cum - cs)
            rows = [jnp.zeros((SUB, CHUNK), F32)]
            for i in range(1, nsub):
                k_i = k * jnp.exp(jnp.minimum(starts[i] - cum, 0.0))
                rows.append(_dot_nt(q_sub[SUB * i:SUB * (i + 1), :], k_i))
            a_off = jnp.where(off_mask, jnp.concatenate(rows, axis=0), 0.0)
            state = st_ref[hd]
            o = _dot(a_off, vb) + _dot_nt(q * jnp.exp(cum), state)
            diag = []
            for j in range(nsub):
                rs = slice(SUB * j, SUB * (j + 1))
                cum_b = cum[rs, :]
                q_b = q[rs, :]
                k_b = k[rs, :]
                pieces = []
                for t in range(SUB):
                    expo = jnp.where(s_idx <= t, cum_b[t:t + 1, :] - cum_b, -1e30)
                    pieces.append(jnp.exp(expo) * k_b * q_b[t:t + 1, :])
                pmat = jnp.concatenate(pieces, axis=0)
                score = _dot(pmat, ones_sq)
                wv = score * jnp.concatenate([v[rs, :]] * SUB, axis=0)
                diag.append(_dot(sel, wv))
            o = o + jnp.concatenate(diag, axis=0)
            o_s[sl, ls] = o
            k_end = k * jnp.exp(last - cum)
            st_ref[hd] = state * jnp.exp(last) + _dot_tn(vb, k_end)
        return carry

    sub_sums = _dot_sel_left(bsel_ref[...], log_f, 3)
    fast_ok = jnp.min(sub_sums) >= -HGRN_FAST_LOG_RANGE

    @pl.when(fast_ok)
    def _():
        lax.fori_loop(0, ts // (CHUNK * HGRN_LOOP_CHUNKS), fast_chunk_body, 0)

    @pl.when(jnp.logical_not(fast_ok))
    def _():
        lax.fori_loop(0, ts // CHUNK, chunk_body, 0)
        _interleave(gate_rows(0, ts))

    for hd in range(HGRN_HEADS):
        ls = slice(LANES * hd, LANES * (hd + 1))
        o = o_s[:, ls]
        o = o * lax.rsqrt(jnp.mean(o * o, axis=-1, keepdims=True) + NORM_EPS) * gn_w[:, ls]
        o_ref[:, ls] = (o * g_s[:, ls]).astype(o_ref.dtype)


def _ssm_kernel(x_ref, gain_ref, w_ref, cw_ref, cb_ref, hv_ref, wv_ref, tri_ref,
                o_ref, xb_ref, st_ref, h_s, xs_s, ec_s, xd_s, b_s, c_s, y_s, *, ts):
    W = SSM_WIDTH
    gw = SSM_GROUP_WIDTH
    pairs = W // LANES

    @pl.when(pl.program_id(1) == 0)
    def _():
        xb_ref[0:SUBLANES, :] = jnp.zeros((SUBLANES, SSM_CONV_DIM), F32)
        st_ref[...] = jnp.zeros(st_ref.shape, F32)

    h = _bf(_rmsnorm(x_ref[...], gain_ref[...]))
    h_s[...] = h

    dt_bias = hv_ref[0:1, :]
    neg_a = hv_ref[1:2, :]
    dt = _softplus(_dot(h, w_ref[:, W + SSM_CONV_DIM:]) + dt_bias)
    log_a = dt * neg_a
    tri = tri_ref[...]
    cum = jnp.concatenate([_dot_sel_left(tri, log_a[CHUNK * c:CHUNK * (c + 1), :], 3)
                           for c in range(ts // CHUNK)], axis=0)

    first_head_rows = _iota((ts, LANES), 1) < SSM_HEAD_DIM
    heads_per_group = gw // SSM_HEAD_DIM

    def expand_heads(v, g):
        tiles = []
        for j in range(gw // LANES):
            h0 = heads_per_group * g + 2 * j
            a = jnp.broadcast_to(v[:, h0:h0 + 1], (ts, LANES))
            b = jnp.broadcast_to(v[:, h0 + 1:h0 + 2], (ts, LANES))
            tiles.append(jnp.where(first_head_rows, a, b))
        return jnp.concatenate(tiles, axis=1)

    def project(lo, hi):
        xb_ref[SUBLANES:SUBLANES + ts, lo:hi] = _dot(h, w_ref[:, W + lo:W + hi])

    def conv_silu(lo, hi):
        full = xb_ref[0:ts + SUBLANES, lo:hi]
        conv = cb_ref[:, lo:hi] + full[SUBLANES:, :] * cw_ref[SSM_CONV_WIDTH - 1:SSM_CONV_WIDTH, lo:hi]
        for k in range(1, SSM_CONV_WIDTH):
            j = SSM_CONV_WIDTH - 1 - k
            conv = conv + pltpu.roll(full, k, 0)[SUBLANES:, :] * cw_ref[j:j + 1, lo:hi]
        xb_ref[0:SUBLANES, lo:hi] = xb_ref[ts:ts + SUBLANES, lo:hi]
        return _silu(conv)

    def finish(idx):
        lo, hi = slabs[idx]
        act = conv_silu(lo, hi)
        if idx < SSM_GROUPS:
            xs_s[:, lo:hi] = act
            xd_s[:, lo:hi] = act * expand_heads(dt, idx)
            ec_s[:, lo:hi] = expand_heads(cum, idx)
        elif idx == SSM_GROUPS:
            b_s[...] = act
        else:
            c_s[...] = act

    slabs = [(gw * g, gw * (g + 1)) for g in range(SSM_GROUPS)] + [(W, W + SSM_BC), (W + SSM_BC, W + 2 * SSM_BC)]
    project(*slabs[0])
    for idx in range(len(slabs)):
        if idx + 1 < len(slabs):
            project(*slabs[idx + 1])
        finish(idx)

    t_idx = _iota((CHUNK, W), 0)
    s_idx = _iota((CHUNK, W), 1) & (CHUNK - 1)
    causal = s_idx <= t_idx
    on_diag = s_idx == t_idx
    first_head = (_iota((CHUNK, LANES), 1) < SSM_HEAD_DIM)

    groups = range(SSM_GROUPS)
    glanes = [slice(gw * g, gw * (g + 1)) for g in groups]
    per_group = gw // LANES

    def chunk(c, states):
        c0 = pl.multiple_of(c * CHUNK, CHUNK)
        sl = pl.ds(c0, CHUNK)
        ec = ec_s[sl, :]
        xd = xd_s[sl, :]
        bm = b_s[sl, :]
        cm = c_s[sl, :]
        last = ec[CHUNK - 1:CHUNK, :]
        by_src = jnp.sum(jnp.where(on_diag, ec, 0.0), axis=0, keepdims=True)
        decay = jnp.exp(jnp.where(causal, ec - by_src, -1e30))
        yield
        xw = xd * jnp.exp(last - ec)
        e_in = jnp.exp(ec)
        e_last = jnp.exp(last)
        c_g = [_bf(cm[:, SSM_STATE * g:SSM_STATE * (g + 1)]) for g in groups]
        b_g = [_bf(bm[:, SSM_STATE * g:SSM_STATE * (g + 1)]) for g in groups]
        cb2 = [_dot_nt(c, jnp.concatenate([b, b], axis=0)) for c, b in zip(c_g, b_g)]
        yield
        before = list(states)
        for g, gl in enumerate(glanes):
            states[g] = before[g] * e_last[:, gl] + _dot_tn(b_g[g], xw[:, gl])
        y_in = [_dot(c, s) for c, s in zip(c_g, before)]
        yield
        ys = []
        for p in range(pairs):
            ls = slice(LANES * p, LANES * (p + 1))
            m = decay[:, ls] * cb2[(LANES * p) // gw]
            xp = xd[:, ls]
            x2 = jnp.concatenate([jnp.where(first_head, xp, 0.0), jnp.where(first_head, 0.0, xp)], axis=0)
            ys.append(_dot(m, x2))
            if p % 4 == 3:
                yield
        for g, gl in enumerate(glanes):
            y_s[sl, gl] = (y_in[g] * e_in[:, gl]
                           + jnp.concatenate(ys[per_group * g:per_group * (g + 1)], axis=1))

    def chunk_body(i, carry):
        states = [st_ref[g] for g in groups]
        _interleave(*[chunk(LOOP_CHUNKS * i + j, states) for j in range(LOOP_CHUNKS)])
        for g in groups:
            st_ref[g] = states[g]
        return carry

    lax.fori_loop(0, ts // (CHUNK * LOOP_CHUNKS), chunk_body, 0)

    d_skip = wv_ref[0:1, :]
    gn_w = wv_ref[1:2, :]
    h = h_s[...]
    for g in range(SSM_GROUPS):
        gl = slice(gw * g, gw * (g + 1))
        z = _dot(h, w_ref[:, gl])
        yg = (y_s[:, gl] + d_skip[:, gl] * xs_s[:, gl]) * _silu(z)
        yg = yg * lax.rsqrt(jnp.mean(yg * yg, axis=-1, keepdims=True) + NORM_EPS)
        o_ref[:, gl] = (yg * gn_w[:, gl]).astype(o_ref.dtype)


def _merge_kernel(x_ref, ya_ref, yb_ref, yc_ref, gmix_ref, wg_ref, wb_ref, wout_ref,
                  gffn_ref, wfi_ref, wfo_ref, gfin_ref, o_ref, *, final_norm):
    x = x_ref[...]
    h = _bf(_rmsnorm(x, gmix_ref[...]))
    rows_b = RWKV_WIDTH + HGRN_WIDTH
    merged = (_sigmoid(_dot(h, wg_ref[:, 0:D_MODEL]))
              * jnp.dot(ya_ref[...], wb_ref[0:RWKV_WIDTH, :], preferred_element_type=F32))
    merged = merged + (_sigmoid(_dot(h, wg_ref[:, D_MODEL:2 * D_MODEL]))
                       * jnp.dot(yb_ref[...], wb_ref[RWKV_WIDTH:rows_b, :], preferred_element_type=F32))
    merged = merged + (_sigmoid(_dot(h, wg_ref[:, 2 * D_MODEL:]))
                       * jnp.dot(yc_ref[...], wb_ref[rows_b:, :], preferred_element_type=F32))
    x = x + _dot(merged, wout_ref[...])
    h = _bf(_rmsnorm(x, gffn_ref[...]))
    for lo, hi in ((0, FFN_SPLIT), (FFN_SPLIT, FFN_HIDDEN)):
        gate = _dot(h, wfi_ref[:, lo:hi])
        up = _dot(h, wfi_ref[:, FFN_HIDDEN + lo:FFN_HIDDEN + hi])
        x = x + _dot(_silu(gate) * up, wfo_ref[lo:hi, :])
    if final_norm:
        x = _rmsnorm(x, gfin_ref[...])
    o_ref[...] = x


class _PerLayer:
    def __init__(self, value):
        self.value = value


def _const_spec(const, layer):
    if isinstance(const, _PerLayer):
        shape = const.value.shape[1:]
        return pl.BlockSpec((None,) + shape, lambda b, s, _l=layer, _nd=len(shape): (_l,) + (0,) * _nd,
                            pipeline_mode=pl.Buffered(1))
    return pl.BlockSpec(const.shape, lambda b, s, _nd=const.ndim: (0,) * _nd, pipeline_mode=pl.Buffered(1))


def _const_value(const):
    return const.value if isinstance(const, _PerLayer) else const


def _per_layer(fn, *stacked_args):
    return [_PerLayer(v) for v in jax.vmap(fn)(*stacked_args)]


def _seq_spec(ts, width):
    return pl.BlockSpec((None, ts, width), lambda b, s: (b, s, 0))


def _params():
    return pltpu.CompilerParams(dimension_semantics=("arbitrary", "arbitrary"),
                                vmem_limit_bytes=VMEM_LIMIT_BYTES)


def _run_mixer(kernel_fn, x, layer, consts, out_width, scratch, ts):
    bsz, seq, _ = x.shape
    return pl.pallas_call(
        functools.partial(kernel_fn, ts=ts),
        grid=(bsz, seq // ts),
        in_specs=[_seq_spec(ts, D_MODEL)] + [_const_spec(c, layer) for c in consts],
        out_specs=_seq_spec(ts, out_width),
        out_shape=jax.ShapeDtypeStruct((bsz, seq, out_width), BF16),
        scratch_shapes=scratch,
        compiler_params=_params(),
    )(x, *[_const_value(c) for c in consts])


def _row(v):
    return v.astype(F32).reshape(1, -1)


def _pad_rows(m, rows):
    return jnp.concatenate([m, jnp.zeros((rows - m.shape[0], m.shape[1]), m.dtype)], axis=0)


def _pad_cols(m, width):
    return jnp.concatenate([m, jnp.zeros((m.shape[0], width - m.shape[1]), m.dtype)], axis=1)


def _rows8(rows, width):
    m = jnp.concatenate([_row(r) for r in rows], axis=0)
    return _pad_rows(_pad_cols(m, width), SUBLANES)


def _head_sum_matrix(width, head_dim):
    idx = np.arange(width) // head_dim
    return jnp.asarray(idx[:, None] == idx[None, :], dtype=BF16)


def _tri_ones(n):
    return jnp.asarray(np.tril(np.ones((n, n), np.float32)), dtype=BF16)


def _rwkv_layer_consts(gain, w_in, mu, w0, w_up, a0, a_up, g_up, k_k, k_a, r_k, gn_w, gn_b):
    W = RWKV_WIDTH
    dr, ir, gr = RWKV_DECAY_RANK, RWKV_ICL_RANK, RWKV_GATE_RANK
    w = _pad_cols(w_in, RWKV_U_COLS).astype(BF16)
    mu_p = _pad_cols(_row(mu), RWKV_U_COLS)
    wlr = jnp.concatenate([jnp.concatenate([w_up, jnp.zeros((dr, W), w_up.dtype)], axis=1),
                           jnp.concatenate([jnp.zeros((ir, W), a_up.dtype), a_up], axis=1)], axis=0)
    wlr = _pad_rows(wlr, LANES).astype(BF16)
    blr = jnp.concatenate([_row(w0), _row(a0)], axis=1)
    gup = _pad_rows(g_up, RWKV_GATE_PAD).astype(BF16)
    pv = _rows8([k_k, k_a, r_k.reshape(-1), gn_w, gn_b], W)
    return _row(gain), w, mu_p, wlr, blr, gup, pv


def _rwkv_branch(x, layer, layer_consts, ts):
    W = RWKV_WIDTH
    consts = layer_consts + [_head_sum_matrix(MXU_TILE, RWKV_HEAD_DIM), _tri_ones(CHUNK)]
    scratch = ([pltpu.VMEM((ts + SUBLANES, RWKV_U_COLS), F32),
                pltpu.VMEM((W // RWKV_GROUP_LANES, RWKV_GROUP_LANES, RWKV_GROUP_LANES), F32)]
               + [pltpu.VMEM((ts, W), F32) for _ in range(8)])
    return _run_mixer(_rwkv_kernel, x, layer, consts, W, scratch, ts)


def _hgrn_layer_consts(gain, w_in, lb, gn_w):
    lb = lb.astype(F32)
    pv = _rows8([jnp.log(lb), jnp.log1p(-lb), gn_w, 1.0 - lb], HGRN_WIDTH)
    return _row(gain), w_in.astype(BF16), pv


def _hgrn_branch(x, layer, layer_consts, ts):
    W = HGRN_WIDTH
    t_i = np.arange(SUB)[:, None]
    pair = np.arange(SUB * SUB)[None, :]
    sel = jnp.asarray((pair // SUB == t_i) & (pair % SUB <= t_i), dtype=BF16)
    bsel = jnp.asarray(np.arange(ts // SUB)[:, None] == (np.arange(ts) // SUB)[None, :], dtype=BF16)
    consts = layer_consts + [_tri_ones(CHUNK), jnp.ones((LANES, LANES), BF16), sel, bsel]
    scratch = ([pltpu.VMEM((HGRN_HEADS, HGRN_HEAD_DIM, HGRN_HEAD_DIM), F32), pltpu.VMEM((ts, D_MODEL), BF16)]
               + [pltpu.VMEM((ts, W), F32) for _ in range(6)])
    return _run_mixer(_hgrn_kernel, x, layer, consts, W, scratch, ts)


def _ssm_layer_consts(gain, w_in, conv_w, conv_b, dt_bias, a_log, d_skip_lanes, gn_w):
    w = _pad_cols(w_in, SSM_U_COLS).astype(BF16)
    cw = _pad_rows(conv_w.astype(F32).T, SUBLANES)
    hv = _rows8([dt_bias, -jnp.exp(a_log.astype(F32))], LANES)
    wv = _rows8([d_skip_lanes, gn_w], SSM_WIDTH)
    return _row(gain), w, cw, _row(conv_b), hv, wv


def _ssm_branch(x, layer, layer_consts, ts):
    W = SSM_WIDTH
    consts = layer_consts + [_tri_ones(CHUNK)]
    scratch = [pltpu.VMEM((ts + SUBLANES, SSM_CONV_DIM), F32),
               pltpu.VMEM((SSM_GROUPS, SSM_STATE, SSM_GROUP_WIDTH), F32),
               pltpu.VMEM((ts, D_MODEL), BF16), pltpu.VMEM((ts, W), F32),
               pltpu.VMEM((ts, W), F32), pltpu.VMEM((ts, W), F32),
               pltpu.VMEM((ts, SSM_BC), F32), pltpu.VMEM((ts, SSM_BC), F32),
               pltpu.VMEM((ts, W), F32)]
    return _run_mixer(_ssm_kernel, x, layer, consts, W, scratch, ts)


def _merge_layer_consts(gain_mix, w_gate, w_branch, w_out, gain_ffn, w_ffn_in, w_ffn_out):
    return (_row(gain_mix), w_gate.astype(BF16), w_branch.astype(BF16), w_out.astype(BF16), _row(gain_ffn),
            w_ffn_in.astype(BF16), w_ffn_out.astype(BF16))


def _merge_ffn(x, ya, yb, yc, layer, layer_consts, gain_final, final_norm, ts):
    bsz, seq, _ = x.shape
    consts = layer_consts + [_row(gain_final)]
    seqs = [x, ya, yb, yc]
    return pl.pallas_call(
        functools.partial(_merge_kernel, final_norm=final_norm),
        grid=(bsz, seq // ts),
        in_specs=[_seq_spec(ts, a.shape[-1]) for a in seqs] + [_const_spec(c, layer) for c in consts],
        out_specs=_seq_spec(ts, D_MODEL),
        out_shape=jax.ShapeDtypeStruct((bsz, seq, D_MODEL), F32),
        compiler_params=_params(),
    )(*seqs, *[_const_value(c) for c in consts])


def kernel(x, norm_mix, w_in, rwkv_mu, rwkv_w0, rwkv_w_up, rwkv_a0, rwkv_a_up, rwkv_g_up, rwkv_k_k, rwkv_k_a,
           rwkv_r_k, rwkv_gn_w, rwkv_gn_b, hgrn_lb_logits, hgrn_gn_w, ssm_conv_w, ssm_conv_b, ssm_dt_bias,
           ssm_a_log, ssm_d, ssm_gn_w, w_branch, w_out, norm_ffn, w_ffn_in, w_ffn_out, norm_final):
    bsz, seq, d = x.shape
    assert d == D_MODEL
    depth = w_in.shape[0]
    ts = min(SEQ_BLOCK, seq)
    tm = min(MERGE_BLOCK, seq)
    assert seq % ts == 0 and seq % tm == 0
    assert all(ts % (CHUNK * n) == 0 for n in (LOOP_CHUNKS, HGRN_LOOP_CHUNKS, RWKV_LOOP_CHUNKS))

    rwkv_cols = 3 * RWKV_WIDTH + RWKV_DECAY_RANK + RWKV_ICL_RANK + RWKV_GATE_RANK
    off_hgrn = rwkv_cols
    off_ssm = off_hgrn + 4 * HGRN_WIDTH
    off_gate = off_ssm + SSM_WIDTH + SSM_CONV_DIM + SSM_HEADS

    cs = jnp.cumsum(jax.nn.softmax(hgrn_lb_logits.astype(F32), axis=0), axis=0)
    lbs = cs - cs[:1]

    rwkv_consts = _per_layer(_rwkv_layer_consts, norm_mix, w_in[:, :, :off_hgrn], rwkv_mu, rwkv_w0, rwkv_w_up,
                             rwkv_a0, rwkv_a_up, rwkv_g_up, rwkv_k_k, rwkv_k_a, rwkv_r_k, rwkv_gn_w, rwkv_gn_b)
    hgrn_consts = _per_layer(_hgrn_layer_consts, norm_mix, w_in[:, :, off_hgrn:off_ssm], lbs, hgrn_gn_w)
    ssm_consts = _per_layer(_ssm_layer_consts, norm_mix, w_in[:, :, off_ssm:off_gate], ssm_conv_w, ssm_conv_b,
                            ssm_dt_bias, ssm_a_log, jnp.repeat(ssm_d.astype(F32), SSM_HEAD_DIM, axis=1), ssm_gn_w)
    merge_consts = _per_layer(_merge_layer_consts, norm_mix, w_in[:, :, off_gate:], w_branch, w_out, norm_ffn,
                              w_ffn_in, w_ffn_out)

    x = x.astype(F32)
    for l in range(depth):
        ya = _rwkv_branch(x, l, rwkv_consts, ts)
        yb = _hgrn_branch(x, l, hgrn_consts, ts)
        yc = _ssm_branch(x, l, ssm_consts, ts)
        x = _merge_ffn(x, ya, yb, yc, l, merge_consts, norm_final, l == depth - 1, tm)
    return x
```

```python
import functools

import numpy as np
import jax
import jax.numpy as jnp
from jax import lax
from jax.experimental import pallas as pl
from jax.experimental.pallas import tpu as pltpu

F32 = jnp.float32
BF16 = jnp.bfloat16

D_MODEL = 1024
CHUNK = 64
LOOP_CHUNKS = 2
HGRN_LOOP_CHUNKS = 8
SUB = 16
HGRN_FAST_LOG_RANGE = 60.0
NORM_EPS = 1e-5
LANES = 128
SUBLANES = 8
MXU_TILE = 256
VMEM_LIMIT_BYTES = 56 * 1024 * 1024

RWKV_HEAD_DIM = 64
RWKV_WIDTH = D_MODEL
RWKV_DECAY_RANK = 64
RWKV_ICL_RANK = 64
RWKV_GATE_RANK = 160
RWKV_GATE_PAD = 256
RWKV_GN_EPS = 64e-5
RWKV_LOOP_CHUNKS = 8
RWKV_GROUP_LANES = MXU_TILE
RWKV_U_COLS = 3 * RWKV_WIDTH + LANES + RWKV_GATE_PAD

HGRN_HEAD_DIM = 128
HGRN_WIDTH = D_MODEL
HGRN_HEADS = HGRN_WIDTH // HGRN_HEAD_DIM

SSM_WIDTH = 2 * D_MODEL
SSM_HEAD_DIM = 64
SSM_HEADS = SSM_WIDTH // SSM_HEAD_DIM
SSM_GROUPS = 4
SSM_STATE = 128
SSM_CONV_WIDTH = 4
SSM_BC = SSM_GROUPS * SSM_STATE
SSM_CONV_DIM = SSM_WIDTH + 2 * SSM_BC
SSM_GROUP_WIDTH = SSM_WIDTH // SSM_GROUPS
SSM_U_COLS = SSM_WIDTH + SSM_CONV_DIM + LANES

FFN_HIDDEN = ((8 * D_MODEL + 3 * 256 - 1) // (3 * 256)) * 256

SEQ_BLOCK = 512
MERGE_BLOCK = 512
FFN_SPLIT = 6 * MXU_TILE


def _bf(x):
    return x if x.dtype == BF16 else x.astype(BF16)


def _dot(a, b):
    return jnp.dot(_bf(a), _bf(b), preferred_element_type=F32)


def _dot_nt(a, b):
    return lax.dot_general(_bf(a), _bf(b), (((1,), (1,)), ((), ())), preferred_element_type=F32)


def _dot_tn(a, b):
    return lax.dot_general(_bf(a), _bf(b), (((0,), (0,)), ((), ())), preferred_element_type=F32)


def _split(x, n):
    parts = []
    rest = x
    for i in range(n):
        p = rest.astype(BF16)
        parts.append(p)
        if i + 1 < n:
            rest = rest - p.astype(F32)
    return parts


def _dot_sel_left(sel, x, passes=3):
    out = None
    for p in _split(x, passes):
        t = jnp.dot(sel, p, preferred_element_type=F32)
        out = t if out is None else out + t
    return out


def _rmsnorm(x, gain):
    ms = jnp.mean(x * x, axis=-1, keepdims=True)
    return x * lax.rsqrt(ms + NORM_EPS) * gain


def _sigmoid(x):
    return 1.0 / (1.0 + jnp.exp(-x))


def _silu(x):
    return x * _sigmoid(x)


def _softplus(x):
    return jnp.maximum(x, 0.0) + jnp.log(1.0 + jnp.exp(-jnp.abs(x)))


def _iota(shape, axis):
    return lax.broadcasted_iota(jnp.int32, shape, axis)


def _interleave(*programs):
    live = list(programs)
    while live:
        for gen in list(live):
            try:
                next(gen)
            except StopIteration:
                live.remove(gen)


def _rwkv_kernel(x_ref, gain_ref, w_ref, mu_ref, wlr_ref, blr_ref, gup_ref, pv_ref, hsum_ref, tri_ref,
                 o_ref, ush_ref, st_ref, r_s, lw_s, k_s, v_s, ka_s, kb_s, o_s, g_s, *, ts):
    W = RWKV_WIDTH

    @pl.when(pl.program_id(1) == 0)
    def _():
        ush_ref[0:SUBLANES, :] = jnp.zeros((SUBLANES, RWKV_U_COLS), F32)
        st_ref[...] = jnp.zeros(st_ref.shape, F32)

    h = _bf(_rmsnorm(x_ref[...], gain_ref[...]))

    def project(lo, hi):
        ush_ref[SUBLANES:SUBLANES + ts, lo:hi] = _dot(h, w_ref[:, lo:hi])

    def token_shift(lo, hi):
        full = ush_ref[0:ts + SUBLANES, lo:hi]
        u = full[SUBLANES:, :]
        prev = pltpu.roll(full, 1, 0)[SUBLANES:, :]
        ush_ref[0:SUBLANES, lo:hi] = u[ts - SUBLANES:ts, :]
        return u + (prev - u) * mu_ref[:, lo:hi]

    project(3 * W, RWKV_U_COLS)
    project(W, 2 * W)
    low_rank = token_shift(3 * W, RWKV_U_COLS)
    xwa = low_rank[:, 0:LANES]
    xg = low_rank[:, LANES:]
    lane = _iota(xwa.shape, 1)
    lr_in = jnp.where(lane < RWKV_DECAY_RANK, jnp.tanh(xwa), xwa)
    lr = _dot(lr_in, wlr_ref[...]) + blr_ref[...]
    log_w = -jnp.exp(-_softplus(-lr[:, 0:W]) - 0.5)
    a = _sigmoid(lr[:, W:])
    g_s[...] = _dot(_sigmoid(xg), gup_ref[...])
    project(0, W)
    k = token_shift(W, 2 * W)

    k_k = pv_ref[0:1, :]
    k_a = pv_ref[1:2, :]
    r_k = pv_ref[2:3, :]
    gn_w = pv_ref[3:4, :]
    gn_b = pv_ref[4:5, :]
    hsum = hsum_ref[...]

    def head_sum(t):
        return jnp.concatenate([_dot(t[:, MXU_TILE * j:MXU_TILE * (j + 1)], hsum)
                                for j in range(W // MXU_TILE)], axis=1)

    kk = k * k_k
    ss = head_sum(kk * kk)
    kk = kk * lax.rsqrt(jnp.maximum(ss, 1e-24))
    kmod = k * (1.0 + (a - 1.0) * k_a)

    lw_s[...] = log_w
    k_s[...] = kmod
    ka_s[...] = -kk
    kb_s[...] = kk * a
    project(2 * W, 3 * W)
    r_s[...] = token_shift(0, W)
    v_s[...] = token_shift(2 * W, 3 * W)

    tri = tri_ref[...]
    gl = RWKV_GROUP_LANES
    n_groups = W // gl
    heads_per_group = gl // RWKV_HEAD_DIM
    row_w = _iota((CHUNK, gl), 0)
    col_w = _iota((CHUNK, gl), 1) & (RWKV_HEAD_DIM - 1)
    strict = row_w > col_w
    incl = row_w >= col_w
    eye_w = jnp.where(row_w == col_w, 1.0, 0.0)
    first_head = _iota((CHUNK, LANES), 1) < RWKV_HEAD_DIM
    head_shift = int(np.log2(RWKV_HEAD_DIM))
    same_head = (_iota((gl, gl), 0) >> head_shift) == (_iota((gl, gl), 1) >> head_shift)
    zeros_tile = jnp.zeros((CHUNK, LANES), F32)

    def blockdiag(t):
        blocks = []
        for hd in range(heads_per_group):
            tile = t[:, LANES * (hd // 2):LANES * (hd // 2 + 1)]
            kept = jnp.where(first_head, tile, 0.0) if hd % 2 == 0 else jnp.where(first_head, 0.0, tile)
            tiles = [kept if j == hd // 2 else zeros_tile for j in range(gl // LANES)]
            blocks.append(jnp.concatenate(tiles, axis=1))
        return jnp.concatenate(blocks, axis=0)

    lanes = [slice(gl * g, gl * (g + 1)) for g in range(n_groups)]

    def state_free(c, ctx):
        c0 = pl.multiple_of(c * CHUNK, CHUNK)
        sl = pl.ds(c0, CHUNK)
        lw = lw_s[sl, :]
        cum = _dot_sel_left(tri, lw, 2)
        e_last = jnp.exp(cum[CHUNK - 1:CHUNK, :])
        e_neg = jnp.exp(-cum)
        rt = r_s[sl, :] * jnp.exp(cum)
        at = ka_s[sl, :] * jnp.exp(cum - lw)
        bt = kb_s[sl, :] * e_neg
        kt = k_s[sl, :] * e_neg
        vv = v_s[sl, :]
        ar = [jnp.concatenate([at[:, ls], rt[:, ls]], axis=0) for ls in lanes]
        bk_bd = [jnp.concatenate([blockdiag(bt[:, ls]), blockdiag(kt[:, ls])], axis=0) for ls in lanes]
        v_bd = [blockdiag(vv[:, ls]) for ls in lanes]
        bk_end = [jnp.concatenate([bt[:, ls], kt[:, ls]], axis=0) * e_last[:, ls] for ls in lanes]
        yield
        gram = [_dot_nt(a, b) for a, b in zip(ar, bk_bd)]
        n_ab = [jnp.where(strict, g[0:CHUNK, 0:gl], 0.0) for g in gram]
        n_ak = [jnp.where(strict, g[0:CHUNK, gl:], 0.0) for g in gram]
        n_rbk = [jnp.concatenate([jnp.where(incl, g[CHUNK:, 0:gl], 0.0), jnp.where(incl, g[CHUNK:, gl:], 0.0)], axis=1)
                 for g in gram]
        yield
        ak_v = [_dot(n, v) for n, v in zip(n_ak, v_bd)]
        prod = [eye_w + n for n in n_ab]
        npow = [_dot(n, blockdiag(n)) for n in n_ab]
        for _ in range(int(np.log2(CHUNK)) - 2):
            yield
            both = [_dot(jnp.concatenate([nk, pr], axis=0), blockdiag(nk)) for nk, pr in zip(npow, prod)]
            prod = [pr + b[CHUNK:] for pr, b in zip(prod, both)]
            npow = [b[0:CHUNK] for b in both]
        yield
        inv = [pr + _dot(pr, blockdiag(nk)) for pr, nk in zip(prod, npow)]
        ctx.update(sl=sl, e_last=e_last, ar=ar, vv=vv, v_bd=v_bd, bk_end=bk_end, ak_v=ak_v, n_rbk=n_rbk, inv=inv)

    def state_bound(ctx, states):
        sl, e_last = ctx["sl"], ctx["e_last"]
        from_state = [_dot_nt(a, s) for a, s in zip(ctx["ar"], states)]
        yield
        u = [_dot(t, blockdiag(f[0:CHUNK] + r)) for t, f, r in zip(ctx["inv"], from_state, ctx["ak_v"])]
        yield
        for g, ls in enumerate(lanes):
            uv = jnp.concatenate([u[g], ctx["vv"][:, ls]], axis=0)
            states[g] = states[g] * e_last[:, ls] + jnp.where(same_head, _dot_tn(uv, ctx["bk_end"][g]), 0.0)
        yield
        for g, ls in enumerate(lanes):
            o_s[sl, ls] = from_state[g][CHUNK:] + _dot(ctx["n_rbk"][g],
                                                       jnp.concatenate([blockdiag(u[g]), ctx["v_bd"][g]], axis=0))

    def chained(*programs):
        for program in programs:
            yield from program

    def chunk_body(i, carry):
        n = RWKV_LOOP_CHUNKS
        states = [st_ref[g] for g in range(n_groups)]
        ctxs = [dict() for _ in range(n)]
        pending = []
        for j in range(0, n, 2):
            frees = [state_free(n * i + j + d, ctxs[j + d]) for d in range(2)]
            _interleave(*frees, *([chained(*pending)] if pending else []))
            pending = [state_bound(ctxs[j + d], states) for d in range(2)]
        _interleave(chained(*pending))
        for g in range(n_groups):
            st_ref[g] = states[g]
        return carry

    lax.fori_loop(0, ts // (CHUNK * RWKV_LOOP_CHUNKS), chunk_body, 0)

    inv_n = 1.0 / RWKV_HEAD_DIM
    for j in range(W // MXU_TILE):
        cs = slice(MXU_TILE * j, MXU_TILE * (j + 1))
        o = o_s[:, cs]
        mean = _dot(o, hsum) * inv_n
        cen = o - mean
        var = _dot(cen * cen, hsum) * inv_n
        o = cen * lax.rsqrt(var + RWKV_GN_EPS) * gn_w[:, cs] + gn_b[:, cs]
        bonus = _dot(r_s[:, cs] * k_s[:, cs] * r_k[:, cs], hsum)
        o = o + bonus * v_s[:, cs]
        o_ref[:, cs] = (o * g_s[:, cs]).astype(o_ref.dtype)


def _hgrn_kernel(x_ref, gain_ref, w_ref, pv_ref, tri_ref, ones_ref, sel_ref, bsel_ref,
                 o_ref, st_ref, h_s, q_s, k_s, v_s, lf_s, o_s, *, ts):
    W = HGRN_WIDTH
    nsub = CHUNK // SUB

    @pl.when(pl.program_id(1) == 0)
    def _():
        st_ref[...] = jnp.zeros(st_ref.shape, F32)

    h = _bf(_rmsnorm(x_ref[...], gain_ref[...]))
    h_s[...] = h
    log_lb = pv_ref[0:1, :]
    log1m_lb = pv_ref[1:2, :]
    gn_w = pv_ref[2:3, :]
    f_pre = _dot(h, w_ref[:, W:2 * W])
    q_pre = _dot(h, w_ref[:, 0:W])
    e = jnp.exp(-jnp.abs(f_pre))
    b = log1m_lb - (jnp.maximum(-f_pre, 0.0) + jnp.log(1.0 + e))
    mx = jnp.maximum(log_lb, b)
    log_f = mx + jnp.log(1.0 + jnp.exp(-jnp.abs(log_lb - b)))
    lf_s[...] = log_f
    k_s[...] = pv_ref[3:4, :] * jnp.where(f_pre > 0.0, e, 1.0) / (1.0 + e)
    v_s[...] = _dot(h, w_ref[:, 2 * W:3 * W])
    q_s[...] = _silu(q_pre)

    tri = tri_ref[...]
    ones_sq = ones_ref[...]
    sel = sel_ref[...]
    sub_shift = int(np.log2(SUB))
    rblk = _iota((CHUNK, CHUNK), 0) >> sub_shift
    cblk = _iota((CHUNK, CHUNK), 1) >> sub_shift
    off_mask = cblk < rblk
    rowblk = _iota((CHUNK, LANES), 0) >> sub_shift
    s_idx = _iota((SUB, LANES), 0)

    heads = range(HGRN_HEADS)
    lanes = [slice(LANES * hd, LANES * (hd + 1)) for hd in heads]
    causal = _iota((CHUNK, CHUNK), 1) <= _iota((CHUNK, CHUNK), 0)

    def fast_chunk(c, states):
        c0 = pl.multiple_of(c * CHUNK, CHUNK)
        sl = pl.ds(c0, CHUNK)
        cum = _dot_sel_left(tri, lf_s[sl, :], 3)
        q = q_s[sl, :]
        k = k_s[sl, :]
        v = v_s[sl, :]
        yield
        starts = [None] + [cum[SUB * i - 1:SUB * i, :] for i in range(1, nsub)]
        cs = jnp.zeros_like(cum)
        rowblk_w = _iota(cum.shape, 0) >> sub_shift
        for i in range(1, nsub):
            cs = jnp.where(rowblk_w == i, starts[i], cs)
        q_sub = q * jnp.exp(cum - cs)
        k_own = k * jnp.exp(cs - cum)
        k_prev = [None] + [k[0:SUB * i, :] * jnp.exp(jnp.minimum(starts[i] - cum[0:SUB * i, :], 0.0))
                           for i in range(1, nsub)]
        yield
        last = cum[CHUNK - 1:CHUNK, :]
        q_in = q * jnp.exp(cum)
        k_end = k * jnp.exp(last - cum)
        e_last = jnp.exp(last)
        before = list(states)
        for hd, ls in enumerate(lanes):
            states[hd] = before[hd] * e_last[:, ls] + _dot_tn(v[:, ls], k_end[:, ls])
        yield
        scores = []
        for n, ls in enumerate(lanes):
            rows = []
            for i in range(nsub):
                parts = [k_own[SUB * i:SUB * (i + 1), ls]]
                if i > 0:
                    parts = [k_prev[i][:, ls]] + parts
                if i + 1 < nsub:
                    parts.append(jnp.zeros((CHUNK - SUB * (i + 1), LANES), F32))
                rows.append(_dot_nt(q_sub[SUB * i:SUB * (i + 1), ls], jnp.concatenate(parts, axis=0)))
            scores.append(jnp.where(causal, jnp.concatenate(rows, axis=0), 0.0))
            if n % 4 == 3:
                yield
        outs = [_dot(a, v[:, ls]) + _dot_nt(q_in[:, ls], s) for a, ls, s in zip(scores, lanes, before)]
        yield
        for hd, ls in enumerate(lanes):
            o_s[sl, ls] = outs[hd]

    def fast_chunk_body(i, carry):
        states = [st_ref[hd] for hd in heads]
        _interleave(*[fast_chunk(HGRN_LOOP_CHUNKS * i + j, states) for j in range(HGRN_LOOP_CHUNKS)])
        for hd in heads:
            st_ref[hd] = states[hd]
        return carry

    def chunk_body(c, carry):
        c0 = pl.multiple_of(c * CHUNK, CHUNK)
        sl = pl.ds(c0, CHUNK)
        cum_all = _dot_sel_left(tri, lf_s[sl, :], 3)
        q_all = q_s[sl, :]
        k_all = k_s[sl, :]
        v_all = v_s[sl, :]
        for hd in range(HGRN_HEADS):
            ls = slice(LANES * hd, LANES * (hd + 1))
            cum = cum_all[:, ls]
            q = q_all[:, ls]
            k = k_all[:, ls]
            v = v_all[:, ls]
            vb = _bf(v)
            last = cum[CHUNK - 1:CHUNK, :]
            starts = [None] + [cum[SUB * i - 1:SUB * i, :] for i in range(1, nsub)]
            cs = jnp.zeros_like(cum)
            for i in range(1, nsub):
                cs = jnp.where(rowblk == i, starts[i], cs)
            q_sub = q * jnp.exp(cum - cs)
            rows = [jnp.zeros((SUB, CHUNK), F32)]
            for i in range(1, nsub):
                k_i = k * jnp.exp(jnp.minimum(starts[i] - cum, 0.0))
                rows.append(_dot_nt(q_sub[SUB * i:SUB * (i + 1), :], k_i))
            a_off = jnp.where(off_mask, jnp.concatenate(rows, axis=0), 0.0)
            state = st_ref[hd]
            o = _dot(a_off, vb) + _dot_nt(q * jnp.exp(cum), state)
            diag = []
            for j in range(nsub):
                rs = slice(SUB * j, SUB * (j + 1))
                cum_b = cum[rs, :]
                q_b = q[rs, :]
                k_b = k[rs, :]
                pieces = []
                for t in range(SUB):
                    expo = jnp.where(s_idx <= t, cum_b[t:t + 1, :] - cum_b, -1e30)
                    pieces.append(jnp.exp(expo) * k_b * q_b[t:t + 1, :])
                pmat = jnp.concatenate(pieces, axis=0)
                score = _dot(pmat, ones_sq)
                wv = score * jnp.concatenate([v[rs, :]] * SUB, axis=0)
                diag.append(_dot(sel, wv))
            o = o + jnp.concatenate(diag, axis=0)
            o_s[sl, ls] = o
            k_end = k * jnp.exp(last - cum)
            st_ref[hd] = state * jnp.exp(last) + _dot_tn(vb, k_end)
        return carry

    sub_sums = _dot_sel_left(bsel_ref[...], log_f, 3)
    fast_ok = jnp.min(sub_sums) >= -HGRN_FAST_LOG_RANGE

    @pl.when(fast_ok)
    def _():
        lax.fori_loop(0, ts // (CHUNK * HGRN_LOOP_CHUNKS), fast_chunk_body, 0)

    @pl.when(jnp.logical_not(fast_ok))
    def _():
        lax.fori_loop(0, ts // CHUNK, chunk_body, 0)

    gate = _sigmoid(_dot(h_s[...], w_ref[:, 3 * W:]))
    for hd in range(HGRN_HEADS):
        ls = slice(LANES * hd, LANES * (hd + 1))
        o = o_s[:, ls]
        o = o * lax.rsqrt(jnp.mean(o * o, axis=-1, keepdims=True) + NORM_EPS) * gn_w[:, ls]
        o_ref[:, ls] = (o * gate[:, ls]).astype(o_ref.dtype)


def _ssm_kernel(x_ref, gain_ref, w_ref, cw_ref, cb_ref, hv_ref, wv_ref, tri_ref,
                o_ref, xb_ref, st_ref, h_s, xs_s, ec_s, xd_s, b_s, c_s, y_s, *, ts):
    W = SSM_WIDTH
    gw = SSM_GROUP_WIDTH
    pairs = W // LANES

    @pl.when(pl.program_id(1) == 0)
    def _():
        xb_ref[0:SUBLANES, :] = jnp.zeros((SUBLANES, SSM_CONV_DIM), F32)
        st_ref[...] = jnp.zeros(st_ref.shape, F32)

    h = _bf(_rmsnorm(x_ref[...], gain_ref[...]))
    h_s[...] = h

    dt_bias = hv_ref[0:1, :]
    neg_a = hv_ref[1:2, :]
    dt = _softplus(_dot(h, w_ref[:, W + SSM_CONV_DIM:]) + dt_bias)
    log_a = dt * neg_a
    tri = tri_ref[...]
    cum = jnp.concatenate([_dot_sel_left(tri, log_a[CHUNK * c:CHUNK * (c + 1), :], 3)
                           for c in range(ts // CHUNK)], axis=0)

    first_head_rows = _iota((ts, LANES), 1) < SSM_HEAD_DIM
    heads_per_group = gw // SSM_HEAD_DIM

    def expand_heads(v, g):
        tiles = []
        for j in range(gw // LANES):
            h0 = heads_per_group * g + 2 * j
            a = jnp.broadcast_to(v[:, h0:h0 + 1], (ts, LANES))
            b = jnp.broadcast_to(v[:, h0 + 1:h0 + 2], (ts, LANES))
            tiles.append(jnp.where(first_head_rows, a, b))
        return jnp.concatenate(tiles, axis=1)

    def project(lo, hi):
        xb_ref[SUBLANES:SUBLANES + ts, lo:hi] = _dot(h, w_ref[:, W + lo:W + hi])

    def conv_silu(lo, hi):
        full = xb_ref[0:ts + SUBLANES, lo:hi]
        conv = cb_ref[:, lo:hi] + full[SUBLANES:, :] * cw_ref[SSM_CONV_WIDTH - 1:SSM_CONV_WIDTH, lo:hi]
        for k in range(1, SSM_CONV_WIDTH):
            j = SSM_CONV_WIDTH - 1 - k
            conv = conv + pltpu.roll(full, k, 0)[SUBLANES:, :] * cw_ref[j:j + 1, lo:hi]
        xb_ref[0:SUBLANES, lo:hi] = xb_ref[ts:ts + SUBLANES, lo:hi]
        return _silu(conv)

    def finish(idx):
        lo, hi = slabs[idx]
        act = conv_silu(lo, hi)
        if idx < SSM_GROUPS:
            xs_s[:, lo:hi] = act
            xd_s[:, lo:hi] = act * expand_heads(dt, idx)
            ec_s[:, lo:hi] = expand_heads(cum, idx)
        elif idx == SSM_GROUPS:
            b_s[...] = act
        else:
            c_s[...] = act

    slabs = [(gw * g, gw * (g + 1)) for g in range(SSM_GROUPS)] + [(W, W + SSM_BC), (W + SSM_BC, W + 2 * SSM_BC)]
    project(*slabs[0])
    for idx in range(len(slabs)):
        if idx + 1 < len(slabs):
            project(*slabs[idx + 1])
        finish(idx)

    t_idx = _iota((CHUNK, W), 0)
    s_idx = _iota((CHUNK, W), 1) & (CHUNK - 1)
    causal = s_idx <= t_idx
    on_diag = s_idx == t_idx
    first_head = (_iota((CHUNK, LANES), 1) < SSM_HEAD_DIM)

    groups = range(SSM_GROUPS)
    glanes = [slice(gw * g, gw * (g + 1)) for g in groups]
    per_group = gw // LANES

    def chunk(c, states):
        c0 = pl.multiple_of(c * CHUNK, CHUNK)
        sl = pl.ds(c0, CHUNK)
        ec = ec_s[sl, :]
        xd = xd_s[sl, :]
        bm = b_s[sl, :]
        cm = c_s[sl, :]
        last = ec[CHUNK - 1:CHUNK, :]
        by_src = jnp.sum(jnp.where(on_diag, ec, 0.0), axis=0, keepdims=True)
        decay = jnp.exp(jnp.where(causal, ec - by_src, -1e30))
        yield
        xw = xd * jnp.exp(last - ec)
        e_in = jnp.exp(ec)
        e_last = jnp.exp(last)
        c_g = [_bf(cm[:, SSM_STATE * g:SSM_STATE * (g + 1)]) for g in groups]
        b_g = [_bf(bm[:, SSM_STATE * g:SSM_STATE * (g + 1)]) for g in groups]
        cb2 = [_dot_nt(c, jnp.concatenate([b, b], axis=0)) for c, b in zip(c_g, b_g)]
        yield
        before = list(states)
        for g, gl in enumerate(glanes):
            states[g] = before[g] * e_last[:, gl] + _dot_tn(b_g[g], xw[:, gl])
        y_in = [_dot(c, s) for c, s in zip(c_g, before)]
        yield
        ys = []
        for p in range(pairs):
            ls = slice(LANES * p, LANES * (p + 1))
            m = decay[:, ls] * cb2[(LANES * p) // gw]
            xp = xd[:, ls]
            x2 = jnp.concatenate([jnp.where(first_head, xp, 0.0), jnp.where(first_head, 0.0, xp)], axis=0)
            ys.append(_dot(m, x2))
            if p % 4 == 3:
                yield
        for g, gl in enumerate(glanes):
            y_s[sl, gl] = (y_in[g] * e_in[:, gl]
                           + jnp.concatenate(ys[per_group * g:per_group * (g + 1)], axis=1))

    def chunk_body(i, carry):
        states = [st_ref[g] for g in groups]
        _interleave(*[chunk(LOOP_CHUNKS * i + j, states) for j in range(LOOP_CHUNKS)])
        for g in groups:
            st_ref[g] = states[g]
        return carry

    lax.fori_loop(0, ts // (CHUNK * LOOP_CHUNKS), chunk_body, 0)

    d_skip = wv_ref[0:1, :]
    gn_w = wv_ref[1:2, :]
    h = h_s[...]
    for g in range(SSM_GROUPS):
        gl = slice(gw * g, gw * (g + 1))
        z = _dot(h, w_ref[:, gl])
        yg = (y_s[:, gl] + d_skip[:, gl] * xs_s[:, gl]) * _silu(z)
        yg = yg * lax.rsqrt(jnp.mean(yg * yg, axis=-1, keepdims=True) + NORM_EPS)
        o_ref[:, gl] = (yg * gn_w[:, gl]).astype(o_ref.dtype)


def _merge_kernel(x_ref, ya_ref, yb_ref, yc_ref, gmix_ref, wg_ref, wb_ref, wout_ref,
                  gffn_ref, wfi_ref, wfo_ref, gfin_ref, o_ref, *, final_norm):
    x = x_ref[...]
    h = _bf(_rmsnorm(x, gmix_ref[...]))
    rows_b = RWKV_WIDTH + HGRN_WIDTH
    merged = (_sigmoid(_dot(h, wg_ref[:, 0:D_MODEL]))
              * jnp.dot(ya_ref[...], wb_ref[0:RWKV_WIDTH, :], preferred_element_type=F32))
    merged = merged + (_sigmoid(_dot(h, wg_ref[:, D_MODEL:2 * D_MODEL]))
                       * jnp.dot(yb_ref[...], wb_ref[RWKV_WIDTH:rows_b, :], preferred_element_type=F32))
    merged = merged + (_sigmoid(_dot(h, wg_ref[:, 2 * D_MODEL:]))
                       * jnp.dot(yc_ref[...], wb_ref[rows_b:, :], preferred_element_type=F32))
    x = x + _dot(merged, wout_ref[...])
    h = _bf(_rmsnorm(x, gffn_ref[...]))
    for lo, hi in ((0, FFN_SPLIT), (FFN_SPLIT, FFN_HIDDEN)):
        gate = _dot(h, wfi_ref[:, lo:hi])
        up = _dot(h, wfi_ref[:, FFN_HIDDEN + lo:FFN_HIDDEN + hi])
        x = x + _dot(_silu(gate) * up, wfo_ref[lo:hi, :])
    if final_norm:
        x = _rmsnorm(x, gfin_ref[...])
    o_ref[...] = x


class _PerLayer:
    def __init__(self, value):
        self.value = value


def _const_spec(const, layer):
    if isinstance(const, _PerLayer):
        shape = const.value.shape[1:]
        return pl.BlockSpec((None,) + shape, lambda b, s, _l=layer, _nd=len(shape): (_l,) + (0,) * _nd,
                            pipeline_mode=pl.Buffered(1))
    return pl.BlockSpec(const.shape, lambda b, s, _nd=const.ndim: (0,) * _nd, pipeline_mode=pl.Buffered(1))


def _const_value(const):
    return const.value if isinstance(const, _PerLayer) else const


def _per_layer(fn, *stacked_args):
    return [_PerLayer(v) for v in jax.vmap(fn)(*stacked_args)]


def _seq_spec(ts, width):
    return pl.BlockSpec((None, ts, width), lambda b, s: (b, s, 0))


def _params():
    return pltpu.CompilerParams(dimension_semantics=("arbitrary", "arbitrary"),
                                vmem_limit_bytes=VMEM_LIMIT_BYTES)


def _run_mixer(kernel_fn, x, layer, consts, out_width, scratch, ts):
    bsz, seq, _ = x.shape
    return pl.pallas_call(
        functools.partial(kernel_fn, ts=ts),
        grid=(bsz, seq // ts),
        in_specs=[_seq_spec(ts, D_MODEL)] + [_const_spec(c, layer) for c in consts],
        out_specs=_seq_spec(ts, out_width),
        out_shape=jax.ShapeDtypeStruct((bsz, seq, out_width), BF16),
        scratch_shapes=scratch,
        compiler_params=_params(),
    )(x, *[_const_value(c) for c in consts])


def _row(v):
    return v.astype(F32).reshape(1, -1)


def _pad_rows(m, rows):
    return jnp.concatenate([m, jnp.zeros((rows - m.shape[0], m.shape[1]), m.dtype)], axis=0)


def _pad_cols(m, width):
    return jnp.concatenate([m, jnp.zeros((m.shape[0], width - m.shape[1]), m.dtype)], axis=1)


def _rows8(rows, width):
    m = jnp.concatenate([_row(r) for r in rows], axis=0)
    return _pad_rows(_pad_cols(m, width), SUBLANES)


def _head_sum_matrix(width, head_dim):
    idx = np.arange(width) // head_dim
    return jnp.asarray(idx[:, None] == idx[None, :], dtype=BF16)


def _tri_ones(n):
    return jnp.asarray(np.tril(np.ones((n, n), np.float32)), dtype=BF16)


def _rwkv_layer_consts(gain, w_in, mu, w0, w_up, a0, a_up, g_up, k_k, k_a, r_k, gn_w, gn_b):
    W = RWKV_WIDTH
    dr, ir, gr = RWKV_DECAY_RANK, RWKV_ICL_RANK, RWKV_GATE_RANK
    w = _pad_cols(w_in, RWKV_U_COLS).astype(BF16)
    mu_p = _pad_cols(_row(mu), RWKV_U_COLS)
    wlr = jnp.concatenate([jnp.concatenate([w_up, jnp.zeros((dr, W), w_up.dtype)], axis=1),
                           jnp.concatenate([jnp.zeros((ir, W), a_up.dtype), a_up], axis=1)], axis=0)
    wlr = _pad_rows(wlr, LANES).astype(BF16)
    blr = jnp.concatenate([_row(w0), _row(a0)], axis=1)
    gup = _pad_rows(g_up, RWKV_GATE_PAD).astype(BF16)
    pv = _rows8([k_k, k_a, r_k.reshape(-1), gn_w, gn_b], W)
    return _row(gain), w, mu_p, wlr, blr, gup, pv


def _rwkv_branch(x, layer, layer_consts, ts):
    W = RWKV_WIDTH
    consts = layer_consts + [_head_sum_matrix(MXU_TILE, RWKV_HEAD_DIM), _tri_ones(CHUNK)]
    scratch = ([pltpu.VMEM((ts + SUBLANES, RWKV_U_COLS), F32),
                pltpu.VMEM((W // RWKV_GROUP_LANES, RWKV_GROUP_LANES, RWKV_GROUP_LANES), F32)]
               + [pltpu.VMEM((ts, W), F32) for _ in range(8)])
    return _run_mixer(_rwkv_kernel, x, layer, consts, W, scratch, ts)


def _hgrn_layer_consts(gain, w_in, lb, gn_w):
    lb = lb.astype(F32)
    pv = _rows8([jnp.log(lb), jnp.log1p(-lb), gn_w, 1.0 - lb], HGRN_WIDTH)
    return _row(gain), w_in.astype(BF16), pv


def _hgrn_branch(x, layer, layer_consts, ts):
    W = HGRN_WIDTH
    t_i = np.arange(SUB)[:, None]
    pair = np.arange(SUB * SUB)[None, :]
    sel = jnp.asarray((pair // SUB == t_i) & (pair % SUB <= t_i), dtype=BF16)
    bsel = jnp.asarray(np.arange(ts // SUB)[:, None] == (np.arange(ts) // SUB)[None, :], dtype=BF16)
    consts = layer_consts + [_tri_ones(CHUNK), jnp.ones((LANES, LANES), BF16), sel, bsel]
    scratch = ([pltpu.VMEM((HGRN_HEADS, HGRN_HEAD_DIM, HGRN_HEAD_DIM), F32), pltpu.VMEM((ts, D_MODEL), BF16)]
               + [pltpu.VMEM((ts, W), F32) for _ in range(5)])
    return _run_mixer(_hgrn_kernel, x, layer, consts, W, scratch, ts)


def _ssm_layer_consts(gain, w_in, conv_w, conv_b, dt_bias, a_log, d_skip_lanes, gn_w):
    w = _pad_cols(w_in, SSM_U_COLS).astype(BF16)
    cw = _pad_rows(conv_w.astype(F32).T, SUBLANES)
    hv = _rows8([dt_bias, -jnp.exp(a_log.astype(F32))], LANES)
    wv = _rows8([d_skip_lanes, gn_w], SSM_WIDTH)
    return _row(gain), w, cw, _row(conv_b), hv, wv


def _ssm_branch(x, layer, layer_consts, ts):
    W = SSM_WIDTH
    consts = layer_consts + [_tri_ones(CHUNK)]
    scratch = [pltpu.VMEM((ts + SUBLANES, SSM_CONV_DIM), F32),
               pltpu.VMEM((SSM_GROUPS, SSM_STATE, SSM_GROUP_WIDTH), F32),
               pltpu.VMEM((ts, D_MODEL), BF16), pltpu.VMEM((ts, W), F32),
               pltpu.VMEM((ts, W), F32), pltpu.VMEM((ts, W), F32),
               pltpu.VMEM((ts, SSM_BC), F32), pltpu.VMEM((ts, SSM_BC), F32),
               pltpu.VMEM((ts, W), F32)]
    return _run_mixer(_ssm_kernel, x, layer, consts, W, scratch, ts)


def _merge_layer_consts(gain_mix, w_gate, w_branch, w_out, gain_ffn, w_ffn_in, w_ffn_out):
    return (_row(gain_mix), w_gate.astype(BF16), w_branch.astype(BF16), w_out.astype(BF16), _row(gain_ffn),
            w_ffn_in.astype(BF16), w_ffn_out.astype(BF16))


def _merge_ffn(x, ya, yb, yc, layer, layer_consts, gain_final, final_norm, ts):
    bsz, seq, _ = x.shape
    consts = layer_consts + [_row(gain_final)]
    seqs = [x, ya, yb, yc]
    return pl.pallas_call(
        functools.partial(_merge_kernel, final_norm=final_norm),
        grid=(bsz, seq // ts),
        in_specs=[_seq_spec(ts, a.shape[-1]) for a in seqs] + [_const_spec(c, layer) for c in consts],
        out_specs=_seq_spec(ts, D_MODEL),
        out_shape=jax.ShapeDtypeStruct((bsz, seq, D_MODEL), F32),
        compiler_params=_params(),
    )(*seqs, *[_const_value(c) for c in consts])


def kernel(x, norm_mix, w_in, rwkv_mu, rwkv_w0, rwkv_w_up, rwkv_a0, rwkv_a_up, rwkv_g_up, rwkv_k_k, rwkv_k_a,
           rwkv_r_k, rwkv_gn_w, rwkv_gn_b, hgrn_lb_logits, hgrn_gn_w, ssm_conv_w, ssm_conv_b, ssm_dt_bias,
           ssm_a_log, ssm_d, ssm_gn_w, w_branch, w_out, norm_ffn, w_ffn_in, w_ffn_out, norm_final):
    bsz, seq, d = x.shape
    assert d == D_MODEL
    depth = w_in.shape[0]
    ts = min(SEQ_BLOCK, seq)
    tm = min(MERGE_BLOCK, seq)
    assert seq % ts == 0 and seq % tm == 0
    assert all(ts % (CHUNK * n) == 0 for n in (LOOP_CHUNKS, HGRN_LOOP_CHUNKS, RWKV_LOOP_CHUNKS))

    rwkv_cols = 3 * RWKV_WIDTH + RWKV_DECAY_RANK + RWKV_ICL_RANK + RWKV_GATE_RANK
    off_hgrn = rwkv_cols
    off_ssm = off_hgrn + 4 * HGRN_WIDTH
    off_gate = off_ssm + SSM_WIDTH + SSM_CONV_DIM + SSM_HEADS

    cs = jnp.cumsum(jax.nn.softmax(hgrn_lb_logits.astype(F32), axis=0), axis=0)
    lbs = cs - cs[:1]

    rwkv_consts = _per_layer(_rwkv_layer_consts, norm_mix, w_in[:, :, :off_hgrn], rwkv_mu, rwkv_w0, rwkv_w_up,
                             rwkv_a0, rwkv_a_up, rwkv_g_up, rwkv_k_k, rwkv_k_a, rwkv_r_k, rwkv_gn_w, rwkv_gn_b)
    hgrn_consts = _per_layer(_hgrn_layer_consts, norm_mix, w_in[:, :, off_hgrn:off_ssm], lbs, hgrn_gn_w)
    ssm_consts = _per_layer(_ssm_layer_consts, norm_mix, w_in[:, :, off_ssm:off_gate], ssm_conv_w, ssm_conv_b,
                            ssm_dt_bias, ssm_a_log, jnp.repeat(ssm_d.astype(F32), SSM_HEAD_DIM, axis=1), ssm_gn_w)
    merge_consts = _per_layer(_merge_layer_consts, norm_mix, w_in[:, :, off_gate:], w_branch, w_out, norm_ffn,
                              w_ffn_in, w_ffn_out)

    x = x.astype(F32)
    for l in range(depth):
        ya = _rwkv_branch(x, l, rwkv_consts, ts)
        yb = _hgrn_branch(x, l, hgrn_consts, ts)
        yc = _ssm_branch(x, l, ssm_consts, ts)
        x = _merge_ffn(x, ya, yb, yc, l, merge_consts, norm_final, l == depth - 1, tm)
    return x
```
